```python
import math
import jax, jax.numpy as jnp
from jax import lax
import numpy as np

D_MODEL = 1024
BATCH = 8
SEQ = 2048
DEPTH = 2

HEAD_DIM = 64
A_WIDTH = D_MODEL // 2
A_HEADS = A_WIDTH // (2 * HEAD_DIM)
POOL_WINDOWS = (2, 4, 8, 16)
POOL_WIDTH = D_MODEL // 4
POOL_GROUPS = len(POOL_WINDOWS)
POOL_GROUP_DIM = POOL_WIDTH // POOL_GROUPS
C_WIDTH = D_MODEL // 4
C_HEADS = C_WIDTH // HEAD_DIM
DILATED_PAIRS = ((128, 1), (512, 4), (2048, 16))
IN_WIDTH = 3 * A_WIDTH + POOL_WIDTH + 3 * C_WIDTH
BLOCK = 128
ROPE_THETA = 10000.0
N_GROUPS = 4
EXPERTS_PER_GROUP = 8
TOP_K = 2
EXPERT_HIDDEN = D_MODEL // 4
N_MOD = 6
EPS = 1e-6

kernel_name = 'hymba_style_diffattn_pool_dilated_hmoe'


def rms_norm(x, g):
    xf = x.astype(jnp.float32)
    y = xf * lax.rsqrt(jnp.mean(xf * xf, axis=-1, keepdims=True) + EPS)
    return (y * g.astype(jnp.float32)).astype(x.dtype)


def rope_tables(seq, dim):
    inv = 1.0 / (ROPE_THETA ** (jnp.arange(0, dim, 2, dtype=jnp.float32) / dim))
    ang = jnp.arange(seq, dtype=jnp.float32)[:, None] * inv[None, :]
    ang = jnp.concatenate([ang, ang], axis=-1)
    return jnp.cos(ang), jnp.sin(ang)


def rope(t, cos, sin):
    half = t.shape[-1] // 2
    tf = t.astype(jnp.float32)
    rot = jnp.concatenate([-tf[..., half:], tf[..., :half]], axis=-1)
    return (tf * cos + rot * sin).astype(t.dtype)


def diff_attention(q, k, v, lam):
    B, H, _, S, dh = q.shape
    nb = S // BLOCK
    kpos = jnp.arange(S)
    scale = dh ** -0.5

    def block(i):
        qb = lax.dynamic_slice_in_dim(q, i * BLOCK, BLOCK, axis=3)
        s = jnp.einsum('bhmqd,bhmkd->bhmqk', qb, k, preferred_element_type=jnp.float32) * scale
        qpos = i * BLOCK + jnp.arange(BLOCK)
        s = jnp.where(kpos[None, :] <= qpos[:, None], s, -jnp.inf)
        p = jax.nn.softmax(s, axis=-1)
        a = p[:, :, 0] - lam * p[:, :, 1]
        return jnp.einsum('bhqk,bhkd->bhqd', a.astype(v.dtype), v)

    o = lax.map(block, jnp.arange(nb))
    return o.transpose(1, 2, 0, 3, 4).reshape(B, H, S, v.shape[-1])


def banded_causal_attention(q, k, v, w):
    lead = list(q.shape[:-2])
    L, dh = q.shape[-2], q.shape[-1]
    nb = -(-L // BLOCK)
    pad = nb * BLOCK - L
    padcfg = [(0, 0)] * len(lead) + [(0, pad), (0, 0)]

    def blocks(t):
        return jnp.pad(t, padcfg).reshape(*lead, nb, BLOCK, dh)

    def with_prev(t):
        prev = jnp.concatenate([jnp.zeros_like(t[..., :1, :, :]), t[..., :-1, :, :]], axis=-3)
        return jnp.concatenate([prev, t], axis=-2)

    qb = blocks(q)
    kb = with_prev(blocks(k))
    vb = with_prev(blocks(v))
    s = jnp.einsum('...nqd,...nkd->...nqk', qb, kb, preferred_element_type=jnp.float32) * (dh ** -0.5)
    a = jnp.arange(BLOCK)[:, None]
    b = jnp.arange(2 * BLOCK)[None, :]
    dist = a + BLOCK - b
    band = (dist >= 0) & (dist <= w)
    first = (jnp.arange(nb)[:, None, None] == 0) & (b[None] < BLOCK)
    mask = band[None] & jnp.logical_not(first)
    s = jnp.where(mask, s, -jnp.inf)
    lse = jax.nn.logsumexp(s, axis=-1)
    p = jnp.exp(s - lse[..., None])
    o = jnp.einsum('...nqk,...nkd->...nqd', p.astype(v.dtype), vb)
    o = o.reshape(*lead, nb * BLOCK, dh)[..., :L, :]
    lse = lse.reshape(*lead, nb * BLOCK)[..., :L]
    return o, lse


def dilated_branch(q, k, v, window, dil):
    B, H, S, dh = q.shape
    L = S // dil

    def sub(t):
        return t.reshape(B, H, L, dil, dh).transpose(0, 1, 3, 2, 4)

    o, lse = banded_causal_attention(sub(q), sub(k), sub(v), window // dil)
    o = o.transpose(0, 1, 3, 2, 4).reshape(B, H, S, dh)
    lse = lse.transpose(0, 1, 3, 2).reshape(B, H, S)
    return o, lse


def dilated_mixture(q, k, v):
    outs, lses = [], []
    for window, dil in DILATED_PAIRS:
        o, l = dilated_branch(q, k, v, window, dil)
        outs.append(o)
        lses.append(l)
    wts = jax.nn.softmax(jnp.stack(lses, axis=0), axis=0)
    out = jnp.sum(wts[..., None] * jnp.stack(outs, axis=0).astype(jnp.float32), axis=0)
    return out.astype(q.dtype)


def pool_mixer(u, w_pool, b_pool, scale):
    B, S, _ = u.shape
    uf = u.astype(jnp.float32).reshape(B, S, POOL_GROUPS, POOL_GROUP_DIM)
    cs = jnp.concatenate([jnp.zeros((B, 1, POOL_GROUPS, POOL_GROUP_DIM), jnp.float32),
                          jnp.cumsum(uf, axis=1)], axis=1)
    t = jnp.arange(S)
    win = jnp.array(POOL_WINDOWS, dtype=jnp.int32)
    lo = jnp.maximum(t[:, None] + 1 - win[None, :], 0)
    cnt = (t[:, None] + 1 - lo).astype(jnp.float32)
    gidx = jnp.arange(POOL_GROUPS)
    window_sum = cs[:, 1:] - cs[:, lo, gidx]
    d = window_sum / cnt[None, :, :, None] - uf
    y = jnp.einsum('bsgc,gce->bsge', d, w_pool.astype(jnp.float32)) + b_pool.astype(jnp.float32)
    return (y.reshape(B, S, POOL_WIDTH) * scale.astype(jnp.float32)).astype(u.dtype)


def hier_moe(h, w_rg, b_rg, w_re, b_re, w1, w3, w2):
    B, S, D = h.shape
    t = h.reshape(-1, D)
    gl = (t @ w_rg).astype(jnp.float32) + b_rg.astype(jnp.float32)
    gp = jax.nn.softmax(gl, axis=-1)
    g_idx = jnp.argmax(gl, axis=-1)
    g_w = jnp.take_along_axis(gp, g_idx[:, None], axis=1)[:, 0]
    el = jnp.einsum('td,gde->tge', t, w_re).astype(jnp.float32) + b_re.astype(jnp.float32)
    el_sel = jnp.take_along_axis(el, g_idx[:, None, None], axis=1)[:, 0]
    top_v, top_i = lax.top_k(el_sel, TOP_K)
    top_w = jax.nn.softmax(top_v, axis=-1) * g_w[:, None]
    e_w = jnp.sum(jax.nn.one_hot(top_i, EXPERTS_PER_GROUP, dtype=jnp.float32) * top_w[..., None], axis=1)
    combine = jax.nn.one_hot(g_idx, N_GROUPS, dtype=jnp.float32)[:, :, None] * e_w[:, None, :]
    y = jnp.zeros((t.shape[0], D), jnp.float32)
    for g in range(N_GROUPS):
        hid = jax.nn.silu(jnp.einsum('td,edf->tef', t, w1[g])) * jnp.einsum('td,edf->tef', t, w3[g])
        hid = hid * combine[:, g, :, None].astype(hid.dtype)
        y = y + jnp.einsum('tef,efd->td', hid, w2[g]).astype(jnp.float32)
    return y.reshape(B, S, D).astype(h.dtype)


def setup_inputs(seed: int = 0) -> dict:
    key = jax.random.key(seed)
    ks = jax.random.split(key, 24)
    L, D = DEPTH, D_MODEL
    G, E, F = N_GROUPS, EXPERTS_PER_GROUP, EXPERT_HIDDEN

    def nrm(k, shape, scale):
        return jax.random.normal(k, shape, jnp.float32) * scale

    return {
        'x': nrm(ks[0], (BATCH, SEQ, D), 1.0),
        'c': nrm(ks[1], (BATCH, D), 1.0),
        'w_mod': nrm(ks[2], (L, D, N_MOD * D), 0.5 * D ** -0.5),
        'b_mod': nrm(ks[3], (L, N_MOD * D), 0.01),
        'g_norm1': 1.0 + nrm(ks[4], (L, D), 0.1),
        'w_in': nrm(ks[5], (L, D, IN_WIDTH), D ** -0.5),
        'gq_a': 1.0 + nrm(ks[6], (L, HEAD_DIM), 0.1),
        'gk_a': 1.0 + nrm(ks[7], (L, HEAD_DIM), 0.1),
        'lam_a': nrm(ks[8], (L, 4, HEAD_DIM), 0.1),
        'g_sub_a': 1.0 + nrm(ks[9], (L, 2 * HEAD_DIM), 0.1),
        'w_pool': nrm(ks[10], (L, POOL_GROUPS, POOL_GROUP_DIM, POOL_GROUP_DIM), POOL_GROUP_DIM ** -0.5),
        'b_pool': nrm(ks[11], (L, POOL_GROUPS, POOL_GROUP_DIM), 0.01),
        'pool_scale': 1.0 + nrm(ks[12], (L, POOL_WIDTH), 0.1),
        'gq_c': 1.0 + nrm(ks[13], (L, HEAD_DIM), 0.1),
        'gk_c': 1.0 + nrm(ks[14], (L, HEAD_DIM), 0.1),
        'w_out': nrm(ks[15], (L, D, D), D ** -0.5),
        'g_norm2': 1.0 + nrm(ks[16], (L, D), 0.1),
        'w_rg': nrm(ks[17], (L, D, G), D ** -0.5),
        'b_rg': nrm(ks[18], (L, G), 0.01),
        'w_re': nrm(ks[19], (L, G, D, E), D ** -0.5),
        'b_re': nrm(ks[20], (L, G, E), 0.01),
        'w1': nrm(ks[21], (L, G, E, D, F), D ** -0.5),
        'w3': nrm(ks[22], (L, G, E, D, F), D ** -0.5),
        'w2': nrm(ks[23], (L, G, E, F, D), F ** -0.5),
    }


def reference(x, c, w_mod, b_mod, g_norm1, w_in, gq_a, gk_a, lam_a, g_sub_a, w_pool, b_pool,
              pool_scale, gq_c, gk_c, w_out, g_norm2, w_rg, b_rg, w_re, b_re, w1, w3, w2):
    B, S, D = x.shape
    cos, sin = rope_tables(S, HEAD_DIM)
    cond = jax.nn.silu(c)
    cuts = [A_WIDTH, 2 * A_WIDTH, 3 * A_WIDTH, 3 * A_WIDTH + POOL_WIDTH,
            3 * A_WIDTH + POOL_WIDTH + C_WIDTH, 3 * A_WIDTH + POOL_WIDTH + 2 * C_WIDTH]
    for l in range(DEPTH):
        mod = cond @ w_mod[l] + b_mod[l]
        sh1, sc1, ga1, sh2, sc2, ga2 = [m[:, None, :] for m in jnp.split(mod, N_MOD, axis=-1)]

        h = rms_norm(x, g_norm1[l]) * (1.0 + sc1) + sh1
        z = h @ w_in[l]
        qa, ka, va, ub, qc, kc, vc = jnp.split(z, cuts, axis=-1)

        qa = qa.reshape(B, S, A_HEADS, 2, HEAD_DIM).transpose(0, 2, 3, 1, 4)
        ka = ka.reshape(B, S, A_HEADS, 2, HEAD_DIM).transpose(0, 2, 3, 1, 4)
        va = va.reshape(B, S, A_HEADS, 2 * HEAD_DIM).transpose(0, 2, 1, 3)
        qa = rope(rms_norm(qa, gq_a[l]), cos, sin)
        ka = rope(rms_norm(ka, gk_a[l]), cos, sin)
        lam_init = 0.8 - 0.6 * math.exp(-0.3 * l)
        lp = lam_a[l].astype(jnp.float32)
        lam = jnp.exp(jnp.sum(lp[0] * lp[1])) - jnp.exp(jnp.sum(lp[2] * lp[3])) + lam_init
        ya = diff_attention(qa, ka, va, lam)
        ya = (rms_norm(ya, g_sub_a[l]) * (1.0 - lam_init)).transpose(0, 2, 1, 3).reshape(B, S, A_WIDTH)

        yb = pool_mixer(ub, w_pool[l], b_pool[l], pool_scale[l])

        qc = qc.reshape(B, S, C_HEADS, HEAD_DIM).transpose(0, 2, 1, 3)
        kc = kc.reshape(B, S, C_HEADS, HEAD_DIM).transpose(0, 2, 1, 3)
        vc = vc.reshape(B, S, C_HEADS, HEAD_DIM).transpose(0, 2, 1, 3)
        qc = rope(rms_norm(qc, gq_c[l]), cos, sin)
        kc = rope(rms_norm(kc, gk_c[l]), cos, sin)
        yc = dilated_mixture(qc, kc, vc).transpose(0, 2, 1, 3).reshape(B, S, C_WIDTH)

        y = jnp.concatenate([ya, yb, yc], axis=-1) @ w_out[l]
        x = x + ga1 * y

        h = rms_norm(x, g_norm2[l]) * (1.0 + sc2) + sh2
        x = x + ga2 * hier_moe(h, w_rg[l], b_rg[l], w_re[l], b_re[l], w1[l], w3[l], w2[l])
    return x
```

```python
import functools
import math

import jax
import jax.numpy as jnp
import numpy as np
from jax import lax
from jax.experimental import pallas as pl
from jax.experimental.pallas import tpu as pltpu

HEAD_DIM = 64
POOL_WINDOWS = (2, 4, 8, 16)
DILATED_PAIRS = ((128, 1), (512, 4), (2048, 16))
ROPE_THETA = 10000.0
N_GROUPS = 4
EXPERTS_PER_GROUP = 8
N_EXPERTS = N_GROUPS * EXPERTS_PER_GROUP
N_MOD = 6
EPS = 1e-6

LANES = 128
MXU_DIM = 256
VMEM_LIMIT = 48 * 1024 * 1024
NEG = -1e30
F32 = jnp.float32
BF16 = jnp.bfloat16


def _cparams(sem):
    return pltpu.CompilerParams(dimension_semantics=sem, vmem_limit_bytes=VMEM_LIMIT)


def _dot(a, b):
    return jnp.dot(a, b, preferred_element_type=F32)


def _dot_nt(a, b):
    return lax.dot_general(a, b, (((1,), (1,)), ((), ())), preferred_element_type=F32)


def _mod_kernel(c_ref, w_ref, b_ref, o_ref):
    c = c_ref[...]
    cond = c * (1.0 / (1.0 + jnp.exp(-c)))
    o_ref[0] = _dot(cond.astype(BF16), w_ref[0].astype(BF16)) + b_ref[0]


def _modulation(c, w_mod, b_mod):
    L, D, N = w_mod.shape
    B = c.shape[0]
    tn = 1024
    return pl.pallas_call(
        _mod_kernel,
        grid=(L, N // tn),
        in_specs=[pl.BlockSpec((B, D), lambda l, j: (0, 0)),
                  pl.BlockSpec((1, D, tn), lambda l, j: (l, 0, j)),
                  pl.BlockSpec((1, 1, tn), lambda l, j: (l, 0, j))],
        out_specs=pl.BlockSpec((1, B, tn), lambda l, j: (l, 0, j)),
        out_shape=jax.ShapeDtypeStruct((L, B, N), F32),
        compiler_params=_cparams(("parallel", "parallel")),
        name="modulation",
    )(c, w_mod, b_mod.reshape(L, 1, N))


def _inproj_kernel(norm_chunks, x_ref, sc_ref, sh_ref, g_ref, w_ref, gain_ref, cos_ref, sa_ref,
                   sb_ref, bd_ref, z_ref):
    x = x_ref[0]
    ms = jnp.mean(x * x, axis=-1, keepdims=True)
    h = x * lax.rsqrt(ms + EPS) * g_ref[...]
    h = h * (1.0 + sc_ref[0]) + sh_ref[0]
    hb = h.astype(BF16)
    W = MXU_DIM
    for j in range(w_ref.shape[1] // W):
        zc = _dot(hb, w_ref[:, j * W:(j + 1) * W])
        if j in norm_chunks:
            msq = _dot((zc * zc).astype(BF16), bd_ref[...])
            y = zc * lax.rsqrt(msq + EPS) * gain_ref[:, j * W:(j + 1) * W]
            r_up = pltpu.roll(y, HEAD_DIM // 2, 1)
            r_dn = pltpu.roll(y, W - HEAD_DIM // 2, 1)
            zc = y * cos_ref[...] + r_up * sa_ref[...] + r_dn * sb_ref[...]
        z_ref[0, :, j * W:(j + 1) * W] = zc.astype(BF16)


def _in_projection(x, sc, sh, g, w_bf, gain, cos_t, sa_t, sb_t, bd, norm_chunks, tm=512):
    B, S, D = x.shape
    N = w_bf.shape[1]
    W = MXU_DIM
    return pl.pallas_call(
        functools.partial(_inproj_kernel, norm_chunks),
        grid=(B, S // tm),
        in_specs=[pl.BlockSpec((1, tm, D), lambda b, i: (b, i, 0)),
                  pl.BlockSpec((1, 1, D), lambda b, i: (b, 0, 0)),
                  pl.BlockSpec((1, 1, D), lambda b, i: (b, 0, 0)),
                  pl.BlockSpec((1, D), lambda b, i: (0, 0)),
                  pl.BlockSpec((D, N), lambda b, i: (0, 0)),
                  pl.BlockSpec((1, N), lambda b, i: (0, 0)),
                  pl.BlockSpec((tm, W), lambda b, i: (i, 0)),
                  pl.BlockSpec((tm, W), lambda b, i: (i, 0)),
                  pl.BlockSpec((tm, W), lambda b, i: (i, 0)),
                  pl.BlockSpec((W, W), lambda b, i: (0, 0))],
        out_specs=pl.BlockSpec((1, tm, N), lambda b, i: (b, i, 0)),
        out_shape=jax.ShapeDtypeStruct((B, S, N), BF16),
        compiler_params=_cparams(("parallel", "parallel")),
        name="in_projection",
    )(x, sc, sh, g, w_bf, gain, cos_t, sa_t, sb_t, bd)


def _attn_kernel(mode, bq, bk, lam_init, *refs):
    if mode == "diff":
        q_ref, k_ref, v_ref, lam_ref, g_ref, o_ref = refs
    else:
        q_ref, k_ref, v_ref, cnt_ref, bias_ref, o_ref = refs
    S = q_ref.shape[1]
    nq = S // bq
    lane = lax.broadcasted_iota(jnp.int32, (1, LANES), 1)
    lo = lane < HEAD_DIM

    def step(ql, qh, ki, carry, bias, cnt):
        m1, l1, a1, m2, l2, a2 = carry
        ks = pl.multiple_of(ki * bk, bk)
        k = k_ref[0, pl.ds(ks, bk), :]
        v = v_ref[0, pl.ds(ks, bk), :]
        out = []
        for q, m, l, a in ((ql, m1, l1, a1), (qh, m2, l2, a2)):
            s = _dot_nt(q, k)
            if bias is not None:
                s = s + bias
            mn = jnp.maximum(m, jnp.max(s, axis=-1, keepdims=True))
            alpha = jnp.exp(m - mn)
            p = jnp.exp(s - mn)
            if cnt is not None:
                p = p * cnt
            l = alpha * l + jnp.sum(p, axis=-1, keepdims=True)
            a = alpha * a + _dot(p.astype(BF16), v)
            out += [mn, l, a]
        return tuple(out)

    def q_block(qi, _):
        qs = pl.multiple_of(qi * bq, bq)
        q = q_ref[0, pl.ds(qs, bq), :]
        zero = jnp.zeros_like(q)
        ql = jnp.where(lo, q, zero)
        qh = jnp.where(lo, zero, q)
        init = (jnp.full((bq, 1), NEG, F32), jnp.zeros((bq, 1), F32), jnp.zeros((bq, LANES), F32)) * 2
        if mode == "diff":
            carry = lax.fori_loop(0, qi, lambda ki, c: step(ql, qh, ki, c, None, None), init)
            row = lax.broadcasted_iota(jnp.int32, (bq, bk), 0)
            col = lax.broadcasted_iota(jnp.int32, (bq, bk), 1)
            causal = jnp.where(col <= row, 0.0, NEG).astype(F32)
            m1, l1, a1, m2, l2, a2 = step(ql, qh, qi, carry, causal, None)
            lp = lam_ref[...]
            lam = (jnp.exp(jnp.sum(lp[0:1] * lp[1:2], axis=-1, keepdims=True))
                   - jnp.exp(jnp.sum(lp[2:3] * lp[3:4], axis=-1, keepdims=True)) + lam_init)
            y = a1 / l1 - lam * (a2 / l2)
            ms = jnp.mean(y * y, axis=-1, keepdims=True)
            y = y * lax.rsqrt(ms + EPS) * g_ref[...] * (1.0 - lam_init)
        else:
            def body(ki, c):
                d = qi - ki
                return step(ql, qh, ki, c, bias_ref[d], cnt_ref[d])
            m1, l1, a1, m2, l2, a2 = lax.fori_loop(0, qi + 1, body, init)
            y = jnp.where(lo, a1 / l1, a2 / l2)
        o_ref[0, pl.ds(qs, bq), :] = y.astype(o_ref.dtype)
        return 0

    lax.fori_loop(0, nq, q_block, 0)


def _attention(mode, z, q_col, k_col, v_col, n_blocks, extras, extra_specs, lam_init=0.0,
               bq=256, bk=256):
    B, S, _ = z.shape
    zspec = lambda col: pl.BlockSpec((1, S, LANES), lambda b, h: (b, 0, col + h))
    return pl.pallas_call(
        functools.partial(_attn_kernel, mode, bq, bk, lam_init),
        grid=(B, n_blocks),
        in_specs=[zspec(q_col), zspec(k_col), zspec(v_col)] + extra_specs,
        out_specs=pl.BlockSpec((1, S, LANES), lambda b, h: (b, 0, h)),
        out_shape=jax.ShapeDtypeStruct((B, S, n_blocks * LANES), BF16),
        compiler_params=_cparams(("parallel", "parallel")),
        name=mode + "_attention",
    )(z, z, z, *extras)


def _dilated_tables(S, bq, bk):
    nd = S // bk
    dist = (np.arange(nd)[:, None, None] * bk + np.arange(bq)[None, :, None]
            - np.arange(bk)[None, None, :])
    cnt = np.zeros(dist.shape, np.float32)
    for window, dil in DILATED_PAIRS:
        cnt += (dist >= 0) & (dist <= window) & (dist % dil == 0)
    bias = np.where(cnt > 0, 0.0, NEG).astype(np.float32)
    return jnp.asarray(cnt), jnp.asarray(bias)


def _pool_kernel(u_ref, w_ref, b_ref, sc_ref, o_ref):
    u = u_ref[0].astype(F32)
    S, C = u.shape
    row = lax.broadcasted_iota(jnp.int32, (S, C), 0)
    lane = lax.broadcasted_iota(jnp.int32, (S, C), 1)
    grp = lane // (C // len(POOL_WINDOWS))

    def shifted(a, k):
        return jnp.where(row >= k, pltpu.roll(a, k, 0), 0.0)

    acc = u
    win = jnp.zeros_like(u)
    width = 1
    for gi, w in enumerate(POOL_WINDOWS):
        while width < w:
            acc = acc + shifted(acc, width)
            width *= 2
        win = jnp.where(grp == gi, acc, win)
    wl = jnp.zeros_like(row)
    for gi, w in enumerate(POOL_WINDOWS):
        wl = jnp.where(grp == gi, w, wl)
    cnt = jnp.minimum(row + 1, wl).astype(F32)
    d = win / cnt - u
    y = _dot(d.astype(BF16), w_ref[...]) + b_ref[...]
    o_ref[0] = (y * sc_ref[...]).astype(o_ref.dtype)


def _pool_mixer(z, col_block, w_bd, b, scale):
    B, S, _ = z.shape
    C = w_bd.shape[0]
    return pl.pallas_call(
        _pool_kernel,
        grid=(B,),
        in_specs=[pl.BlockSpec((1, S, C), lambda b_: (b_, 0, col_block)),
                  pl.BlockSpec((C, C), lambda b_: (0, 0)),
                  pl.BlockSpec((1, C), lambda b_: (0, 0)),
                  pl.BlockSpec((1, C), lambda b_: (0, 0))],
        out_specs=pl.BlockSpec((1, S, C), lambda b_: (b_, 0, 0)),
        out_shape=jax.ShapeDtypeStruct((B, S, C), BF16),
        compiler_params=_cparams(("parallel",)),
        name="pool_mixer",
    )(z, w_bd, b, scale)


def _outproj_kernel(ya_ref, yb_ref, yc_ref, x_ref, wo_ref, ga_ref, sc_ref, sh_ref, g_ref, wr_ref,
                    br_ref, x1_ref, h_ref, comb_ref):
    na = ya_ref.shape[2]
    nb = yb_ref.shape[2]
    y = (_dot(ya_ref[0], wo_ref[0:na, :]) + _dot(yb_ref[0], wo_ref[na:na + nb, :])
         + _dot(yc_ref[0], wo_ref[na + nb:, :]))
    x1 = x_ref[0] + ga_ref[0] * y
    x1_ref[0] = x1
    ms = jnp.mean(x1 * x1, axis=-1, keepdims=True)
    h = x1 * lax.rsqrt(ms + EPS) * g_ref[...]
    h = h * (1.0 + sc_ref[0]) + sh_ref[0]
    hb = h.astype(BF16)
    h_ref[0] = hb

    logits = _dot(hb, wr_ref[...]) + br_ref[...]
    tm = logits.shape[0]
    G, E = N_GROUPS, EXPERTS_PER_GROUP
    lane = lax.broadcasted_iota(jnp.int32, (tm, LANES), 1)
    lane_f = lane.astype(F32)
    big = float(LANES)
    is_g = lane < G
    gl = jnp.where(is_g, logits, -jnp.inf)
    gmax = jnp.max(gl, axis=-1, keepdims=True)
    gidx = jnp.min(jnp.where(gl == gmax, lane_f, big), axis=-1, keepdims=True)
    gsum = jnp.sum(jnp.where(is_g, jnp.exp(gl - gmax), 0.0), axis=-1, keepdims=True)
    g_w = 1.0 / gsum
    lane_grp = ((lane - G) // E).astype(F32)
    emask = (lane >= G) & (lane < G + G * E) & (lane_grp == gidx)
    el = jnp.where(emask, logits, -jnp.inf)
    v1 = jnp.max(el, axis=-1, keepdims=True)
    i1 = jnp.min(jnp.where(el == v1, lane_f, big), axis=-1, keepdims=True)
    el2 = jnp.where(lane_f == i1, -jnp.inf, el)
    v2 = jnp.max(el2, axis=-1, keepdims=True)
    i2 = jnp.min(jnp.where(el2 == v2, lane_f, big), axis=-1, keepdims=True)
    t = jnp.exp(v2 - v1)
    w1 = g_w / (1.0 + t)
    w2 = w1 * t
    comb_ref[0] = (jnp.where(lane_f == i1 - G, w1, 0.0) + jnp.where(lane_f == i2 - G, w2, 0.0))


def _out_projection(ya, yb, yc, x, wo_bf, ga, sc, sh, g, wr, br, tm=512):
    B, S, D = x.shape
    tok = lambda n: pl.BlockSpec((1, tm, n), lambda b, i: (b, i, 0))
    per_b = pl.BlockSpec((1, 1, D), lambda b, i: (b, 0, 0))
    const = lambda shape: pl.BlockSpec(shape, lambda b, i: (0, 0))
    return pl.pallas_call(
        _outproj_kernel,
        grid=(B, S // tm),
        in_specs=[tok(ya.shape[2]), tok(yb.shape[2]), tok(yc.shape[2]), tok(D), const((D, D)),
                  per_b, per_b, per_b, const((1, D)), const((D, LANES)), const((1, LANES))],
        out_specs=[tok(D), tok(D), tok(LANES)],
        out_shape=[jax.ShapeDtypeStruct((B, S, D), F32), jax.ShapeDtypeStruct((B, S, D), BF16),
                   jax.ShapeDtypeStruct((B, S, LANES), F32)],
        compiler_params=_cparams(("parallel", "parallel")),
        name="out_projection",
    )(ya, yb, yc, x, wo_bf, ga, sc, sh, g, wr, br)


def _moe_kernel(h_ref, comb_ref, x_ref, ga_ref, w1_ref, w3_ref, w2_ref, o_ref, acc_ref):
    e = pl.program_id(2)

    @pl.when(e == 0)
    def _():
        acc_ref[...] = jnp.zeros_like(acc_ref)

    h = h_ref[0]
    lane = lax.broadcasted_iota(jnp.int32, comb_ref.shape[1:], 1)
    cw = jnp.sum(jnp.where(lane == e, comb_ref[0], 0.0), axis=-1, keepdims=True)
    a = _dot(h, w1_ref[0].astype(BF16))
    b = _dot(h, w3_ref[0].astype(BF16))
    hid = a * (1.0 / (1.0 + jnp.exp(-a))) * b * cw
    acc_ref[...] += _dot(hid.astype(BF16), w2_ref[0].astype(BF16))

    @pl.when(e == pl.num_programs(2) - 1)
    def _():
        o_ref[0] = x_ref[0] + ga_ref[0] * acc_ref[...]


def _moe(h, comb, x1, ga, w1, w3, w2, tm=1024):
    B, S, D = x1.shape
    NE, _, F = w1.shape
    tm = min(tm, S)
    tok = lambda n: pl.BlockSpec((1, tm, n), lambda b, i, e: (b, i, 0))
    return pl.pallas_call(
        _moe_kernel,
        grid=(B, S // tm, NE),
        in_specs=[tok(D), tok(LANES), tok(D), pl.BlockSpec((1, 1, D), lambda b, i, e: (b, 0, 0)),
                  pl.BlockSpec((1, D, F), lambda b, i, e: (e, 0, 0)),
                  pl.BlockSpec((1, D, F), lambda b, i, e: (e, 0, 0)),
                  pl.BlockSpec((1, F, D), lambda b, i, e: (e, 0, 0))],
        out_specs=tok(D),
        out_shape=jax.ShapeDtypeStruct((B, S, D), F32),
        scratch_shapes=[pltpu.VMEM((tm, D), F32)],
        compiler_params=_cparams(("parallel", "parallel", "arbitrary")),
        name="expert_mixture",
    )(h, comb, x1, ga, w1, w3, w2)


def _rope_tables(S, width):
    inv = 1.0 / (ROPE_THETA ** (np.arange(0, HEAD_DIM, 2, dtype=np.float32) / HEAD_DIM))
    ang = np.arange(S, dtype=np.float32)[:, None] * inv[None, :]
    ang = np.concatenate([ang, ang], axis=-1)
    return ang


def kernel(x, c, w_mod, b_mod, g_norm1, w_in, gq_a, gk_a, lam_a, g_sub_a, w_pool, b_pool, pool_scale, gq_c, gk_c, w_out, g_norm2, w_rg, b_rg, w_re, b_re, w1, w3, w2):
    B, S, D = x.shape
    L = w_mod.shape[0]
    a_width = D // 2
    pool_width = D // 4
    c_width = D // 4
    W = MXU_DIM
    reps = W // HEAD_DIM

    inv = 1.0 / (ROPE_THETA ** (jnp.arange(0, HEAD_DIM, 2, dtype=F32) / HEAD_DIM))
    ang = jnp.arange(S, dtype=F32)[:, None] * inv[None, :]
    ang = jnp.concatenate([ang, ang], axis=-1)
    cos_h, sin_h = jnp.cos(ang), jnp.sin(ang)
    first = jnp.arange(HEAD_DIM) < HEAD_DIM // 2
    cos_t = jnp.tile(cos_h, (1, reps))
    sa_t = jnp.tile(jnp.where(first[None, :], 0.0, sin_h), (1, reps))
    sb_t = jnp.tile(jnp.where(first[None, :], -sin_h, 0.0), (1, reps))
    head_of = np.arange(W) // HEAD_DIM
    bd = jnp.asarray((head_of[:, None] == head_of[None, :]) / HEAD_DIM, BF16)

    qa0, ka0, va0 = 0, a_width, 2 * a_width
    ub0 = 3 * a_width
    qc0, kc0, vc0 = ub0 + pool_width, ub0 + pool_width + c_width, ub0 + pool_width + 2 * c_width
    norm_chunks = tuple(range(qa0 // W, va0 // W)) + tuple(range(qc0 // W, vc0 // W))
    scale = HEAD_DIM ** -0.5
    cnt_t, bias_t = _dilated_tables(S, 256, 256)

    mod = _modulation(c, w_mod, b_mod)
    for l in range(L):
        sh1, sc1, ga1, sh2, sc2, ga2 = [m[:, None, :] for m in jnp.split(mod[l], N_MOD, axis=-1)]
        gain = jnp.ones((w_in.shape[2],), F32)
        gain = gain.at[qa0:ka0].set(jnp.tile(gq_a[l], a_width // HEAD_DIM) * scale)
        gain = gain.at[ka0:va0].set(jnp.tile(gk_a[l], a_width // HEAD_DIM))
        gain = gain.at[qc0:kc0].set(jnp.tile(gq_c[l], c_width // HEAD_DIM) * scale)
        gain = gain.at[kc0:vc0].set(jnp.tile(gk_c[l], c_width // HEAD_DIM))
        z = _in_projection(x, sc1, sh1, g_norm1[l][None, :], w_in[l].astype(BF16), gain[None, :],
                           cos_t, sa_t, sb_t, bd, norm_chunks)

        lam_init = 0.8 - 0.6 * math.exp(-0.3 * l)
        n_a = a_width // LANES
        ya = _attention("diff", z, qa0 // LANES, ka0 // LANES, va0 // LANES, n_a,
                        [lam_a[l], g_sub_a[l][None, :]],
                        [pl.BlockSpec(lam_a[l].shape, lambda b, h: (0, 0)),
                         pl.BlockSpec((1, LANES), lambda b, h: (0, 0))], lam_init=lam_init)
        n_c = c_width // LANES
        yc = _attention("dil", z, qc0 // LANES, kc0 // LANES, vc0 // LANES, n_c,
                        [cnt_t, bias_t],
                        [pl.BlockSpec(cnt_t.shape, lambda b, h: (0, 0, 0)),
                         pl.BlockSpec(bias_t.shape, lambda b, h: (0, 0, 0))])
        w_bd = jax.scipy.linalg.block_diag(*[w_pool[l, g] for g in range(w_pool.shape[1])])
        yb = _pool_mixer(z, ub0 // pool_width, w_bd.astype(BF16), b_pool[l].reshape(1, -1),
                         pool_scale[l][None, :])

        wr = jnp.zeros((D, LANES), F32)
        wr = wr.at[:, :N_GROUPS].set(w_rg[l])
        wr = wr.at[:, N_GROUPS:N_GROUPS + N_EXPERTS].set(
            w_re[l].transpose(1, 0, 2).reshape(D, N_EXPERTS))
        br = jnp.zeros((1, LANES), F32)
        br = br.at[0, :N_GROUPS].set(b_rg[l])
        br = br.at[0, N_GROUPS:N_GROUPS + N_EXPERTS].set(b_re[l].reshape(-1))
        x1, h2, comb = _out_projection(ya, yb, yc, x, w_out[l].astype(BF16), ga1, sc2, sh2,
                                       g_norm2[l][None, :], wr.astype(BF16), br)
        F = w1.shape[-1]
        x = _moe(h2, comb, x1, ga2, w1[l].reshape(N_EXPERTS, D, F), w3[l].reshape(N_EXPERTS, D, F),
                 w2[l].reshape(N_EXPERTS, F, D))
    return x
```

```python
import functools
import math

import jax
import jax.numpy as jnp
import numpy as np
from jax import lax
from jax.experimental import pallas as pl
from jax.experimental.pallas import tpu as pltpu

HEAD_DIM = 64
POOL_WINDOWS = (2, 4, 8, 16)
DILATED_PAIRS = ((128, 1), (512, 4), (2048, 16))
ROPE_THETA = 10000.0
N_GROUPS = 4
EXPERTS_PER_GROUP = 8
N_EXPERTS = N_GROUPS * EXPERTS_PER_GROUP
N_MOD = 6
EPS = 1e-6

LANES = 128
MXU_DIM = 256
KV_BLOCK = 256
Q_BLOCK = 256
VMEM_LIMIT = 48 * 1024 * 1024
NEG = -1e30
F32 = jnp.float32
BF16 = jnp.bfloat16


def _cparams(sem):
    return pltpu.CompilerParams(dimension_semantics=sem, vmem_limit_bytes=VMEM_LIMIT)


def _dot(a, b):
    return jnp.dot(a, b, preferred_element_type=F32)


def _dot_nt(a, b):
    return lax.dot_general(a, b, (((1,), (1,)), ((), ())), preferred_element_type=F32)


def _mod_kernel(c_ref, w_ref, b_ref, o_ref):
    c = c_ref[...]
    cond = c * (1.0 / (1.0 + jnp.exp(-c)))
    o_ref[0] = _dot(cond.astype(BF16), w_ref[0].astype(BF16)) + b_ref[0]


def _modulation(c, w_mod, b_mod):
    L, D, N = w_mod.shape
    B = c.shape[0]
    tn = 1024
    return pl.pallas_call(
        _mod_kernel,
        grid=(L, N // tn),
        in_specs=[pl.BlockSpec((B, D), lambda l, j: (0, 0)),
                  pl.BlockSpec((1, D, tn), lambda l, j: (l, 0, j)),
                  pl.BlockSpec((1, 1, tn), lambda l, j: (l, 0, j))],
        out_specs=pl.BlockSpec((1, B, tn), lambda l, j: (l, 0, j)),
        out_shape=jax.ShapeDtypeStruct((L, B, N), F32),
        compiler_params=_cparams(("parallel", "parallel")),
        name="modulation",
    )(c, w_mod, b_mod.reshape(L, 1, N))


def _inproj_kernel(z_chunks, v_chunks, x_ref, sc_ref, sh_ref, g_ref, w_ref, gain_ref, cos_ref,
                   sa_ref, sb_ref, bd_ref, z_ref, vt_ref):
    x = x_ref[0]
    ms = jnp.mean(x * x, axis=-1, keepdims=True)
    h = x * lax.rsqrt(ms + EPS) * g_ref[...]
    h = h * (1.0 + sc_ref[0]) + sh_ref[0]
    hb = h.astype(BF16)
    W = MXU_DIM
    for dst, (src, normed) in enumerate(z_chunks):
        zc = _dot(hb, w_ref[:, src * W:(src + 1) * W])
        if normed:
            msq = _dot((zc * zc).astype(BF16), bd_ref[...])
            y = zc * lax.rsqrt(msq + EPS) * gain_ref[:, src * W:(src + 1) * W]
            r_up = pltpu.roll(y, HEAD_DIM // 2, 1)
            r_dn = pltpu.roll(y, W - HEAD_DIM // 2, 1)
            zc = y * cos_ref[...] + r_up * sa_ref[...] + r_dn * sb_ref[...]
        z_ref[0, :, dst * W:(dst + 1) * W] = zc.astype(BF16)
    for dst, src in enumerate(v_chunks):
        zt = _dot(hb, w_ref[:, src * W:(src + 1) * W]).T.astype(BF16)
        for cb in range(vt_ref.shape[1]):
            vt_ref[0, cb, dst * W:(dst + 1) * W, :] = zt[:, cb * KV_BLOCK:(cb + 1) * KV_BLOCK]


def _in_projection(x, sc, sh, g, w_bf, gain, cos_t, sa_t, sb_t, bd, z_chunks, v_chunks, tm=512):
    B, S, D = x.shape
    N = w_bf.shape[1]
    W = MXU_DIM
    nz, nv = len(z_chunks) * W, len(v_chunks) * W
    return pl.pallas_call(
        functools.partial(_inproj_kernel, z_chunks, v_chunks),
        grid=(B, S // tm),
        in_specs=[pl.BlockSpec((1, tm, D), lambda b, i: (b, i, 0)),
                  pl.BlockSpec((1, 1, D), lambda b, i: (b, 0, 0)),
                  pl.BlockSpec((1, 1, D), lambda b, i: (b, 0, 0)),
                  pl.BlockSpec((1, D), lambda b, i: (0, 0)),
                  pl.BlockSpec((D, N), lambda b, i: (0, 0)),
                  pl.BlockSpec((1, N), lambda b, i: (0, 0)),
                  pl.BlockSpec((tm, W), lambda b, i: (i, 0)),
                  pl.BlockSpec((tm, W), lambda b, i: (i, 0)),
                  pl.BlockSpec((tm, W), lambda b, i: (i, 0)),
                  pl.BlockSpec((W, W), lambda b, i: (0, 0))],
        out_specs=[pl.BlockSpec((1, tm, nz), lambda b, i: (b, i, 0)),
                   pl.BlockSpec((1, tm // KV_BLOCK, nv, KV_BLOCK), lambda b, i: (b, i, 0, 0))],
        out_shape=[jax.ShapeDtypeStruct((B, S, nz), BF16),
                   jax.ShapeDtypeStruct((B, S // KV_BLOCK, nv, KV_BLOCK), BF16)],
        compiler_params=_cparams(("parallel", "parallel")),
        name="in_projection",
    )(x, sc, sh, g, w_bf, gain, cos_t, sa_t, sb_t, bd)


def _attn_kernel(mode, lam_init, *refs):
    if mode == "diff":
        q_ref, k_ref, vt_ref, lam_ref, g_ref, o_ref = refs
    else:
        q_ref, k_ref, vt_ref, cnt_ref, bias_ref, o_ref = refs
    bq, bk = Q_BLOCK, KV_BLOCK
    S = q_ref.shape[1]
    nq = S // bq
    lane = lax.broadcasted_iota(jnp.int32, (1, LANES), 1)
    lo = lane < HEAD_DIM

    def scores(q2, ki):
        ks = pl.multiple_of(ki * bk, bk)
        return _dot_nt(k_ref[0, pl.ds(ks, bk), :], q2)

    def both(t):
        return jnp.concatenate([t, t], axis=1)

    def update(s, ki, carry, bias, cnt):
        m, l, a = carry
        if bias is not None:
            s = s + both(bias)
        mn = jnp.maximum(m, jnp.max(s, axis=0, keepdims=True))
        alpha = jnp.exp(m - mn)
        p = jnp.exp(s - mn)
        if cnt is not None:
            p = p * both(cnt)
        l = alpha * l + jnp.sum(p, axis=0, keepdims=True)
        a = alpha * a + _dot(vt_ref[0, ki], p.astype(BF16))
        return mn, l, a

    def q_block(qi, _):
        qs = pl.multiple_of(qi * bq, bq)
        q = q_ref[0, pl.ds(qs, bq), :]
        zero = jnp.zeros_like(q)
        q2 = jnp.concatenate([jnp.where(lo, q, zero), jnp.where(lo, zero, q)], axis=0)
        init = (jnp.full((1, 2 * bq), NEG, F32), jnp.zeros((1, 2 * bq), F32),
                jnp.zeros((LANES, 2 * bq), F32))

        def body(ki, c):
            s_next = scores(q2, ki + 1)
            if mode == "diff":
                stats = update(c[0], ki, c[1:], None, None)
            else:
                d = qi - ki
                stats = update(c[0], ki, c[1:], bias_ref[d], cnt_ref[d])
            return (s_next,) + stats

        c = lax.fori_loop(0, qi, body, (scores(q2, 0),) + init)
        if mode == "diff":
            krow = lax.broadcasted_iota(jnp.int32, (bk, bq), 0)
            qcol = lax.broadcasted_iota(jnp.int32, (bk, bq), 1)
            causal = jnp.where(krow <= qcol, 0.0, NEG).astype(F32)
            m, l, a = update(c[0], qi, c[1:], causal, None)
            lp = lam_ref[...]
            lam = (jnp.exp(jnp.sum(lp[0:1] * lp[1:2], axis=-1, keepdims=True))
                   - jnp.exp(jnp.sum(lp[2:3] * lp[3:4], axis=-1, keepdims=True)) + lam_init)
            y = a[:, :bq] * (1.0 / l[:, :bq]) - a[:, bq:] * (lam / l[:, bq:])
            ms = jnp.mean(y * y, axis=0, keepdims=True)
            y = y * lax.rsqrt(ms + EPS) * (g_ref[...] * (1.0 - lam_init))
        else:
            m, l, a = update(c[0], qi, c[1:], bias_ref[0], cnt_ref[0])
            feat = lax.broadcasted_iota(jnp.int32, (LANES, 1), 0)
            y = jnp.where(feat < HEAD_DIM, a[:, :bq] * (1.0 / l[:, :bq]), a[:, bq:] * (1.0 / l[:, bq:]))
        o_ref[0, pl.ds(qs, bq), :] = y.T.astype(o_ref.dtype)
        return 0

    lax.fori_loop(0, nq, q_block, 0)


def _attention(mode, z, vt, q_col, k_col, v_row, n_blocks, extras, extra_specs, lam_init=0.0):
    B, S, _ = z.shape
    zspec = lambda col: pl.BlockSpec((1, S, LANES), lambda b, h: (b, 0, col + h))
    vspec = pl.BlockSpec((1, S // KV_BLOCK, LANES, KV_BLOCK), lambda b, h: (b, 0, v_row + h, 0))
    return pl.pallas_call(
        functools.partial(_attn_kernel, mode, lam_init),
        grid=(B, n_blocks),
        in_specs=[zspec(q_col), zspec(k_col), vspec] + extra_specs,
        out_specs=pl.BlockSpec((1, S, LANES), lambda b, h: (b, 0, h)),
        out_shape=jax.ShapeDtypeStruct((B, S, n_blocks * LANES), BF16),
        compiler_params=_cparams(("parallel", "parallel")),
        name=mode + "_attention",
    )(z, z, vt, *extras)


def _dilated_tables(S):
    bq, bk = Q_BLOCK, KV_BLOCK
    nd = S // bk
    dist = (np.arange(nd)[:, None, None] * bk + np.arange(bq)[None, None, :]
            - np.arange(bk)[None, :, None])
    cnt = np.zeros(dist.shape, np.float32)
    for window, dil in DILATED_PAIRS:
        cnt += (dist >= 0) & (dist <= window) & (dist % dil == 0)
    bias = np.where(cnt > 0, 0.0, NEG).astype(np.float32)
    return jnp.asarray(cnt), jnp.asarray(bias)


def _pool_kernel(u_ref, w_ref, b_ref, sc_ref, o_ref):
    u = u_ref[0].astype(F32)
    S, C = u.shape
    row = lax.broadcasted_iota(jnp.int32, (S, C), 0)
    lane = lax.broadcasted_iota(jnp.int32, (S, C), 1)
    grp = lane // (C // len(POOL_WINDOWS))

    def shifted(a, k):
        return jnp.where(row >= k, pltpu.roll(a, k, 0), 0.0)

    acc = u
    win = jnp.zeros_like(u)
    width = 1
    for gi, w in enumerate(POOL_WINDOWS):
        while width < w:
            acc = acc + shifted(acc, width)
            width *= 2
        win = jnp.where(grp == gi, acc, win)
    wl = jnp.zeros_like(row)
    for gi, w in enumerate(POOL_WINDOWS):
        wl = jnp.where(grp == gi, w, wl)
    cnt = jnp.minimum(row + 1, wl).astype(F32)
    d = win / cnt - u
    y = _dot(d.astype(BF16), w_ref[...]) + b_ref[...]
    o_ref[0] = (y * sc_ref[...]).astype(o_ref.dtype)


def _pool_mixer(z, col_block, w_bd, b, scale):
    B, S, _ = z.shape
    C = w_bd.shape[0]
    return pl.pallas_call(
        _pool_kernel,
        grid=(B,),
        in_specs=[pl.BlockSpec((1, S, C), lambda b_: (b_, 0, col_block)),
                  pl.BlockSpec((C, C), lambda b_: (0, 0)),
                  pl.BlockSpec((1, C), lambda b_: (0, 0)),
                  pl.BlockSpec((1, C), lambda b_: (0, 0))],
        out_specs=pl.BlockSpec((1, S, C), lambda b_: (b_, 0, 0)),
        out_shape=jax.ShapeDtypeStruct((B, S, C), BF16),
        compiler_params=_cparams(("parallel",)),
        name="pool_mixer",
    )(z, w_bd, b, scale)


def _outproj_kernel(ya_ref, yb_ref, yc_ref, x_ref, wo_ref, ga_ref, sc_ref, sh_ref, g_ref, wr_ref,
                    br_ref, x1_ref, h_ref, comb_ref):
    na = ya_ref.shape[2]
    nb = yb_ref.shape[2]
    y = (_dot(ya_ref[0], wo_ref[0:na, :]) + _dot(yb_ref[0], wo_ref[na:na + nb, :])
         + _dot(yc_ref[0], wo_ref[na + nb:, :]))
    x1 = x_ref[0] + ga_ref[0] * y
    x1_ref[0] = x1
    ms = jnp.mean(x1 * x1, axis=-1, keepdims=True)
    h = x1 * lax.rsqrt(ms + EPS) * g_ref[...]
    h = h * (1.0 + sc_ref[0]) + sh_ref[0]
    hb = h.astype(BF16)
    h_ref[0] = hb

    logits = _dot(hb, wr_ref[...]) + br_ref[...]
    tm = logits.shape[0]
    G, E = N_GROUPS, EXPERTS_PER_GROUP
    lane = lax.broadcasted_iota(jnp.int32, (tm, LANES), 1)
    lane_f = lane.astype(F32)
    big = float(LANES)
    is_g = lane < G
    gl = jnp.where(is_g, logits, -jnp.inf)
    gmax = jnp.max(gl, axis=-1, keepdims=True)
    gidx = jnp.min(jnp.where(gl == gmax, lane_f, big), axis=-1, keepdims=True)
    gsum = jnp.sum(jnp.where(is_g, jnp.exp(gl - gmax), 0.0), axis=-1, keepdims=True)
    g_w = 1.0 / gsum
    lane_grp = ((lane - G) // E).astype(F32)
    emask = (lane >= G) & (lane < G + G * E) & (lane_grp == gidx)
    el = jnp.where(emask, logits, -jnp.inf)
    v1 = jnp.max(el, axis=-1, keepdims=True)
    i1 = jnp.min(jnp.where(el == v1, lane_f, big), axis=-1, keepdims=True)
    el2 = jnp.where(lane_f == i1, -jnp.inf, el)
    v2 = jnp.max(el2, axis=-1, keepdims=True)
    i2 = jnp.min(jnp.where(el2 == v2, lane_f, big), axis=-1, keepdims=True)
    t = jnp.exp(v2 - v1)
    w1 = g_w / (1.0 + t)
    w2 = w1 * t
    comb_ref[0] = (jnp.where(lane_f == i1 - G, w1, 0.0) + jnp.where(lane_f == i2 - G, w2, 0.0))


def _out_projection(ya, yb, yc, x, wo_bf, ga, sc, sh, g, wr, br, tm=512):
    B, S, D = x.shape
    tok = lambda n: pl.BlockSpec((1, tm, n), lambda b, i: (b, i, 0))
    per_b = pl.BlockSpec((1, 1, D), lambda b, i: (b, 0, 0))
    const = lambda shape: pl.BlockSpec(shape, lambda b, i: (0, 0))
    return pl.pallas_call(
        _outproj_kernel,
        grid=(B, S // tm),
        in_specs=[tok(ya.shape[2]), tok(yb.shape[2]), tok(yc.shape[2]), tok(D), const((D, D)),
                  per_b, per_b, per_b, const((1, D)), const((D, LANES)), const((1, LANES))],
        out_specs=[tok(D), tok(D), tok(LANES)],
        out_shape=[jax.ShapeDtypeStruct((B, S, D), F32), jax.ShapeDtypeStruct((B, S, D), BF16),
                   jax.ShapeDtypeStruct((B, S, LANES), F32)],
        compiler_params=_cparams(("parallel", "parallel")),
        name="out_projection",
    )(ya, yb, yc, x, wo_bf, ga, sc, sh, g, wr, br)


def _moe_kernel(h_ref, comb_ref, x_ref, ga_ref, w1_ref, w3_ref, w2_ref, o_ref, acc_ref):
    e = pl.program_id(2)

    @pl.when(e == 0)
    def _():
        acc_ref[...] = jnp.zeros_like(acc_ref)

    h = h_ref[0]
    lane = lax.broadcasted_iota(jnp.int32, comb_ref.shape[1:], 1)
    cw = jnp.sum(jnp.where(lane == e, comb_ref[0], 0.0), axis=-1, keepdims=True)
    a = _dot(h, w1_ref[0].astype(BF16))
    b = _dot(h, w3_ref[0].astype(BF16))
    hid = a * (1.0 / (1.0 + jnp.exp(-a))) * b * cw
    acc_ref[...] += _dot(hid.astype(BF16), w2_ref[0].astype(BF16))

    @pl.when(e == pl.num_programs(2) - 1)
    def _():
        o_ref[0] = x_ref[0] + ga_ref[0] * acc_ref[...]


def _moe(h, comb, x1, ga, w1, w3, w2, tm=1024):
    B, S, D = x1.shape
    NE, _, F = w1.shape
    tm = min(tm, S)
    tok = lambda n: pl.BlockSpec((1, tm, n), lambda b, i, e: (b, i, 0))
    return pl.pallas_call(
        _moe_kernel,
        grid=(B, S // tm, NE),
        in_specs=[tok(D), tok(LANES), tok(D), pl.BlockSpec((1, 1, D), lambda b, i, e: (b, 0, 0)),
                  pl.BlockSpec((1, D, F), lambda b, i, e: (e, 0, 0)),
                  pl.BlockSpec((1, D, F), lambda b, i, e: (e, 0, 0)),
                  pl.BlockSpec((1, F, D), lambda b, i, e: (e, 0, 0))],
        out_specs=tok(D),
        out_shape=jax.ShapeDtypeStruct((B, S, D), F32),
        scratch_shapes=[pltpu.VMEM((tm, D), F32)],
        compiler_params=_cparams(("parallel", "parallel", "arbitrary")),
        name="expert_mixture",
    )(h, comb, x1, ga, w1, w3, w2)


def kernel(x, c, w_mod, b_mod, g_norm1, w_in, gq_a, gk_a, lam_a, g_sub_a, w_pool, b_pool, pool_scale, gq_c, gk_c, w_out, g_norm2, w_rg, b_rg, w_re, b_re, w1, w3, w2):
    B, S, D = x.shape
    L = w_mod.shape[0]
    a_width = D // 2
    pool_width = D // 4
    c_width = D // 4
    W = MXU_DIM
    reps = W // HEAD_DIM

    inv = 1.0 / (ROPE_THETA ** (jnp.arange(0, HEAD_DIM, 2, dtype=F32) / HEAD_DIM))
    ang = jnp.arange(S, dtype=F32)[:, None] * inv[None, :]
    ang = jnp.concatenate([ang, ang], axis=-1)
    cos_h, sin_h = jnp.cos(ang), jnp.sin(ang)
    first = jnp.arange(HEAD_DIM) < HEAD_DIM // 2
    cos_t = jnp.tile(cos_h, (1, reps))
    sa_t = jnp.tile(jnp.where(first[None, :], 0.0, sin_h), (1, reps))
    sb_t = jnp.tile(jnp.where(first[None, :], -sin_h, 0.0), (1, reps))
    head_of = np.arange(W) // HEAD_DIM
    bd = jnp.asarray((head_of[:, None] == head_of[None, :]) / HEAD_DIM, BF16)

    qa0, ka0, va0 = 0, a_width, 2 * a_width
    ub0 = 3 * a_width
    qc0, kc0, vc0 = ub0 + pool_width, ub0 + pool_width + c_width, ub0 + pool_width + 2 * c_width
    chunks = lambda lo_, hi_: list(range(lo_ // W, hi_ // W))
    z_chunks = tuple([(j, True) for j in chunks(qa0, va0)] + [(j, False) for j in chunks(ub0, qc0)]
                     + [(j, True) for j in chunks(qc0, vc0)])
    v_chunks = tuple(chunks(va0, ub0) + chunks(vc0, w_in.shape[2]))
    zq_a, zk_a = 0, a_width // LANES
    z_ub = 2 * a_width // pool_width
    zq_c = (2 * a_width + pool_width) // LANES
    zk_c = zq_c + c_width // LANES
    scale = HEAD_DIM ** -0.5
    cnt_t, bias_t = _dilated_tables(S)

    mod = _modulation(c, w_mod, b_mod)
    for l in range(L):
        sh1, sc1, ga1, sh2, sc2, ga2 = [m[:, None, :] for m in jnp.split(mod[l], N_MOD, axis=-1)]
        gain = jnp.ones((w_in.shape[2],), F32)
        gain = gain.at[qa0:ka0].set(jnp.tile(gq_a[l], a_width // HEAD_DIM) * scale)
        gain = gain.at[ka0:va0].set(jnp.tile(gk_a[l], a_width // HEAD_DIM))
        gain = gain.at[qc0:kc0].set(jnp.tile(gq_c[l], c_width // HEAD_DIM) * scale)
        gain = gain.at[kc0:vc0].set(jnp.tile(gk_c[l], c_width // HEAD_DIM))
        z, vt = _in_projection(x, sc1, sh1, g_norm1[l][None, :], w_in[l].astype(BF16),
                               gain[None, :], cos_t, sa_t, sb_t, bd, z_chunks, v_chunks)

        lam_init = 0.8 - 0.6 * math.exp(-0.3 * l)
        n_a = a_width // LANES
        ya = _attention("diff", z, vt, zq_a, zk_a, 0, n_a,
                        [lam_a[l], g_sub_a[l][:, None]],
                        [pl.BlockSpec(lam_a[l].shape, lambda b, h: (0, 0)),
                         pl.BlockSpec((LANES, 1), lambda b, h: (0, 0))], lam_init=lam_init)
        n_c = c_width // LANES
        yc = _attention("dil", z, vt, zq_c, zk_c, n_a, n_c,
                        [cnt_t, bias_t],
                        [pl.BlockSpec(cnt_t.shape, lambda b, h: (0, 0, 0)),
                         pl.BlockSpec(bias_t.shape, lambda b, h: (0, 0, 0))])
        w_bd = jax.scipy.linalg.block_diag(*[w_pool[l, g] for g in range(w_pool.shape[1])])
        yb = _pool_mixer(z, z_ub, w_bd.astype(BF16), b_pool[l].reshape(1, -1),
                         pool_scale[l][None, :])

        wr = jnp.zeros((D, LANES), F32)
        wr = wr.at[:, :N_GROUPS].set(w_rg[l])
        wr = wr.at[:, N_GROUPS:N_GROUPS + N_EXPERTS].set(
            w_re[l].transpose(1, 0, 2).reshape(D, N_EXPERTS))
        br = jnp.zeros((1, LANES), F32)
        br = br.at[0, :N_GROUPS].set(b_rg[l])
        br = br.at[0, N_GROUPS:N_GROUPS + N_EXPERTS].set(b_re[l].reshape(-1))
        x1, h2, comb = _out_projection(ya, yb, yc, x, w_out[l].astype(BF16), ga1, sc2, sh2,
                                       g_norm2[l][None, :], wr.astype(BF16), br)
        F = w1.shape[-1]
        x = _moe(h2, comb, x1, ga2, w1[l].reshape(N_EXPERTS, D, F), w3[l].reshape(N_EXPERTS, D, F),
                 w2[l].reshape(N_EXPERTS, F, D))
    return x
```

```python
import functools
import math

import jax
import jax.numpy as jnp
import numpy as np
from jax import lax
from jax.experimental import pallas as pl
from jax.experimental.pallas import tpu as pltpu

HEAD_DIM = 64
POOL_WINDOWS = (2, 4, 8, 16)
DILATED_PAIRS = ((128, 1), (512, 4), (2048, 16))
ROPE_THETA = 10000.0
N_GROUPS = 4
EXPERTS_PER_GROUP = 8
N_EXPERTS = N_GROUPS * EXPERTS_PER_GROUP
N_MOD = 6
EPS = 1e-6

LANES = 128
MXU_DIM = 256
KV_BLOCK = 256
Q_BLOCK = 256
MOE_TILE = 256
RT_E, RT_W = 0, 2
VMEM_LIMIT = 48 * 1024 * 1024
NEG = -1e30
F32 = jnp.float32
BF16 = jnp.bfloat16


def _cparams(sem):
    return pltpu.CompilerParams(dimension_semantics=sem, vmem_limit_bytes=VMEM_LIMIT)


def _dot(a, b):
    return jnp.dot(a, b, preferred_element_type=F32)


def _dot_nt(a, b):
    return lax.dot_general(a, b, (((1,), (1,)), ((), ())), preferred_element_type=F32)


def _mod_kernel(c_ref, w_ref, b_ref, o_ref):
    c = c_ref[...]
    cond = c * (1.0 / (1.0 + jnp.exp(-c)))
    o_ref[0] = _dot(cond.astype(BF16), w_ref[0].astype(BF16)) + b_ref[0]


def _modulation(c, w_mod, b_mod):
    L, D, N = w_mod.shape
    B = c.shape[0]
    tn = 1024
    return pl.pallas_call(
        _mod_kernel,
        grid=(L, N // tn),
        in_specs=[pl.BlockSpec((B, D), lambda l, j: (0, 0)),
                  pl.BlockSpec((1, D, tn), lambda l, j: (l, 0, j)),
                  pl.BlockSpec((1, 1, tn), lambda l, j: (l, 0, j))],
        out_specs=pl.BlockSpec((1, B, tn), lambda l, j: (l, 0, j)),
        out_shape=jax.ShapeDtypeStruct((L, B, N), F32),
        compiler_params=_cparams(("parallel", "parallel")),
        name="modulation",
    )(c, w_mod, b_mod.reshape(L, 1, N))


def _inproj_kernel(z_chunks, v_chunks, x_ref, sc_ref, sh_ref, g_ref, w_ref, gain_ref, cos_ref,
                   sa_ref, sb_ref, bd_ref, z_ref, vt_ref):
    x = x_ref[0]
    ms = jnp.mean(x * x, axis=-1, keepdims=True)
    h = x * lax.rsqrt(ms + EPS) * g_ref[...]
    h = h * (1.0 + sc_ref[0]) + sh_ref[0]
    hb = h.astype(BF16)
    W = MXU_DIM
    for dst, (src, normed) in enumerate(z_chunks):
        zc = _dot(hb, w_ref[:, src * W:(src + 1) * W])
        if normed:
            msq = _dot((zc * zc).astype(BF16), bd_ref[...])
            y = zc * lax.rsqrt(msq + EPS) * gain_ref[:, src * W:(src + 1) * W]
            r_up = pltpu.roll(y, HEAD_DIM // 2, 1)
            r_dn = pltpu.roll(y, W - HEAD_DIM // 2, 1)
            zc = y * cos_ref[...] + r_up * sa_ref[...] + r_dn * sb_ref[...]
        z_ref[0, :, dst * W:(dst + 1) * W] = zc.astype(BF16)
    for dst, src in enumerate(v_chunks):
        zt = _dot(hb, w_ref[:, src * W:(src + 1) * W]).T.astype(BF16)
        for cb in range(vt_ref.shape[1]):
            vt_ref[0, cb, dst * W:(dst + 1) * W, :] = zt[:, cb * KV_BLOCK:(cb + 1) * KV_BLOCK]


def _in_projection(x, sc, sh, g, w_bf, gain, cos_t, sa_t, sb_t, bd, z_chunks, v_chunks, tm=512):
    B, S, D = x.shape
    N = w_bf.shape[1]
    W = MXU_DIM
    nz, nv = len(z_chunks) * W, len(v_chunks) * W
    return pl.pallas_call(
        functools.partial(_inproj_kernel, z_chunks, v_chunks),
        grid=(B, S // tm),
        in_specs=[pl.BlockSpec((1, tm, D), lambda b, i: (b, i, 0)),
                  pl.BlockSpec((1, 1, D), lambda b, i: (b, 0, 0)),
                  pl.BlockSpec((1, 1, D), lambda b, i: (b, 0, 0)),
                  pl.BlockSpec((1, D), lambda b, i: (0, 0)),
                  pl.BlockSpec((D, N), lambda b, i: (0, 0)),
                  pl.BlockSpec((1, N), lambda b, i: (0, 0)),
                  pl.BlockSpec((tm, W), lambda b, i: (i, 0)),
                  pl.BlockSpec((tm, W), lambda b, i: (i, 0)),
                  pl.BlockSpec((tm, W), lambda b, i: (i, 0)),
                  pl.BlockSpec((W, W), lambda b, i: (0, 0))],
        out_specs=[pl.BlockSpec((1, tm, nz), lambda b, i: (b, i, 0)),
                   pl.BlockSpec((1, tm // KV_BLOCK, nv, KV_BLOCK), lambda b, i: (b, i, 0, 0))],
        out_shape=[jax.ShapeDtypeStruct((B, S, nz), BF16),
                   jax.ShapeDtypeStruct((B, S // KV_BLOCK, nv, KV_BLOCK), BF16)],
        compiler_params=_cparams(("parallel", "parallel")),
        name="in_projection",
    )(x, sc, sh, g, w_bf, gain, cos_t, sa_t, sb_t, bd)


def _attn_kernel(mode, lam_init, *refs):
    if mode == "diff":
        q_ref, k_ref, vt_ref, lam_ref, g_ref, o_ref = refs
    else:
        q_ref, k_ref, vt_ref, cnt_ref, bias_ref, o_ref = refs
    bq, bk = Q_BLOCK, KV_BLOCK
    S = q_ref.shape[1]
    nq = S // bq
    lane = lax.broadcasted_iota(jnp.int32, (1, LANES), 1)
    lo = lane < HEAD_DIM

    def scores(q2, ki):
        ks = pl.multiple_of(ki * bk, bk)
        return _dot_nt(k_ref[0, pl.ds(ks, bk), :], q2)

    def both(t):
        return jnp.concatenate([t, t], axis=1)

    def update(s, ki, carry, bias, cnt):
        m, l, a = carry
        if bias is not None:
            s = s + both(bias)
        mn = jnp.maximum(m, jnp.max(s, axis=0, keepdims=True))
        alpha = jnp.exp(m - mn)
        p = jnp.exp(s - mn)
        if cnt is not None:
            p = p * both(cnt)
        l = alpha * l + jnp.sum(p, axis=0, keepdims=True)
        a = alpha * a + _dot(vt_ref[0, ki], p.astype(BF16))
        return mn, l, a

    def q_block(qi, _):
        qs = pl.multiple_of(qi * bq, bq)
        q = q_ref[0, pl.ds(qs, bq), :]
        zero = jnp.zeros_like(q)
        q2 = jnp.concatenate([jnp.where(lo, q, zero), jnp.where(lo, zero, q)], axis=0)
        init = (jnp.full((1, 2 * bq), NEG, F32), jnp.zeros((1, 2 * bq), F32),
                jnp.zeros((LANES, 2 * bq), F32))

        def body(ki, c):
            s_next = scores(q2, ki + 1)
            if mode == "diff":
                stats = update(c[0], ki, c[1:], None, None)
            else:
                d = qi - ki
                stats = update(c[0], ki, c[1:], bias_ref[d], cnt_ref[d])
            return (s_next,) + stats

        c = lax.fori_loop(0, qi, body, (scores(q2, 0),) + init)
        if mode == "diff":
            krow = lax.broadcasted_iota(jnp.int32, (bk, bq), 0)
            qcol = lax.broadcasted_iota(jnp.int32, (bk, bq), 1)
            causal = jnp.where(krow <= qcol, 0.0, NEG).astype(F32)
            m, l, a = update(c[0], qi, c[1:], causal, None)
            lp = lam_ref[...]
            lam = (jnp.exp(jnp.sum(lp[0:1] * lp[1:2], axis=-1, keepdims=True))
                   - jnp.exp(jnp.sum(lp[2:3] * lp[3:4], axis=-1, keepdims=True)) + lam_init)
            y = a[:, :bq] * (1.0 / l[:, :bq]) - a[:, bq:] * (lam / l[:, bq:])
            ms = jnp.mean(y * y, axis=0, keepdims=True)
            y = y * lax.rsqrt(ms + EPS) * (g_ref[...] * (1.0 - lam_init))
        else:
            m, l, a = update(c[0], qi, c[1:], bias_ref[0], cnt_ref[0])
            feat = lax.broadcasted_iota(jnp.int32, (LANES, 1), 0)
            y = jnp.where(feat < HEAD_DIM, a[:, :bq] * (1.0 / l[:, :bq]), a[:, bq:] * (1.0 / l[:, bq:]))
        o_ref[0, pl.ds(qs, bq), :] = y.T.astype(o_ref.dtype)
        return 0

    lax.fori_loop(0, nq, q_block, 0)


def _attention(mode, z, vt, q_col, k_col, v_row, n_blocks, extras, extra_specs, lam_init=0.0):
    B, S, _ = z.shape
    zspec = lambda col: pl.BlockSpec((1, S, LANES), lambda b, h: (b, 0, col + h))
    vspec = pl.BlockSpec((1, S // KV_BLOCK, LANES, KV_BLOCK), lambda b, h: (b, 0, v_row + h, 0))
    return pl.pallas_call(
        functools.partial(_attn_kernel, mode, lam_init),
        grid=(B, n_blocks),
        in_specs=[zspec(q_col), zspec(k_col), vspec] + extra_specs,
        out_specs=pl.BlockSpec((1, S, LANES), lambda b, h: (b, 0, h)),
        out_shape=jax.ShapeDtypeStruct((B, S, n_blocks * LANES), BF16),
        compiler_params=_cparams(("parallel", "parallel")),
        name=mode + "_attention",
    )(z, z, vt, *extras)


def _dilated_tables(S):
    bq, bk = Q_BLOCK, KV_BLOCK
    nd = S // bk
    dist = (np.arange(nd)[:, None, None] * bk + np.arange(bq)[None, None, :]
            - np.arange(bk)[None, :, None])
    cnt = np.zeros(dist.shape, np.float32)
    for window, dil in DILATED_PAIRS:
        cnt += (dist >= 0) & (dist <= window) & (dist % dil == 0)
    bias = np.where(cnt > 0, 0.0, NEG).astype(np.float32)
    return jnp.asarray(cnt), jnp.asarray(bias)


def _pool_kernel(u_ref, w_ref, b_ref, sc_ref, o_ref):
    u = u_ref[0].astype(F32)
    S, C = u.shape
    row = lax.broadcasted_iota(jnp.int32, (S, C), 0)
    lane = lax.broadcasted_iota(jnp.int32, (S, C), 1)
    grp = lane // (C // len(POOL_WINDOWS))

    def shifted(a, k):
        return jnp.where(row >= k, pltpu.roll(a, k, 0), 0.0)

    acc = u
    win = jnp.zeros_like(u)
    width = 1
    for gi, w in enumerate(POOL_WINDOWS):
        while width < w:
            acc = acc + shifted(acc, width)
            width *= 2
        win = jnp.where(grp == gi, acc, win)
    wl = jnp.zeros_like(row)
    for gi, w in enumerate(POOL_WINDOWS):
        wl = jnp.where(grp == gi, w, wl)
    cnt = jnp.minimum(row + 1, wl).astype(F32)
    d = win / cnt - u
    y = _dot(d.astype(BF16), w_ref[...]) + b_ref[...]
    o_ref[0] = (y * sc_ref[...]).astype(o_ref.dtype)


def _pool_mixer(z, col_block, w_bd, b, scale):
    B, S, _ = z.shape
    C = w_bd.shape[0]
    return pl.pallas_call(
        _pool_kernel,
        grid=(B,),
        in_specs=[pl.BlockSpec((1, S, C), lambda b_: (b_, 0, col_block)),
                  pl.BlockSpec((C, C), lambda b_: (0, 0)),
                  pl.BlockSpec((1, C), lambda b_: (0, 0)),
                  pl.BlockSpec((1, C), lambda b_: (0, 0))],
        out_specs=pl.BlockSpec((1, S, C), lambda b_: (b_, 0, 0)),
        out_shape=jax.ShapeDtypeStruct((B, S, C), BF16),
        compiler_params=_cparams(("parallel",)),
        name="pool_mixer",
    )(z, w_bd, b, scale)


def _outproj_kernel(ya_ref, yb_ref, yc_ref, x_ref, wo_ref, ga_ref, sc_ref, sh_ref, g_ref, wr_ref,
                    br_ref, x1_ref, h_ref, rt_ref):
    na = ya_ref.shape[2]
    nb = yb_ref.shape[2]
    y = (_dot(ya_ref[0], wo_ref[0:na, :]) + _dot(yb_ref[0], wo_ref[na:na + nb, :])
         + _dot(yc_ref[0], wo_ref[na + nb:, :]))
    x1 = x_ref[0] + ga_ref[0] * y
    x1_ref[0] = x1
    ms = jnp.mean(x1 * x1, axis=-1, keepdims=True)
    h = x1 * lax.rsqrt(ms + EPS) * g_ref[...]
    h = h * (1.0 + sc_ref[0]) + sh_ref[0]
    hb = h.astype(BF16)
    for s in range(h_ref.shape[0]):
        h_ref[s] = h[:, s * LANES:(s + 1) * LANES]

    logits = _dot(hb, wr_ref[...]) + br_ref[...]
    tm = logits.shape[0]
    G, E = N_GROUPS, EXPERTS_PER_GROUP
    lane = lax.broadcasted_iota(jnp.int32, (tm, LANES), 1)
    lane_f = lane.astype(F32)
    big = float(LANES)
    is_g = lane < G
    gl = jnp.where(is_g, logits, -jnp.inf)
    gmax = jnp.max(gl, axis=-1, keepdims=True)
    gidx = jnp.min(jnp.where(gl == gmax, lane_f, big), axis=-1, keepdims=True)
    gsum = jnp.sum(jnp.where(is_g, jnp.exp(gl - gmax), 0.0), axis=-1, keepdims=True)
    g_w = 1.0 / gsum
    lane_grp = ((lane - G) // E).astype(F32)
    emask = (lane >= G) & (lane < G + G * E) & (lane_grp == gidx)
    el = jnp.where(emask, logits, -jnp.inf)
    v1 = jnp.max(el, axis=-1, keepdims=True)
    i1 = jnp.min(jnp.where(el == v1, lane_f, big), axis=-1, keepdims=True)
    el2 = jnp.where(lane_f == i1, -jnp.inf, el)
    v2 = jnp.max(el2, axis=-1, keepdims=True)
    i2 = jnp.min(jnp.where(el2 == v2, lane_f, big), axis=-1, keepdims=True)
    t = jnp.exp(v2 - v1)
    w1 = g_w / (1.0 + t)
    w2 = w1 * t
    rt_ref[...] = jnp.where(lane == RT_E, i1 - G, jnp.where(lane == RT_E + 1, i2 - G,
                            jnp.where(lane == RT_W, w1, jnp.where(lane == RT_W + 1, w2, 0.0))))


def _out_projection(ya, yb, yc, x, wo_bf, ga, sc, sh, g, wr, br, tm=512):
    B, S, D = x.shape
    nt = S // tm
    tok = lambda n: pl.BlockSpec((1, tm, n), lambda b, i: (b, i, 0))
    per_b = pl.BlockSpec((1, 1, D), lambda b, i: (b, 0, 0))
    const = lambda shape: pl.BlockSpec(shape, lambda b, i: (0, 0))
    return pl.pallas_call(
        _outproj_kernel,
        grid=(B, nt),
        in_specs=[tok(ya.shape[2]), tok(yb.shape[2]), tok(yc.shape[2]), tok(D), const((D, D)),
                  per_b, per_b, per_b, const((1, D)), const((D, LANES)), const((1, LANES))],
        out_specs=[tok(D), pl.BlockSpec((D // LANES, tm, LANES), lambda b, i: (0, b * nt + i, 0)),
                   pl.BlockSpec((tm, LANES), lambda b, i: (b * nt + i, 0))],
        out_shape=[jax.ShapeDtypeStruct((B, S, D), F32),
                   jax.ShapeDtypeStruct((D // LANES, B * S, LANES), F32),
                   jax.ShapeDtypeStruct((B * S, LANES), F32)],
        compiler_params=_cparams(("parallel", "parallel")),
        name="out_projection",
    )(ya, yb, yc, x, wo_bf, ga, sc, sh, g, wr, br)


def _rank_kernel(rt_ref, tri_ref, striu_ref, pos_ref, info_ref, cnt_ref, off_ref, carry_ref):
    ph, i = pl.program_id(0), pl.program_id(1)
    rt = rt_ref[...]
    lane = lax.broadcasted_iota(jnp.int32, rt.shape, 1)
    lane_f = lane.astype(F32)
    hit1 = lane_f == rt[:, RT_E:RT_E + 1]
    hit2 = lane_f == rt[:, RT_E + 1:RT_E + 2]
    onehot = jnp.where(hit1, 1.0, jnp.where(hit2, 1.0, 0.0))
    colsum = jnp.sum(onehot, axis=0, keepdims=True)

    @pl.when((ph == 0) & (i == 0))
    def _():
        cnt_ref[...] = jnp.zeros_like(cnt_ref)

    @pl.when(ph == 0)
    def _():
        cnt_ref[0:1, :] += colsum

    @pl.when((ph == 1) & (i == 0))
    def _():
        ntile = jnp.ceil(cnt_ref[...] * (1.0 / MOE_TILE))
        off_tiles = _dot(ntile.astype(BF16), striu_ref[...])
        off_ref[...] = off_tiles * MOE_TILE
        info_ref[...] = jnp.zeros_like(info_ref)
        info_ref[0:1, :] = cnt_ref[0:1, :]
        info_ref[1:2, :] = off_tiles[0:1, :]
        carry_ref[...] = jnp.zeros_like(carry_ref)

    @pl.when(ph == 1)
    def _():
        before = _dot(tri_ref[...], onehot.astype(BF16)) + carry_ref[0:1, :] + off_ref[0:1, :]
        p1 = jnp.sum(jnp.where(hit1, before, 0.0), axis=1, keepdims=True)
        p2 = jnp.sum(jnp.where(hit2, before, 0.0), axis=1, keepdims=True)
        pos_ref[...] = jnp.where(lane == 0, p1, jnp.where(lane == 1, p2, 0.0)).astype(jnp.int32)
        carry_ref[0:1, :] += colsum


def _rank(rt, tm=512):
    T = rt.shape[0]
    tm = min(tm, T)
    idx = np.arange(tm)
    tri = jnp.asarray(idx[None, :] < idx[:, None], BF16)
    lanes = np.arange(LANES)
    striu = jnp.asarray(lanes[:, None] < lanes[None, :], BF16)
    return pl.pallas_call(
        _rank_kernel,
        grid=(2, T // tm),
        in_specs=[pl.BlockSpec((tm, LANES), lambda ph, i: (i, 0)),
                  pl.BlockSpec((tm, tm), lambda ph, i: (0, 0)),
                  pl.BlockSpec((LANES, LANES), lambda ph, i: (0, 0))],
        out_specs=[pl.BlockSpec((tm, LANES), lambda ph, i: (i * ph, 0)),
                   pl.BlockSpec((8, LANES), lambda ph, i: (0, 0))],
        out_shape=[jax.ShapeDtypeStruct((T, LANES), jnp.int32),
                   jax.ShapeDtypeStruct((8, LANES), F32)],
        scratch_shapes=[pltpu.VMEM((8, LANES), F32)] * 3,
        compiler_params=_cparams(("arbitrary", "arbitrary")),
        name="moe_rank",
    )(rt, tri, striu)


def _dispatch_kernel(pos_ref, h_ref, xs_init_ref, xs_ref, sem):
    del xs_init_ref
    tm = h_ref.shape[1]
    base = pl.program_id(0) * (2 * tm)

    def row_copy(t, k):
        return pltpu.make_async_copy(h_ref.at[:, t], xs_ref.at[:, pos_ref[base + 2 * t + k]], sem)

    def issue(t, _):
        row_copy(t, 0).start()
        row_copy(t, 1).start()
        return 0

    def drain(t, _):
        row_copy(t, 0).wait()
        row_copy(t, 1).wait()
        return 0

    lax.fori_loop(0, tm, issue, 0, unroll=8)
    lax.fori_loop(0, tm, drain, 0, unroll=8)


def _dispatch(pos_flat, h3, n_rows, tm=256):
    NC, T, _ = h3.shape
    tm = min(tm, T)
    return pl.pallas_call(
        _dispatch_kernel,
        grid_spec=pltpu.PrefetchScalarGridSpec(
            num_scalar_prefetch=1,
            grid=(T // tm,),
            in_specs=[pl.BlockSpec((NC, tm, LANES), lambda i, pos: (0, i, 0)),
                      pl.BlockSpec(memory_space=pl.ANY)],
            out_specs=pl.BlockSpec(memory_space=pl.ANY),
            scratch_shapes=[pltpu.SemaphoreType.DMA]),
        out_shape=jax.ShapeDtypeStruct((NC, n_rows, LANES), F32),
        input_output_aliases={2: 0},
        compiler_params=_cparams(("arbitrary",)),
        name="moe_dispatch",
    )(pos_flat, h3, jnp.zeros((NC, n_rows, LANES), F32))


def _experts_kernel(te_ref, nv_ref, xs_ref, w1_ref, w3_ref, w2_ref, ys_ref, w1b, w3b, w2b):
    j = pl.program_id(0)
    e = te_ref[j]
    e_prev = te_ref[jnp.maximum(j - 1, 0)]

    @pl.when((j == 0) | (e != e_prev))
    def _():
        w1b[...] = w1_ref[0].astype(BF16)
        w3b[...] = w3_ref[0].astype(BF16)
        w2b[...] = w2_ref[0].astype(BF16)

    @pl.when(j < nv_ref[0])
    def _():
        nc = xs_ref.shape[0]
        h = jnp.concatenate([xs_ref[s] for s in range(nc)], axis=1).astype(BF16)
        a = _dot(h, w1b[...])
        b = _dot(h, w3b[...])
        hid = (a * (1.0 / (1.0 + jnp.exp(-a))) * b).astype(BF16)
        y = _dot(hid, w2b[...])
        for s in range(nc):
            ys_ref[s] = y[:, s * LANES:(s + 1) * LANES]

    @pl.when(j >= nv_ref[0])
    def _():
        ys_ref[...] = jnp.zeros_like(ys_ref)


def _experts(tile_expert, n_valid, xs, w1, w3, w2):
    NC, P, _ = xs.shape
    NE, D, F = w1.shape
    nt = P // MOE_TILE
    tile = lambda j, te, nv: (0, jnp.minimum(j, nv[0] - 1), 0)
    wspec = lambda shape: pl.BlockSpec(shape, lambda j, te, nv: (te[j], 0, 0))
    return pl.pallas_call(
        _experts_kernel,
        grid_spec=pltpu.PrefetchScalarGridSpec(
            num_scalar_prefetch=2,
            grid=(nt,),
            in_specs=[pl.BlockSpec((NC, MOE_TILE, LANES), tile),
                      wspec((1, D, F)), wspec((1, D, F)), wspec((1, F, D))],
            out_specs=pl.BlockSpec((NC, MOE_TILE, LANES), lambda j, te, nv: (0, j, 0)),
            scratch_shapes=[pltpu.VMEM((D, F), BF16), pltpu.VMEM((D, F), BF16),
                            pltpu.VMEM((F, D), BF16)]),
        out_shape=jax.ShapeDtypeStruct((NC, P, LANES), F32),
        compiler_params=_cparams(("arbitrary",)),
        name="moe_experts",
    )(tile_expert, n_valid, xs, w1, w3, w2)


def _combine_kernel(pos_ref, x_ref, rt_ref, ga_ref, ys_ref, o_ref, buf, sems):
    i = pl.program_id(0)
    n = pl.num_programs(0)
    tm = x_ref.shape[0]
    nc = buf.shape[1]

    def row_copy(tile, slot, t, k):
        src = ys_ref.at[:, pos_ref[tile * (2 * tm) + 2 * t + k]]
        return pltpu.make_async_copy(src, buf.at[slot, :, k * tm + t], sems.at[slot])

    def issue(tile, slot):
        def body(t, _):
            row_copy(tile, slot, t, 0).start()
            row_copy(tile, slot, t, 1).start()
            return 0
        lax.fori_loop(0, tm, body, 0, unroll=8)

    @pl.when(i == 0)
    def _():
        issue(0, 0)

    @pl.when(i + 1 < n)
    def _():
        issue(i + 1, (i + 1) % 2)

    slot = i % 2

    def drain(t, _):
        row_copy(i, slot, t, 0).wait()
        row_copy(i, slot, t, 1).wait()
        return 0

    lax.fori_loop(0, tm, drain, 0, unroll=8)
    rt = rt_ref[...]
    w1 = rt[:, RT_W:RT_W + 1]
    w2 = rt[:, RT_W + 1:RT_W + 2]
    for s in range(nc):
        cols = slice(s * LANES, (s + 1) * LANES)
        y = w1 * buf[slot, s, 0:tm, :] + w2 * buf[slot, s, tm:2 * tm, :]
        o_ref[:, cols] = x_ref[:, cols] + ga_ref[0][:, cols] * y


def _combine(pos_flat, x1, rt, ga, ys, tm=256):
    B, S, D = x1.shape
    NC = ys.shape[0]
    tm = min(tm, S)
    nt = S // tm
    return pl.pallas_call(
        _combine_kernel,
        grid_spec=pltpu.PrefetchScalarGridSpec(
            num_scalar_prefetch=1,
            grid=(B * nt,),
            in_specs=[pl.BlockSpec((tm, D), lambda i, pos: (i, 0)),
                      pl.BlockSpec((tm, LANES), lambda i, pos: (i, 0)),
                      pl.BlockSpec((1, 1, D), lambda i, pos: (i // nt, 0, 0)),
                      pl.BlockSpec(memory_space=pl.ANY)],
            out_specs=pl.BlockSpec((tm, D), lambda i, pos: (i, 0)),
            scratch_shapes=[pltpu.VMEM((2, NC, 2 * tm, LANES), F32),
                            pltpu.SemaphoreType.DMA((2,))]),
        out_shape=jax.ShapeDtypeStruct((B * S, D), F32),
        compiler_params=_cparams(("arbitrary",)),
        name="moe_combine",
    )(pos_flat, x1.reshape(B * S, D), rt, ga, ys).reshape(B, S, D)


def _moe(h3, rt, x1, ga, w1, w3, w2):
    T = rt.shape[0]
    n_tiles = 2 * T // MOE_TILE + N_EXPERTS
    pos, info = _rank(rt)
    pos_flat = pos[:, :2].reshape(-1)
    counts = info[0, :N_EXPERTS]
    ends = info[1, :N_EXPERTS] + jnp.ceil(counts * (1.0 / MOE_TILE))
    tile_ids = jnp.arange(n_tiles, dtype=F32)
    tile_expert = jnp.minimum(jnp.sum(tile_ids[:, None] >= ends[None, :], axis=1), N_EXPERTS - 1)
    n_valid = ends[N_EXPERTS - 1:].astype(jnp.int32)
    xs = _dispatch(pos_flat, h3, n_tiles * MOE_TILE)
    ys = _experts(tile_expert.astype(jnp.int32), n_valid, xs, w1, w3, w2)
    return _combine(pos_flat, x1, rt, ga, ys)


def kernel(x, c, w_mod, b_mod, g_norm1, w_in, gq_a, gk_a, lam_a, g_sub_a, w_pool, b_pool, pool_scale, gq_c, gk_c, w_out, g_norm2, w_rg, b_rg, w_re, b_re, w1, w3, w2):
    B, S, D = x.shape
    L = w_mod.shape[0]
    a_width = D // 2
    pool_width = D // 4
    c_width = D // 4
    W = MXU_DIM
    reps = W // HEAD_DIM

    inv = 1.0 / (ROPE_THETA ** (jnp.arange(0, HEAD_DIM, 2, dtype=F32) / HEAD_DIM))
    ang = jnp.arange(S, dtype=F32)[:, None] * inv[None, :]
    ang = jnp.concatenate([ang, ang], axis=-1)
    cos_h, sin_h = jnp.cos(ang), jnp.sin(ang)
    first = jnp.arange(HEAD_DIM) < HEAD_DIM // 2
    cos_t = jnp.tile(cos_h, (1, reps))
    sa_t = jnp.tile(jnp.where(first[None, :], 0.0, sin_h), (1, reps))
    sb_t = jnp.tile(jnp.where(first[None, :], -sin_h, 0.0), (1, reps))
    head_of = np.arange(W) // HEAD_DIM
    bd = jnp.asarray((head_of[:, None] == head_of[None, :]) / HEAD_DIM, BF16)

    qa0, ka0, va0 = 0, a_width, 2 * a_width
    ub0 = 3 * a_width
    qc0, kc0, vc0 = ub0 + pool_width, ub0 + pool_width + c_width, ub0 + pool_width + 2 * c_width
    chunks = lambda lo_, hi_: list(range(lo_ // W, hi_ // W))
    z_chunks = tuple([(j, True) for j in chunks(qa0, va0)] + [(j, False) for j in chunks(ub0, qc0)]
                     + [(j, True) for j in chunks(qc0, vc0)])
    v_chunks = tuple(chunks(va0, ub0) + chunks(vc0, w_in.shape[2]))
    zq_a, zk_a = 0, a_width // LANES
    z_ub = 2 * a_width // pool_width
    zq_c = (2 * a_width + pool_width) // LANES
    zk_c = zq_c + c_width // LANES
    scale = HEAD_DIM ** -0.5
    cnt_t, bias_t = _dilated_tables(S)

    mod = _modulation(c, w_mod, b_mod)
    for l in range(L):
        sh1, sc1, ga1, sh2, sc2, ga2 = [m[:, None, :] for m in jnp.split(mod[l], N_MOD, axis=-1)]
        gain = jnp.ones((w_in.shape[2],), F32)
        gain = gain.at[qa0:ka0].set(jnp.tile(gq_a[l], a_width // HEAD_DIM) * scale)
        gain = gain.at[ka0:va0].set(jnp.tile(gk_a[l], a_width // HEAD_DIM))
        gain = gain.at[qc0:kc0].set(jnp.tile(gq_c[l], c_width // HEAD_DIM) * scale)
        gain = gain.at[kc0:vc0].set(jnp.tile(gk_c[l], c_width // HEAD_DIM))
        z, vt = _in_projection(x, sc1, sh1, g_norm1[l][None, :], w_in[l].astype(BF16),
                               gain[None, :], cos_t, sa_t, sb_t, bd, z_chunks, v_chunks)

        lam_init = 0.8 - 0.6 * math.exp(-0.3 * l)
        n_a = a_width // LANES
        ya = _attention("diff", z, vt, zq_a, zk_a, 0, n_a,
                        [lam_a[l], g_sub_a[l][:, None]],
                        [pl.BlockSpec(lam_a[l].shape, lambda b, h: (0, 0)),
                         pl.BlockSpec((LANES, 1), lambda b, h: (0, 0))], lam_init=lam_init)
        n_c = c_width // LANES
        yc = _attention("dil", z, vt, zq_c, zk_c, n_a, n_c,
                        [cnt_t, bias_t],
                        [pl.BlockSpec(cnt_t.shape, lambda b, h: (0, 0, 0)),
                         pl.BlockSpec(bias_t.shape, lambda b, h: (0, 0, 0))])
        w_bd = jax.scipy.linalg.block_diag(*[w_pool[l, g] for g in range(w_pool.shape[1])])
        yb = _pool_mixer(z, z_ub, w_bd.astype(BF16), b_pool[l].reshape(1, -1),
                         pool_scale[l][None, :])

        wr = jnp.zeros((D, LANES), F32)
        wr = wr.at[:, :N_GROUPS].set(w_rg[l])
        wr = wr.at[:, N_GROUPS:N_GROUPS + N_EXPERTS].set(
            w_re[l].transpose(1, 0, 2).reshape(D, N_EXPERTS))
        br = jnp.zeros((1, LANES), F32)
        br = br.at[0, :N_GROUPS].set(b_rg[l])
        br = br.at[0, N_GROUPS:N_GROUPS + N_EXPERTS].set(b_re[l].reshape(-1))
        x1, h3, rt = _out_projection(ya, yb, yc, x, w_out[l].astype(BF16), ga1, sc2, sh2,
                                     g_norm2[l][None, :], wr.astype(BF16), br)
        F = w1.shape[-1]
        x = _moe(h3, rt, x1, ga2, w1[l].reshape(N_EXPERTS, D, F), w3[l].reshape(N_EXPERTS, D, F),
                 w2[l].reshape(N_EXPERTS, F, D))
    return x
```

```python
import functools
import math

import jax
import jax.numpy as jnp
import numpy as np
from jax import lax
from jax.experimental import pallas as pl
from jax.experimental.pallas import tpu as pltpu

HEAD_DIM = 64
POOL_WINDOWS = (2, 4, 8, 16)
DILATED_PAIRS = ((128, 1), (512, 4), (2048, 16))
ROPE_THETA = 10000.0
N_GROUPS = 4
EXPERTS_PER_GROUP = 8
N_EXPERTS = N_GROUPS * EXPERTS_PER_GROUP
N_MOD = 6
EPS = 1e-6

LANES = 128
MXU_DIM = 256
KV_BLOCK = 256
Q_BLOCK = 2 * KV_BLOCK
MOE_TILE = 256
RT_E, RT_W = 0, 2
VMEM_LIMIT = 48 * 1024 * 1024
NEG = -1e30
F32 = jnp.float32
BF16 = jnp.bfloat16


def _cparams(sem):
    return pltpu.CompilerParams(dimension_semantics=sem, vmem_limit_bytes=VMEM_LIMIT)


def _dot(a, b):
    return jnp.dot(a, b, preferred_element_type=F32)


def _dot_nt(a, b):
    return lax.dot_general(a, b, (((1,), (1,)), ((), ())), preferred_element_type=F32)


def _mod_kernel(c_ref, w_ref, b_ref, o_ref):
    c = c_ref[...]
    cond = c * (1.0 / (1.0 + jnp.exp(-c)))
    o_ref[0] = _dot(cond.astype(BF16), w_ref[0].astype(BF16)) + b_ref[0]


def _modulation(c, w_mod, b_mod):
    L, D, N = w_mod.shape
    B = c.shape[0]
    tn = 1024
    return pl.pallas_call(
        _mod_kernel,
        grid=(L, N // tn),
        in_specs=[pl.BlockSpec((B, D), lambda l, j: (0, 0)),
                  pl.BlockSpec((1, D, tn), lambda l, j: (l, 0, j)),
                  pl.BlockSpec((1, 1, tn), lambda l, j: (l, 0, j))],
        out_specs=pl.BlockSpec((1, B, tn), lambda l, j: (l, 0, j)),
        out_shape=jax.ShapeDtypeStruct((L, B, N), F32),
        compiler_params=_cparams(("parallel", "parallel")),
        name="modulation",
    )(c, w_mod, b_mod.reshape(L, 1, N))


def _inproj_kernel(z_chunks, v_chunks, x_ref, sc_ref, sh_ref, g_ref, w_ref, gain_ref, cos_ref,
                   sa_ref, sb_ref, bd_ref, z_ref, vt_ref):
    x = x_ref[0]
    ms = jnp.mean(x * x, axis=-1, keepdims=True)
    h = x * lax.rsqrt(ms + EPS) * g_ref[...]
    h = h * (1.0 + sc_ref[0]) + sh_ref[0]
    hb = h.astype(BF16)
    W = MXU_DIM
    for dst, (src, normed) in enumerate(z_chunks):
        zc = _dot(hb, w_ref[:, src * W:(src + 1) * W])
        if normed:
            msq = _dot((zc * zc).astype(BF16), bd_ref[...])
            y = zc * lax.rsqrt(msq + EPS) * gain_ref[:, src * W:(src + 1) * W]
            r_up = pltpu.roll(y, HEAD_DIM // 2, 1)
            r_dn = pltpu.roll(y, W - HEAD_DIM // 2, 1)
            zc = y * cos_ref[...] + r_up * sa_ref[...] + r_dn * sb_ref[...]
        z_ref[0, :, dst * W:(dst + 1) * W] = zc.astype(BF16)
    for dst, src in enumerate(v_chunks):
        zt = _dot(hb, w_ref[:, src * W:(src + 1) * W]).T.astype(BF16)
        for cb in range(vt_ref.shape[1]):
            vt_ref[0, cb, dst * W:(dst + 1) * W, :] = zt[:, cb * KV_BLOCK:(cb + 1) * KV_BLOCK]


def _in_projection(x, sc, sh, g, w_bf, gain, cos_t, sa_t, sb_t, bd, z_chunks, v_chunks, tm=512):
    B, S, D = x.shape
    N = w_bf.shape[1]
    W = MXU_DIM
    nz, nv = len(z_chunks) * W, len(v_chunks) * W
    return pl.pallas_call(
        functools.partial(_inproj_kernel, z_chunks, v_chunks),
        grid=(B, S // tm),
        in_specs=[pl.BlockSpec((1, tm, D), lambda b, i: (b, i, 0)),
                  pl.BlockSpec((1, 1, D), lambda b, i: (b, 0, 0)),
                  pl.BlockSpec((1, 1, D), lambda b, i: (b, 0, 0)),
                  pl.BlockSpec((1, D), lambda b, i: (0, 0)),
                  pl.BlockSpec((D, N), lambda b, i: (0, 0)),
                  pl.BlockSpec((1, N), lambda b, i: (0, 0)),
                  pl.BlockSpec((tm, W), lambda b, i: (i, 0)),
                  pl.BlockSpec((tm, W), lambda b, i: (i, 0)),
                  pl.BlockSpec((tm, W), lambda b, i: (i, 0)),
                  pl.BlockSpec((W, W), lambda b, i: (0, 0))],
        out_specs=[pl.BlockSpec((1, tm, nz), lambda b, i: (b, i, 0)),
                   pl.BlockSpec((1, tm // KV_BLOCK, nv, KV_BLOCK), lambda b, i: (b, i, 0, 0))],
        out_shape=[jax.ShapeDtypeStruct((B, S, nz), BF16),
                   jax.ShapeDtypeStruct((B, S // KV_BLOCK, nv, KV_BLOCK), BF16)],
        compiler_params=_cparams(("parallel", "parallel")),
        name="in_projection",
    )(x, sc, sh, g, w_bf, gain, cos_t, sa_t, sb_t, bd)


def _attn_kernel(mode, lam_init, *refs):
    if mode == "diff":
        q_ref, k_ref, vt_ref, lam_ref, g_ref, o_ref, s_a, s_b, acc_ref = refs
    else:
        q_ref, k_ref, vt_ref, cnt_ref, bias_ref, o_ref, s_a, s_b, acc_ref = refs
    bq, bk = Q_BLOCK, KV_BLOCK
    S = q_ref.shape[1]
    nq = S // bq
    lane = lax.broadcasted_iota(jnp.int32, (1, LANES), 1)
    lo = lane < HEAD_DIM

    def q_block(qi, _):
        qs = pl.multiple_of(qi * bq, bq)
        q = q_ref[0, pl.ds(qs, bq), :]
        zero = jnp.zeros_like(q)
        q2 = jnp.concatenate([jnp.where(lo, q, zero), jnp.where(lo, zero, q)], axis=0)

        def scores(s_ref, kb):
            ks = pl.multiple_of(kb * bk, bk)
            s_ref[...] = _dot_nt(k_ref[0, pl.ds(ks, bk), :], q2)

        def tables(kb):
            if mode == "diff":
                return None, None
            t = jnp.minimum(2 * qi - kb + 1, cnt_ref.shape[0] - 1)
            return bias_ref[t], cnt_ref[t]

        def update(s_ref, kb, stats, bias, cnt):
            vt = vt_ref[0, kb]
            out = []
            for half in range(2):
                cols = slice(half * bq, (half + 1) * bq)
                m, l = stats[2 * half], stats[2 * half + 1]
                s = s_ref[:, cols]
                if bias is not None:
                    s = s + bias
                mn = jnp.maximum(m, jnp.max(s, axis=0, keepdims=True))
                alpha = jnp.exp(m - mn)
                p = jnp.exp(s - mn)
                if cnt is not None:
                    p = p * cnt
                l = alpha * l + jnp.sum(p, axis=0, keepdims=True)
                acc_ref[:, cols] = alpha * acc_ref[:, cols] + _dot(vt, p.astype(BF16))
                out += [mn, l]
            return tuple(out)

        def pair(j, stats):
            scores(s_b, 2 * j + 1)
            stats = update(s_a, 2 * j, stats, *tables(2 * j))
            scores(s_a, 2 * j + 2)
            return update(s_b, 2 * j + 1, stats, *tables(2 * j + 1))

        acc_ref[...] = jnp.zeros_like(acc_ref)
        scores(s_a, 0)
        init = (jnp.full((1, bq), NEG, F32), jnp.zeros((1, bq), F32)) * 2
        stats = lax.fori_loop(0, qi, pair, init)

        krow = lax.broadcasted_iota(jnp.int32, (bk, bq), 0)
        qcol = lax.broadcasted_iota(jnp.int32, (bk, bq), 1)
        scores(s_b, 2 * qi + 1)
        for rel, s_ref in enumerate((s_a, s_b)):
            bias, cnt = tables(2 * qi + rel)
            if mode == "diff":
                bias = jnp.where(krow + rel * bk <= qcol, 0.0, NEG).astype(F32)
            stats = update(s_ref, 2 * qi + rel, stats, bias, cnt)
        _, l_lo, _, l_hi = stats
        a_lo, a_hi = acc_ref[:, :bq], acc_ref[:, bq:]
        if mode == "diff":
            lp = lam_ref[...]
            lam = (jnp.exp(jnp.sum(lp[0:1] * lp[1:2], axis=-1, keepdims=True))
                   - jnp.exp(jnp.sum(lp[2:3] * lp[3:4], axis=-1, keepdims=True)) + lam_init)
            y = a_lo * (1.0 / l_lo) - a_hi * (lam / l_hi)
            ms = jnp.mean(y * y, axis=0, keepdims=True)
            y = y * lax.rsqrt(ms + EPS) * (g_ref[...] * (1.0 - lam_init))
        else:
            feat = lax.broadcasted_iota(jnp.int32, (LANES, 1), 0)
            y = jnp.where(feat < HEAD_DIM, a_lo * (1.0 / l_lo), a_hi * (1.0 / l_hi))
        o_ref[0, pl.ds(qs, bq), :] = y.T.astype(o_ref.dtype)
        return 0

    lax.fori_loop(0, nq, q_block, 0)


def _attention(mode, z, vt, q_col, k_col, v_row, n_blocks, extras, extra_specs, lam_init=0.0):
    B, S, _ = z.shape
    zspec = lambda col: pl.BlockSpec((1, S, LANES), lambda b, h: (b, 0, col + h))
    vspec = pl.BlockSpec((1, S // KV_BLOCK, LANES, KV_BLOCK), lambda b, h: (b, 0, v_row + h, 0))
    return pl.pallas_call(
        functools.partial(_attn_kernel, mode, lam_init),
        grid=(B, n_blocks),
        in_specs=[zspec(q_col), zspec(k_col), vspec] + extra_specs,
        out_specs=pl.BlockSpec((1, S, LANES), lambda b, h: (b, 0, h)),
        out_shape=jax.ShapeDtypeStruct((B, S, n_blocks * LANES), BF16),
        scratch_shapes=[pltpu.VMEM((KV_BLOCK, 2 * Q_BLOCK), F32), pltpu.VMEM((KV_BLOCK, 2 * Q_BLOCK), F32),
                        pltpu.VMEM((LANES, 2 * Q_BLOCK), F32)],
        compiler_params=_cparams(("parallel", "parallel")),
        name=mode + "_attention",
    )(z, z, vt, *extras)


def _dilated_tables(S):
    bq, bk = Q_BLOCK, KV_BLOCK
    dds = np.arange(-1, S // bk)
    dist = (dds[:, None, None] * bk + np.arange(bq)[None, None, :] - np.arange(bk)[None, :, None])
    cnt = np.zeros(dist.shape, np.float32)
    for window, dil in DILATED_PAIRS:
        cnt += (dist >= 0) & (dist <= window) & (dist % dil == 0)
    n = len(dds)
    while n > 1 and np.array_equal(cnt[n - 1], cnt[n - 2]):
        n -= 1
    cnt = cnt[:n]
    bias = np.where(cnt > 0, 0.0, NEG).astype(np.float32)
    return jnp.asarray(cnt), jnp.asarray(bias)


def _pool_kernel(u_ref, w_ref, b_ref, sc_ref, o_ref):
    u = u_ref[0].astype(F32)
    S, C = u.shape
    row = lax.broadcasted_iota(jnp.int32, (S, C), 0)
    lane = lax.broadcasted_iota(jnp.int32, (S, C), 1)
    grp = lane // (C // len(POOL_WINDOWS))

    def shifted(a, k):
        return jnp.where(row >= k, pltpu.roll(a, k, 0), 0.0)

    acc = u
    win = jnp.zeros_like(u)
    width = 1
    for gi, w in enumerate(POOL_WINDOWS):
        while width < w:
            acc = acc + shifted(acc, width)
            width *= 2
        win = jnp.where(grp == gi, acc, win)
    wl = jnp.zeros_like(row)
    for gi, w in enumerate(POOL_WINDOWS):
        wl = jnp.where(grp == gi, w, wl)
    cnt = jnp.minimum(row + 1, wl).astype(F32)
    d = win / cnt - u
    y = _dot(d.astype(BF16), w_ref[...]) + b_ref[...]
    o_ref[0] = (y * sc_ref[...]).astype(o_ref.dtype)


def _pool_mixer(z, col_block, w_bd, b, scale):
    B, S, _ = z.shape
    C = w_bd.shape[0]
    return pl.pallas_call(
        _pool_kernel,
        grid=(B,),
        in_specs=[pl.BlockSpec((1, S, C), lambda b_: (b_, 0, col_block)),
                  pl.BlockSpec((C, C), lambda b_: (0, 0)),
                  pl.BlockSpec((1, C), lambda b_: (0, 0)),
                  pl.BlockSpec((1, C), lambda b_: (0, 0))],
        out_specs=pl.BlockSpec((1, S, C), lambda b_: (b_, 0, 0)),
        out_shape=jax.ShapeDtypeStruct((B, S, C), BF16),
        compiler_params=_cparams(("parallel",)),
        name="pool_mixer",
    )(z, w_bd, b, scale)


def _outproj_kernel(ya_ref, yb_ref, yc_ref, x_ref, wo_ref, ga_ref, sc_ref, sh_ref, g_ref, wr_ref,
                    br_ref, x1_ref, h_ref, rt_ref):
    na = ya_ref.shape[2]
    nb = yb_ref.shape[2]
    y = (_dot(ya_ref[0], wo_ref[0:na, :]) + _dot(yb_ref[0], wo_ref[na:na + nb, :])
         + _dot(yc_ref[0], wo_ref[na + nb:, :]))
    x1 = x_ref[0] + ga_ref[0] * y
    x1_ref[0] = x1
    ms = jnp.mean(x1 * x1, axis=-1, keepdims=True)
    h = x1 * lax.rsqrt(ms + EPS) * g_ref[...]
    h = h * (1.0 + sc_ref[0]) + sh_ref[0]
    hb = h.astype(BF16)
    for s in range(h_ref.shape[0]):
        h_ref[s] = h[:, s * LANES:(s + 1) * LANES]

    logits = _dot(hb, wr_ref[...]) + br_ref[...]
    tm = logits.shape[0]
    G, E = N_GROUPS, EXPERTS_PER_GROUP
    lane = lax.broadcasted_iota(jnp.int32, (tm, LANES), 1)
    lane_f = lane.astype(F32)
    big = float(LANES)
    is_g = lane < G
    gl = jnp.where(is_g, logits, -jnp.inf)
    gmax = jnp.max(gl, axis=-1, keepdims=True)
    gidx = jnp.min(jnp.where(gl == gmax, lane_f, big), axis=-1, keepdims=True)
    gsum = jnp.sum(jnp.where(is_g, jnp.exp(gl - gmax), 0.0), axis=-1, keepdims=True)
    g_w = 1.0 / gsum
    lane_grp = ((lane - G) // E).astype(F32)
    emask = (lane >= G) & (lane < G + G * E) & (lane_grp == gidx)
    el = jnp.where(emask, logits, -jnp.inf)
    v1 = jnp.max(el, axis=-1, keepdims=True)
    i1 = jnp.min(jnp.where(el == v1, lane_f, big), axis=-1, keepdims=True)
    el2 = jnp.where(lane_f == i1, -jnp.inf, el)
    v2 = jnp.max(el2, axis=-1, keepdims=True)
    i2 = jnp.min(jnp.where(el2 == v2, lane_f, big), axis=-1, keepdims=True)
    t = jnp.exp(v2 - v1)
    w1 = g_w / (1.0 + t)
    w2 = w1 * t
    rt_ref[...] = jnp.where(lane == RT_E, i1 - G, jnp.where(lane == RT_E + 1, i2 - G,
                            jnp.where(lane == RT_W, w1, jnp.where(lane == RT_W + 1, w2, 0.0))))


def _out_projection(ya, yb, yc, x, wo_bf, ga, sc, sh, g, wr, br, tm=512):
    B, S, D = x.shape
    nt = S // tm
    tok = lambda n: pl.BlockSpec((1, tm, n), lambda b, i: (b, i, 0))
    per_b = pl.BlockSpec((1, 1, D), lambda b, i: (b, 0, 0))
    const = lambda shape: pl.BlockSpec(shape, lambda b, i: (0, 0))
    return pl.pallas_call(
        _outproj_kernel,
        grid=(B, nt),
        in_specs=[tok(ya.shape[2]), tok(yb.shape[2]), tok(yc.shape[2]), tok(D), const((D, D)),
                  per_b, per_b, per_b, const((1, D)), const((D, LANES)), const((1, LANES))],
        out_specs=[tok(D), pl.BlockSpec((D // LANES, tm, LANES), lambda b, i: (0, b * nt + i, 0)),
                   pl.BlockSpec((tm, LANES), lambda b, i: (b * nt + i, 0))],
        out_shape=[jax.ShapeDtypeStruct((B, S, D), F32),
                   jax.ShapeDtypeStruct((D // LANES, B * S, LANES), F32),
                   jax.ShapeDtypeStruct((B * S, LANES), F32)],
        compiler_params=_cparams(("parallel", "parallel")),
        name="out_projection",
    )(ya, yb, yc, x, wo_bf, ga, sc, sh, g, wr, br)


def _rank_kernel(rt_ref, tri_ref, striu_ref, pos_ref, info_ref, cnt_ref, off_ref, carry_ref):
    ph, i = pl.program_id(0), pl.program_id(1)
    rt = rt_ref[...]
    lane = lax.broadcasted_iota(jnp.int32, rt.shape, 1)
    lane_f = lane.astype(F32)
    hit1 = lane_f == rt[:, RT_E:RT_E + 1]
    hit2 = lane_f == rt[:, RT_E + 1:RT_E + 2]
    onehot = jnp.where(hit1, 1.0, jnp.where(hit2, 1.0, 0.0))
    colsum = jnp.sum(onehot, axis=0, keepdims=True)

    @pl.when((ph == 0) & (i == 0))
    def _():
        cnt_ref[...] = jnp.zeros_like(cnt_ref)

    @pl.when(ph == 0)
    def _():
        cnt_ref[0:1, :] += colsum

    @pl.when((ph == 1) & (i == 0))
    def _():
        ntile = jnp.ceil(cnt_ref[...] * (1.0 / MOE_TILE))
        off_tiles = _dot(ntile.astype(BF16), striu_ref[...])
        off_ref[...] = off_tiles * MOE_TILE
        info_ref[...] = jnp.zeros_like(info_ref)
        info_ref[0:1, :] = cnt_ref[0:1, :]
        info_ref[1:2, :] = off_tiles[0:1, :]
        carry_ref[...] = jnp.zeros_like(carry_ref)

    @pl.when(ph == 1)
    def _():
        before = _dot(tri_ref[...], onehot.astype(BF16)) + carry_ref[0:1, :] + off_ref[0:1, :]
        p1 = jnp.sum(jnp.where(hit1, before, 0.0), axis=1, keepdims=True)
        p2 = jnp.sum(jnp.where(hit2, before, 0.0), axis=1, keepdims=True)
        pos_ref[...] = jnp.where(lane == 0, p1, jnp.where(lane == 1, p2, 0.0)).astype(jnp.int32)
        carry_ref[0:1, :] += colsum


def _rank(rt, tm=512):
    T = rt.shape[0]
    tm = min(tm, T)
    idx = np.arange(tm)
    tri = jnp.asarray(idx[None, :] < idx[:, None], BF16)
    lanes = np.arange(LANES)
    striu = jnp.asarray(lanes[:, None] < lanes[None, :], BF16)
    return pl.pallas_call(
        _rank_kernel,
        grid=(2, T // tm),
        in_specs=[pl.BlockSpec((tm, LANES), lambda ph, i: (i, 0)),
                  pl.BlockSpec((tm, tm), lambda ph, i: (0, 0)),
                  pl.BlockSpec((LANES, LANES), lambda ph, i: (0, 0))],
        out_specs=[pl.BlockSpec((tm, LANES), lambda ph, i: (i * ph, 0)),
                   pl.BlockSpec((8, LANES), lambda ph, i: (0, 0))],
        out_shape=[jax.ShapeDtypeStruct((T, LANES), jnp.int32),
                   jax.ShapeDtypeStruct((8, LANES), F32)],
        scratch_shapes=[pltpu.VMEM((8, LANES), F32)] * 3,
        compiler_params=_cparams(("arbitrary", "arbitrary")),
        name="moe_rank",
    )(rt, tri, striu)


def _dispatch_kernel(pos_ref, h_ref, xs_init_ref, xs_ref, sem):
    del xs_init_ref
    tm = h_ref.shape[1]
    base = pl.program_id(0) * (2 * tm)

    def row_copy(t, k):
        return pltpu.make_async_copy(h_ref.at[:, t], xs_ref.at[:, pos_ref[base + 2 * t + k]], sem)

    def issue(t, _):
        row_copy(t, 0).start()
        row_copy(t, 1).start()
        return 0

    def drain(t, _):
        row_copy(t, 0).wait()
        row_copy(t, 1).wait()
        return 0

    lax.fori_loop(0, tm, issue, 0, unroll=8)
    lax.fori_loop(0, tm, drain, 0, unroll=8)


def _dispatch(pos_flat, h3, n_rows, tm=256):
    NC, T, _ = h3.shape
    tm = min(tm, T)
    return pl.pallas_call(
        _dispatch_kernel,
        grid_spec=pltpu.PrefetchScalarGridSpec(
            num_scalar_prefetch=1,
            grid=(T // tm,),
            in_specs=[pl.BlockSpec((NC, tm, LANES), lambda i, pos: (0, i, 0)),
                      pl.BlockSpec(memory_space=pl.ANY)],
            out_specs=pl.BlockSpec(memory_space=pl.ANY),
            scratch_shapes=[pltpu.SemaphoreType.DMA]),
        out_shape=jax.ShapeDtypeStruct((NC, n_rows, LANES), F32),
        input_output_aliases={2: 0},
        compiler_params=_cparams(("arbitrary",)),
        name="moe_dispatch",
    )(pos_flat, h3, jnp.zeros((NC, n_rows, LANES), F32))


def _experts_kernel(te_ref, nv_ref, xs_ref, w1_ref, w3_ref, w2_ref, ys_ref, w1b, w3b, w2b):
    j = pl.program_id(0)
    e = te_ref[j]
    e_prev = te_ref[jnp.maximum(j - 1, 0)]

    @pl.when((j == 0) | (e != e_prev))
    def _():
        w1b[...] = w1_ref[0].astype(BF16)
        w3b[...] = w3_ref[0].astype(BF16)
        w2b[...] = w2_ref[0].astype(BF16)

    @pl.when(j < nv_ref[0])
    def _():
        nc = xs_ref.shape[0]
        h = jnp.concatenate([xs_ref[s] for s in range(nc)], axis=1).astype(BF16)
        a = _dot(h, w1b[...])
        b = _dot(h, w3b[...])
        hid = (a * (1.0 / (1.0 + jnp.exp(-a))) * b).astype(BF16)
        y = _dot(hid, w2b[...])
        for s in range(nc):
            ys_ref[s] = y[:, s * LANES:(s + 1) * LANES]

    @pl.when(j >= nv_ref[0])
    def _():
        ys_ref[...] = jnp.zeros_like(ys_ref)


def _experts(tile_expert, n_valid, xs, w1, w3, w2):
    NC, P, _ = xs.shape
    NE, D, F = w1.shape
    nt = P // MOE_TILE
    tile = lambda j, te, nv: (0, jnp.minimum(j, nv[0] - 1), 0)
    wspec = lambda shape: pl.BlockSpec(shape, lambda j, te, nv: (te[j], 0, 0))
    return pl.pallas_call(
        _experts_kernel,
        grid_spec=pltpu.PrefetchScalarGridSpec(
            num_scalar_prefetch=2,
            grid=(nt,),
            in_specs=[pl.BlockSpec((NC, MOE_TILE, LANES), tile),
                      wspec((1, D, F)), wspec((1, D, F)), wspec((1, F, D))],
            out_specs=pl.BlockSpec((NC, MOE_TILE, LANES), lambda j, te, nv: (0, j, 0)),
            scratch_shapes=[pltpu.VMEM((D, F), BF16), pltpu.VMEM((D, F), BF16),
                            pltpu.VMEM((F, D), BF16)]),
        out_shape=jax.ShapeDtypeStruct((NC, P, LANES), F32),
        compiler_params=_cparams(("arbitrary",)),
        name="moe_experts",
    )(tile_expert, n_valid, xs, w1, w3, w2)


def _combine_kernel(pos_ref, x_ref, rt_ref, ga_ref, ys_ref, o_ref, buf, sems):
    i = pl.program_id(0)
    n = pl.num_programs(0)
    tm = x_ref.shape[0]
    nc = buf.shape[1]

    def row_copy(tile, slot, t, k):
        src = ys_ref.at[:, pos_ref[tile * (2 * tm) + 2 * t + k]]
        return pltpu.make_async_copy(src, buf.at[slot, :, k * tm + t], sems.at[slot])

    def issue(tile, slot):
        def body(t, _):
            row_copy(tile, slot, t, 0).start()
            row_copy(tile, slot, t, 1).start()
            return 0
        lax.fori_loop(0, tm, body, 0, unroll=8)

    @pl.when(i == 0)
    def _():
        issue(0, 0)

    @pl.when(i + 1 < n)
    def _():
        issue(i + 1, (i + 1) % 2)

    slot = i % 2

    def drain(t, _):
        row_copy(i, slot, t, 0).wait()
        row_copy(i, slot, t, 1).wait()
        return 0

    lax.fori_loop(0, tm, drain, 0, unroll=8)
    rt = rt_ref[...]
    w1 = rt[:, RT_W:RT_W + 1]
    w2 = rt[:, RT_W + 1:RT_W + 2]
    for s in range(nc):
        cols = slice(s * LANES, (s + 1) * LANES)
        y = w1 * buf[slot, s, 0:tm, :] + w2 * buf[slot, s, tm:2 * tm, :]
        o_ref[:, cols] = x_ref[:, cols] + ga_ref[0][:, cols] * y


def _combine(pos_flat, x1, rt, ga, ys, tm=256):
    B, S, D = x1.shape
    NC = ys.shape[0]
    tm = min(tm, S)
    nt = S // tm
    return pl.pallas_call(
        _combine_kernel,
        grid_spec=pltpu.PrefetchScalarGridSpec(
            num_scalar_prefetch=1,
            grid=(B * nt,),
            in_specs=[pl.BlockSpec((tm, D), lambda i, pos: (i, 0)),
                      pl.BlockSpec((tm, LANES), lambda i, pos: (i, 0)),
                      pl.BlockSpec((1, 1, D), lambda i, pos: (i // nt, 0, 0)),
                      pl.BlockSpec(memory_space=pl.ANY)],
            out_specs=pl.BlockSpec((tm, D), lambda i, pos: (i, 0)),
            scratch_shapes=[pltpu.VMEM((2, NC, 2 * tm, LANES), F32),
                            pltpu.SemaphoreType.DMA((2,))]),
        out_shape=jax.ShapeDtypeStruct((B * S, D), F32),
        compiler_params=_cparams(("arbitrary",)),
        name="moe_combine",
    )(pos_flat, x1.reshape(B * S, D), rt, ga, ys).reshape(B, S, D)


def _moe(h3, rt, x1, ga, w1, w3, w2, first_expert):
    T = rt.shape[0]
    n_tiles = 2 * T // MOE_TILE + N_EXPERTS
    pos, info = _rank(rt)
    pos_flat = pos[:, :2].reshape(-1)
    counts = info[0, :N_EXPERTS]
    ends = info[1, :N_EXPERTS] + jnp.ceil(counts * (1.0 / MOE_TILE))
    tile_ids = jnp.arange(n_tiles, dtype=F32)
    tile_expert = jnp.minimum(jnp.sum(tile_ids[:, None] >= ends[None, :], axis=1), N_EXPERTS - 1)
    n_valid = ends[N_EXPERTS - 1:].astype(jnp.int32)
    xs = _dispatch(pos_flat, h3, n_tiles * MOE_TILE)
    ys = _experts(tile_expert.astype(jnp.int32) + first_expert, n_valid, xs, w1, w3, w2)
    return _combine(pos_flat, x1, rt, ga, ys)


def kernel(x, c, w_mod, b_mod, g_norm1, w_in, gq_a, gk_a, lam_a, g_sub_a, w_pool, b_pool, pool_scale, gq_c, gk_c, w_out, g_norm2, w_rg, b_rg, w_re, b_re, w1, w3, w2):
    B, S, D = x.shape
    L = w_mod.shape[0]
    a_width = D // 2
    pool_width = D // 4
    c_width = D // 4
    W = MXU_DIM
    reps = W // HEAD_DIM

    inv = 1.0 / (ROPE_THETA ** (jnp.arange(0, HEAD_DIM, 2, dtype=F32) / HEAD_DIM))
    ang = jnp.arange(S, dtype=F32)[:, None] * inv[None, :]
    ang = jnp.concatenate([ang, ang], axis=-1)
    cos_h, sin_h = jnp.cos(ang), jnp.sin(ang)
    first = jnp.arange(HEAD_DIM) < HEAD_DIM // 2
    cos_t = jnp.tile(cos_h, (1, reps))
    sa_t = jnp.tile(jnp.where(first[None, :], 0.0, sin_h), (1, reps))
    sb_t = jnp.tile(jnp.where(first[None, :], -sin_h, 0.0), (1, reps))
    head_of = np.arange(W) // HEAD_DIM
    bd = jnp.asarray((head_of[:, None] == head_of[None, :]) / HEAD_DIM, BF16)

    qa0, ka0, va0 = 0, a_width, 2 * a_width
    ub0 = 3 * a_width
    qc0, kc0, vc0 = ub0 + pool_width, ub0 + pool_width + c_width, ub0 + pool_width + 2 * c_width
    chunks = lambda lo_, hi_: list(range(lo_ // W, hi_ // W))
    z_chunks = tuple([(j, True) for j in chunks(qa0, va0)] + [(j, False) for j in chunks(ub0, qc0)]
                     + [(j, True) for j in chunks(qc0, vc0)])
    v_chunks = tuple(chunks(va0, ub0) + chunks(vc0, w_in.shape[2]))
    zq_a, zk_a = 0, a_width // LANES
    z_ub = 2 * a_width // pool_width
    zq_c = (2 * a_width + pool_width) // LANES
    zk_c = zq_c + c_width // LANES
    scale = HEAD_DIM ** -0.5
    cnt_t, bias_t = _dilated_tables(S)

    mod = _modulation(c, w_mod, b_mod)
    for l in range(L):
        sh1, sc1, ga1, sh2, sc2, ga2 = [m[:, None, :] for m in jnp.split(mod[l], N_MOD, axis=-1)]
        gain = jnp.ones((w_in.shape[2],), F32)
        gain = gain.at[qa0:ka0].set(jnp.tile(gq_a[l], a_width // HEAD_DIM) * scale)
        gain = gain.at[ka0:va0].set(jnp.tile(gk_a[l], a_width // HEAD_DIM))
        gain = gain.at[qc0:kc0].set(jnp.tile(gq_c[l], c_width // HEAD_DIM) * scale)
        gain = gain.at[kc0:vc0].set(jnp.tile(gk_c[l], c_width // HEAD_DIM))
        z, vt = _in_projection(x, sc1, sh1, g_norm1[l][None, :], w_in[l].astype(BF16),
                               gain[None, :], cos_t, sa_t, sb_t, bd, z_chunks, v_chunks)

        lam_init = 0.8 - 0.6 * math.exp(-0.3 * l)
        n_a = a_width // LANES
        ya = _attention("diff", z, vt, zq_a, zk_a, 0, n_a,
                        [lam_a[l], g_sub_a[l][:, None]],
                        [pl.BlockSpec(lam_a[l].shape, lambda b, h: (0, 0)),
                         pl.BlockSpec((LANES, 1), lambda b, h: (0, 0))], lam_init=lam_init)
        n_c = c_width // LANES
        yc = _attention("dil", z, vt, zq_c, zk_c, n_a, n_c,
                        [cnt_t, bias_t],
                        [pl.BlockSpec(cnt_t.shape, lambda b, h: (0, 0, 0)),
                         pl.BlockSpec(bias_t.shape, lambda b, h: (0, 0, 0))])
        w_bd = jax.scipy.linalg.block_diag(*[w_pool[l, g] for g in range(w_pool.shape[1])])
        yb = _pool_mixer(z, z_ub, w_bd.astype(BF16), b_pool[l].reshape(1, -1),
                         pool_scale[l][None, :])

        wr = jnp.zeros((D, LANES), F32)
        wr = wr.at[:, :N_GROUPS].set(w_rg[l])
        wr = wr.at[:, N_GROUPS:N_GROUPS + N_EXPERTS].set(
            w_re[l].transpose(1, 0, 2).reshape(D, N_EXPERTS))
        br = jnp.zeros((1, LANES), F32)
        br = br.at[0, :N_GROUPS].set(b_rg[l])
        br = br.at[0, N_GROUPS:N_GROUPS + N_EXPERTS].set(b_re[l].reshape(-1))
        x1, h3, rt = _out_projection(ya, yb, yc, x, w_out[l].astype(BF16), ga1, sc2, sh2,
                                     g_norm2[l][None, :], wr.astype(BF16), br)
        F = w1.shape[-1]
        x = _moe(h3, rt, x1, ga2, w1.reshape(L * N_EXPERTS, D, F), w3.reshape(L * N_EXPERTS, D, F),
                 w2.reshape(L * N_EXPERTS, F, D), l * N_EXPERTS)
    return x
```

```python
import functools
import math

import jax
import jax.numpy as jnp
import numpy as np
from jax import lax
from jax.experimental import pallas as pl
from jax.experimental.pallas import tpu as pltpu

HEAD_DIM = 64
POOL_WINDOWS = (2, 4, 8, 16)
DILATED_PAIRS = ((128, 1), (512, 4), (2048, 16))
ROPE_THETA = 10000.0
N_GROUPS = 4
EXPERTS_PER_GROUP = 8
N_EXPERTS = N_GROUPS * EXPERTS_PER_GROUP
N_MOD = 6
EPS = 1e-6

LANES = 128
MXU_DIM = 256
KV_BLOCK = 256
Q_BLOCK = 2 * KV_BLOCK
MOE_TILE = 256
RT_E, RT_W = 0, 2
VMEM_LIMIT = 48 * 1024 * 1024
NEG = -1e30
F32 = jnp.float32
BF16 = jnp.bfloat16


def _cparams(sem):
    return pltpu.CompilerParams(dimension_semantics=sem, vmem_limit_bytes=VMEM_LIMIT)


def _dot(a, b):
    return jnp.dot(a, b, preferred_element_type=F32)


def _dot_nt(a, b):
    return lax.dot_general(a, b, (((1,), (1,)), ((), ())), preferred_element_type=F32)


def _mod_kernel(c_ref, w_ref, b_ref, o_ref):
    c = c_ref[...]
    cond = c * (1.0 / (1.0 + jnp.exp(-c)))
    o_ref[0] = _dot(cond.astype(BF16), w_ref[0].astype(BF16)) + b_ref[0]


def _modulation(c, w_mod, b_mod):
    L, D, N = w_mod.shape
    B = c.shape[0]
    tn = 1024
    return pl.pallas_call(
        _mod_kernel,
        grid=(L, N // tn),
        in_specs=[pl.BlockSpec((B, D), lambda l, j: (0, 0)),
                  pl.BlockSpec((1, D, tn), lambda l, j: (l, 0, j)),
                  pl.BlockSpec((1, 1, tn), lambda l, j: (l, 0, j))],
        out_specs=pl.BlockSpec((1, B, tn), lambda l, j: (l, 0, j)),
        out_shape=jax.ShapeDtypeStruct((L, B, N), F32),
        compiler_params=_cparams(("parallel", "parallel")),
        name="modulation",
    )(c, w_mod, b_mod.reshape(L, 1, N))


def _inproj_kernel(z_chunks, v_chunks, x_ref, sc_ref, sh_ref, g_ref, w_ref, gain_ref, cos_ref,
                   sa_ref, sb_ref, bd_ref, z_ref, vt_ref):
    x = x_ref[0]
    ms = jnp.mean(x * x, axis=-1, keepdims=True)
    h = x * lax.rsqrt(ms + EPS) * g_ref[...]
    h = h * (1.0 + sc_ref[0]) + sh_ref[0]
    hb = h.astype(BF16)
    W = MXU_DIM
    for dst, (src, normed) in enumerate(z_chunks):
        zc = _dot(hb, w_ref[:, src * W:(src + 1) * W])
        if normed:
            msq = _dot((zc * zc).astype(BF16), bd_ref[...])
            y = zc * lax.rsqrt(msq + EPS) * gain_ref[:, src * W:(src + 1) * W]
            r_up = pltpu.roll(y, HEAD_DIM // 2, 1)
            r_dn = pltpu.roll(y, W - HEAD_DIM // 2, 1)
            zc = y * cos_ref[...] + r_up * sa_ref[...] + r_dn * sb_ref[...]
        z_ref[0, :, dst * W:(dst + 1) * W] = zc.astype(BF16)
    for dst, src in enumerate(v_chunks):
        zt = _dot(hb, w_ref[:, src * W:(src + 1) * W]).T.astype(BF16)
        for cb in range(vt_ref.shape[1]):
            vt_ref[0, cb, dst * W:(dst + 1) * W, :] = zt[:, cb * KV_BLOCK:(cb + 1) * KV_BLOCK]


def _in_projection(x, sc, sh, g, w_bf, gain, cos_t, sa_t, sb_t, bd, z_chunks, v_chunks, tm=512):
    B, S, D = x.shape
    N = w_bf.shape[1]
    W = MXU_DIM
    nz, nv = len(z_chunks) * W, len(v_chunks) * W
    return pl.pallas_call(
        functools.partial(_inproj_kernel, z_chunks, v_chunks),
        grid=(B, S // tm),
        in_specs=[pl.BlockSpec((1, tm, D), lambda b, i: (b, i, 0)),
                  pl.BlockSpec((1, 1, D), lambda b, i: (b, 0, 0)),
                  pl.BlockSpec((1, 1, D), lambda b, i: (b, 0, 0)),
                  pl.BlockSpec((1, D), lambda b, i: (0, 0)),
                  pl.BlockSpec((D, N), lambda b, i: (0, 0)),
                  pl.BlockSpec((1, N), lambda b, i: (0, 0)),
                  pl.BlockSpec((tm, W), lambda b, i: (i, 0)),
                  pl.BlockSpec((tm, W), lambda b, i: (i, 0)),
                  pl.BlockSpec((tm, W), lambda b, i: (i, 0)),
                  pl.BlockSpec((W, W), lambda b, i: (0, 0))],
        out_specs=[pl.BlockSpec((1, tm, nz), lambda b, i: (b, i, 0)),
                   pl.BlockSpec((1, tm // KV_BLOCK, nv, KV_BLOCK), lambda b, i: (b, i, 0, 0))],
        out_shape=[jax.ShapeDtypeStruct((B, S, nz), BF16),
                   jax.ShapeDtypeStruct((B, S // KV_BLOCK, nv, KV_BLOCK), BF16)],
        compiler_params=_cparams(("parallel", "parallel")),
        name="in_projection",
    )(x, sc, sh, g, w_bf, gain, cos_t, sa_t, sb_t, bd)


def _attn_kernel(mode, lam_init, *refs):
    if mode == "diff":
        q_ref, k_ref, vt_ref, lam_ref, g_ref, o_ref, s_a, s_b, acc_ref = refs
    else:
        q_ref, k_ref, vt_ref, cnt_ref, bias_ref, o_ref, s_a, s_b, acc_ref = refs
    bq, bk = Q_BLOCK, KV_BLOCK
    S = q_ref.shape[1]
    nq = S // bq
    lane = lax.broadcasted_iota(jnp.int32, (1, LANES), 1)
    lo = lane < HEAD_DIM

    def q_block(qi, _):
        qs = pl.multiple_of(qi * bq, bq)
        q = q_ref[0, pl.ds(qs, bq), :]
        zero = jnp.zeros_like(q)
        q2 = jnp.concatenate([jnp.where(lo, q, zero), jnp.where(lo, zero, q)], axis=0)

        def scores(s_ref, kb):
            ks = pl.multiple_of(kb * bk, bk)
            s_ref[...] = _dot_nt(k_ref[0, pl.ds(ks, bk), :], q2)

        def tables(kb):
            if mode == "diff":
                return None, None
            t = jnp.minimum(2 * qi - kb + 1, cnt_ref.shape[0] - 1)
            return bias_ref[t], cnt_ref[t]

        def update(s_ref, kb, stats, bias, cnt):
            vt = vt_ref[0, kb]
            out = []
            for half in range(2):
                cols = slice(half * bq, (half + 1) * bq)
                m, l = stats[2 * half], stats[2 * half + 1]
                s = s_ref[:, cols]
                if bias is not None:
                    s = s + bias
                mn = jnp.maximum(m, jnp.max(s, axis=0, keepdims=True))
                alpha = jnp.exp(m - mn)
                p = jnp.exp(s - mn)
                if cnt is not None:
                    p = p * cnt
                l = alpha * l + jnp.sum(p, axis=0, keepdims=True)
                acc_ref[:, cols] = alpha * acc_ref[:, cols] + _dot(vt, p.astype(BF16))
                out += [mn, l]
            return tuple(out)

        def pair(j, stats):
            scores(s_b, 2 * j + 1)
            stats = update(s_a, 2 * j, stats, *tables(2 * j))
            scores(s_a, 2 * j + 2)
            return update(s_b, 2 * j + 1, stats, *tables(2 * j + 1))

        acc_ref[...] = jnp.zeros_like(acc_ref)
        scores(s_a, 0)
        init = (jnp.full((1, bq), NEG, F32), jnp.zeros((1, bq), F32)) * 2
        stats = lax.fori_loop(0, qi, pair, init)

        krow = lax.broadcasted_iota(jnp.int32, (bk, bq), 0)
        qcol = lax.broadcasted_iota(jnp.int32, (bk, bq), 1)
        scores(s_b, 2 * qi + 1)
        for rel, s_ref in enumerate((s_a, s_b)):
            bias, cnt = tables(2 * qi + rel)
            if mode == "diff":
                bias = jnp.where(krow + rel * bk <= qcol, 0.0, NEG).astype(F32)
            stats = update(s_ref, 2 * qi + rel, stats, bias, cnt)
        _, l_lo, _, l_hi = stats
        a_lo, a_hi = acc_ref[:, :bq], acc_ref[:, bq:]
        if mode == "diff":
            lp = lam_ref[...]
            lam = (jnp.exp(jnp.sum(lp[0:1] * lp[1:2], axis=-1, keepdims=True))
                   - jnp.exp(jnp.sum(lp[2:3] * lp[3:4], axis=-1, keepdims=True)) + lam_init)
            y = a_lo * (1.0 / l_lo) - a_hi * (lam / l_hi)
            ms = jnp.mean(y * y, axis=0, keepdims=True)
            y = y * lax.rsqrt(ms + EPS) * (g_ref[...] * (1.0 - lam_init))
        else:
            feat = lax.broadcasted_iota(jnp.int32, (LANES, 1), 0)
            y = jnp.where(feat < HEAD_DIM, a_lo * (1.0 / l_lo), a_hi * (1.0 / l_hi))
        o_ref[0, pl.ds(qs, bq), :] = y.T.astype(o_ref.dtype)
        return 0

    lax.fori_loop(0, nq, q_block, 0)


def _attention(mode, z, vt, q_col, k_col, v_row, n_blocks, extras, extra_specs, lam_init=0.0):
    B, S, _ = z.shape
    zspec = lambda col: pl.BlockSpec((1, S, LANES), lambda b, h: (b, 0, col + h))
    vspec = pl.BlockSpec((1, S // KV_BLOCK, LANES, KV_BLOCK), lambda b, h: (b, 0, v_row + h, 0))
    return pl.pallas_call(
        functools.partial(_attn_kernel, mode, lam_init),
        grid=(B, n_blocks),
        in_specs=[zspec(q_col), zspec(k_col), vspec] + extra_specs,
        out_specs=pl.BlockSpec((1, S, LANES), lambda b, h: (b, 0, h)),
        out_shape=jax.ShapeDtypeStruct((B, S, n_blocks * LANES), BF16),
        scratch_shapes=[pltpu.VMEM((KV_BLOCK, 2 * Q_BLOCK), F32), pltpu.VMEM((KV_BLOCK, 2 * Q_BLOCK), F32),
                        pltpu.VMEM((LANES, 2 * Q_BLOCK), F32)],
        compiler_params=_cparams(("parallel", "parallel")),
        name=mode + "_attention",
    )(z, z, vt, *extras)


def _dilated_tables(S):
    bq, bk = Q_BLOCK, KV_BLOCK
    dds = np.arange(-1, S // bk)
    dist = (dds[:, None, None] * bk + np.arange(bq)[None, None, :] - np.arange(bk)[None, :, None])
    cnt = np.zeros(dist.shape, np.float32)
    for window, dil in DILATED_PAIRS:
        cnt += (dist >= 0) & (dist <= window) & (dist % dil == 0)
    n = len(dds)
    while n > 1 and np.array_equal(cnt[n - 1], cnt[n - 2]):
        n -= 1
    cnt = cnt[:n]
    bias = np.where(cnt > 0, 0.0, NEG).astype(np.float32)
    return jnp.asarray(cnt), jnp.asarray(bias)


def _pool_kernel(u_ref, w_ref, b_ref, sc_ref, o_ref):
    u = u_ref[0].astype(F32)
    S, C = u.shape
    row = lax.broadcasted_iota(jnp.int32, (S, C), 0)
    lane = lax.broadcasted_iota(jnp.int32, (S, C), 1)
    grp = lane // (C // len(POOL_WINDOWS))

    def shifted(a, k):
        return jnp.where(row >= k, pltpu.roll(a, k, 0), 0.0)

    acc = u
    win = jnp.zeros_like(u)
    width = 1
    for gi, w in enumerate(POOL_WINDOWS):
        while width < w:
            acc = acc + shifted(acc, width)
            width *= 2
        win = jnp.where(grp == gi, acc, win)
    wl = jnp.zeros_like(row)
    for gi, w in enumerate(POOL_WINDOWS):
        wl = jnp.where(grp == gi, w, wl)
    cnt = jnp.minimum(row + 1, wl).astype(F32)
    d = win / cnt - u
    y = _dot(d.astype(BF16), w_ref[...]) + b_ref[...]
    o_ref[0] = (y * sc_ref[...]).astype(o_ref.dtype)


def _pool_mixer(z, col_block, w_bd, b, scale):
    B, S, _ = z.shape
    C = w_bd.shape[0]
    return pl.pallas_call(
        _pool_kernel,
        grid=(B,),
        in_specs=[pl.BlockSpec((1, S, C), lambda b_: (b_, 0, col_block)),
                  pl.BlockSpec((C, C), lambda b_: (0, 0)),
                  pl.BlockSpec((1, C), lambda b_: (0, 0)),
                  pl.BlockSpec((1, C), lambda b_: (0, 0))],
        out_specs=pl.BlockSpec((1, S, C), lambda b_: (b_, 0, 0)),
        out_shape=jax.ShapeDtypeStruct((B, S, C), BF16),
        compiler_params=_cparams(("parallel",)),
        name="pool_mixer",
    )(z, w_bd, b, scale)


def _outproj_kernel(ya_ref, yb_ref, yc_ref, x_ref, wo_ref, ga_ref, sc_ref, sh_ref, g_ref, wr_ref,
                    br_ref, x1_ref, h_ref, rt_ref):
    na = ya_ref.shape[2]
    nb = yb_ref.shape[2]
    y = (_dot(ya_ref[0], wo_ref[0:na, :]) + _dot(yb_ref[0], wo_ref[na:na + nb, :])
         + _dot(yc_ref[0], wo_ref[na + nb:, :]))
    x1 = x_ref[0] + ga_ref[0] * y
    x1_ref[0] = x1
    ms = jnp.mean(x1 * x1, axis=-1, keepdims=True)
    h = x1 * lax.rsqrt(ms + EPS) * g_ref[...]
    h = h * (1.0 + sc_ref[0]) + sh_ref[0]
    hb = h.astype(BF16)
    for s in range(h_ref.shape[0]):
        h_ref[s] = h[:, s * LANES:(s + 1) * LANES]

    logits = _dot(hb, wr_ref[...]) + br_ref[...]
    tm = logits.shape[0]
    G, E = N_GROUPS, EXPERTS_PER_GROUP
    lane = lax.broadcasted_iota(jnp.int32, (tm, LANES), 1)
    lane_f = lane.astype(F32)
    big = float(LANES)
    is_g = lane < G
    gl = jnp.where(is_g, logits, -jnp.inf)
    gmax = jnp.max(gl, axis=-1, keepdims=True)
    gidx = jnp.min(jnp.where(gl == gmax, lane_f, big), axis=-1, keepdims=True)
    gsum = jnp.sum(jnp.where(is_g, jnp.exp(gl - gmax), 0.0), axis=-1, keepdims=True)
    g_w = 1.0 / gsum
    lane_grp = ((lane - G) // E).astype(F32)
    emask = (lane >= G) & (lane < G + G * E) & (lane_grp == gidx)
    el = jnp.where(emask, logits, -jnp.inf)
    v1 = jnp.max(el, axis=-1, keepdims=True)
    i1 = jnp.min(jnp.where(el == v1, lane_f, big), axis=-1, keepdims=True)
    el2 = jnp.where(lane_f == i1, -jnp.inf, el)
    v2 = jnp.max(el2, axis=-1, keepdims=True)
    i2 = jnp.min(jnp.where(el2 == v2, lane_f, big), axis=-1, keepdims=True)
    t = jnp.exp(v2 - v1)
    w1 = g_w / (1.0 + t)
    w2 = w1 * t
    rt_ref[...] = jnp.where(lane == RT_E, i1 - G, jnp.where(lane == RT_E + 1, i2 - G,
                            jnp.where(lane == RT_W, w1, jnp.where(lane == RT_W + 1, w2, 0.0))))


def _out_projection(ya, yb, yc, x, wo_bf, ga, sc, sh, g, wr, br, tm=512):
    B, S, D = x.shape
    nt = S // tm
    tok = lambda n: pl.BlockSpec((1, tm, n), lambda b, i: (b, i, 0))
    per_b = pl.BlockSpec((1, 1, D), lambda b, i: (b, 0, 0))
    const = lambda shape: pl.BlockSpec(shape, lambda b, i: (0, 0))
    return pl.pallas_call(
        _outproj_kernel,
        grid=(B, nt),
        in_specs=[tok(ya.shape[2]), tok(yb.shape[2]), tok(yc.shape[2]), tok(D), const((D, D)),
                  per_b, per_b, per_b, const((1, D)), const((D, LANES)), const((1, LANES))],
        out_specs=[tok(D), pl.BlockSpec((D // LANES, tm, LANES), lambda b, i: (0, b * nt + i, 0)),
                   pl.BlockSpec((tm, LANES), lambda b, i: (b * nt + i, 0))],
        out_shape=[jax.ShapeDtypeStruct((B, S, D), F32),
                   jax.ShapeDtypeStruct((D // LANES, B * S, LANES), F32),
                   jax.ShapeDtypeStruct((B * S, LANES), F32)],
        compiler_params=_cparams(("parallel", "parallel")),
        name="out_projection",
    )(ya, yb, yc, x, wo_bf, ga, sc, sh, g, wr, br)


def _rank_kernel(rt_ref, tri_ref, striu_ref, pos_ref, info_ref, cnt_ref, off_ref, carry_ref):
    ph, i = pl.program_id(0), pl.program_id(1)
    rt = rt_ref[...]
    lane = lax.broadcasted_iota(jnp.int32, rt.shape, 1)
    lane_f = lane.astype(F32)
    hit1 = lane_f == rt[:, RT_E:RT_E + 1]
    hit2 = lane_f == rt[:, RT_E + 1:RT_E + 2]
    onehot = jnp.where(hit1, 1.0, jnp.where(hit2, 1.0, 0.0))
    colsum = jnp.sum(onehot, axis=0, keepdims=True)

    @pl.when((ph == 0) & (i == 0))
    def _():
        cnt_ref[...] = jnp.zeros_like(cnt_ref)

    @pl.when(ph == 0)
    def _():
        cnt_ref[0:1, :] += colsum

    @pl.when((ph == 1) & (i == 0))
    def _():
        ntile = jnp.ceil(cnt_ref[...] * (1.0 / MOE_TILE))
        off_tiles = _dot(ntile.astype(BF16), striu_ref[...])
        off_ref[...] = off_tiles * MOE_TILE
        info_ref[...] = jnp.zeros_like(info_ref)
        info_ref[0:1, :] = cnt_ref[0:1, :]
        info_ref[1:2, :] = off_tiles[0:1, :]
        carry_ref[...] = jnp.zeros_like(carry_ref)

    @pl.when(ph == 1)
    def _():
        before = _dot(tri_ref[...], onehot.astype(BF16)) + carry_ref[0:1, :] + off_ref[0:1, :]
        p1 = jnp.sum(jnp.where(hit1, before, 0.0), axis=1, keepdims=True)
        p2 = jnp.sum(jnp.where(hit2, before, 0.0), axis=1, keepdims=True)
        pos_ref[...] = jnp.where(lane == 0, p1, jnp.where(lane == 1, p2, 0.0)).astype(jnp.int32)
        carry_ref[0:1, :] += colsum


def _rank(rt, tm=512):
    T = rt.shape[0]
    tm = min(tm, T)
    idx = np.arange(tm)
    tri = jnp.asarray(idx[None, :] < idx[:, None], BF16)
    lanes = np.arange(LANES)
    striu = jnp.asarray(lanes[:, None] < lanes[None, :], BF16)
    return pl.pallas_call(
        _rank_kernel,
        grid=(2, T // tm),
        in_specs=[pl.BlockSpec((tm, LANES), lambda ph, i: (i, 0)),
                  pl.BlockSpec((tm, tm), lambda ph, i: (0, 0)),
                  pl.BlockSpec((LANES, LANES), lambda ph, i: (0, 0))],
        out_specs=[pl.BlockSpec((tm, LANES), lambda ph, i: (i * ph, 0)),
                   pl.BlockSpec((8, LANES), lambda ph, i: (0, 0))],
        out_shape=[jax.ShapeDtypeStruct((T, LANES), jnp.int32),
                   jax.ShapeDtypeStruct((8, LANES), F32)],
        scratch_shapes=[pltpu.VMEM((8, LANES), F32)] * 3,
        compiler_params=_cparams(("arbitrary", "arbitrary")),
        name="moe_rank",
    )(rt, tri, striu)


def _experts_kernel(te_ref, nv_ref, rows_ref, src_ref, dst_ref, h_ref, w1_ref, w3_ref, w2_ref, y_ref,
                    xbuf, ybuf, w1b, w3b, w2b, gsem, ssem):
    j = pl.program_id(0)
    nv = nv_ref[0]
    R = MOE_TILE
    nc = xbuf.shape[1]
    unroll = 8

    def gather(tile, slot, r):
        return pltpu.make_async_copy(h_ref.at[:, src_ref[tile * R + r]], xbuf.at[slot, :, r],
                                     gsem.at[slot])

    def scatter(tile, slot, r):
        return pltpu.make_async_copy(ybuf.at[slot, :, r], y_ref.at[:, dst_ref[tile * R + r]],
                                     ssem.at[slot])

    def for_rows(fn, n=None):
        def body(r, _):
            fn(r)
            return 0
        if n is None:
            lax.fori_loop(0, R, body, 0, unroll=unroll)
            return

        def group(g, _):
            for u in range(unroll):
                fn(g * unroll + u)
            return 0
        lax.fori_loop(0, n // unroll, group, 0)
        lax.fori_loop((n // unroll) * unroll, n, body, 0)

    @pl.when(j == 0)
    def _():
        for_rows(lambda r: gather(0, 0, r).start())

    @pl.when(j + 1 < nv)
    def _():
        for_rows(lambda r: gather(j + 1, (j + 1) % 2, r).start())

    e = te_ref[j]
    e_prev = te_ref[jnp.maximum(j - 1, 0)]

    @pl.when((j == 0) | (e != e_prev))
    def _():
        w1b[...] = w1_ref[0].astype(BF16)
        w3b[...] = w3_ref[0].astype(BF16)
        w2b[...] = w2_ref[0].astype(BF16)

    @pl.when(j < nv)
    def _():
        slot = j % 2
        for_rows(lambda r: gather(j, slot, r).wait())

        @pl.when(j >= 2)
        def _():
            for_rows(lambda r: scatter(j - 2, slot, r).wait(), rows_ref[jnp.maximum(j - 2, 0)])

        h = jnp.concatenate([xbuf[slot, s] for s in range(nc)], axis=1).astype(BF16)
        a = _dot(h, w1b[...])
        b = _dot(h, w3b[...])
        hid = (a * (1.0 / (1.0 + jnp.exp(-a))) * b).astype(BF16)
        y = _dot(hid, w2b[...])
        for s in range(nc):
            ybuf[slot, s] = y[:, s * LANES:(s + 1) * LANES]
        for_rows(lambda r: scatter(j, slot, r).start(), rows_ref[j])

    @pl.when(j == nv - 1)
    def _():
        for_rows(lambda r: scatter(j, j % 2, r).wait(), rows_ref[j])

        @pl.when(j >= 1)
        def _():
            for_rows(lambda r: scatter(j - 1, (j - 1) % 2, r).wait(), rows_ref[jnp.maximum(j - 1, 0)])


def _experts(tile_expert, n_valid, tile_rows, src_tok, dst_row, h3, w1, w3, w2, n_out_rows):
    NC = h3.shape[0]
    NE, D, F = w1.shape
    nt = tile_expert.shape[0]
    wspec = lambda shape: pl.BlockSpec(shape, lambda j, te, nv, rows, src, dst: (te[j], 0, 0))
    return pl.pallas_call(
        _experts_kernel,
        grid_spec=pltpu.PrefetchScalarGridSpec(
            num_scalar_prefetch=5,
            grid=(nt,),
            in_specs=[pl.BlockSpec(memory_space=pl.ANY),
                      wspec((1, D, F)), wspec((1, D, F)), wspec((1, F, D))],
            out_specs=pl.BlockSpec(memory_space=pl.ANY),
            scratch_shapes=[pltpu.VMEM((2, NC, MOE_TILE, LANES), F32),
                            pltpu.VMEM((2, NC, MOE_TILE, LANES), F32),
                            pltpu.VMEM((D, F), BF16), pltpu.VMEM((D, F), BF16),
                            pltpu.VMEM((F, D), BF16),
                            pltpu.SemaphoreType.DMA((2,)), pltpu.SemaphoreType.DMA((2,))]),
        out_shape=jax.ShapeDtypeStruct((NC, n_out_rows, LANES), F32),
        compiler_params=_cparams(("arbitrary",)),
        name="moe_experts",
    )(tile_expert, n_valid, tile_rows, src_tok, dst_row, h3, w1, w3, w2)


def _combine_kernel(x_ref, rt_ref, ga_ref, y0_ref, y1_ref, o_ref):
    rt = rt_ref[...]
    w1 = rt[:, RT_W:RT_W + 1]
    w2 = rt[:, RT_W + 1:RT_W + 2]
    for s in range(y0_ref.shape[0]):
        cols = slice(s * LANES, (s + 1) * LANES)
        o_ref[:, cols] = x_ref[:, cols] + ga_ref[0][:, cols] * (w1 * y0_ref[s] + w2 * y1_ref[s])


def _combine(x1, rt, ga, y, tm=512):
    B, S, D = x1.shape
    NC = y.shape[0]
    tm = min(tm, S)
    nt = S // tm
    n_tok_tiles = B * nt
    return pl.pallas_call(
        _combine_kernel,
        grid=(n_tok_tiles,),
        in_specs=[pl.BlockSpec((tm, D), lambda i: (i, 0)),
                  pl.BlockSpec((tm, LANES), lambda i: (i, 0)),
                  pl.BlockSpec((1, 1, D), lambda i: (i // nt, 0, 0)),
                  pl.BlockSpec((NC, tm, LANES), lambda i: (0, i, 0)),
                  pl.BlockSpec((NC, tm, LANES), lambda i: (0, n_tok_tiles + i, 0))],
        out_specs=pl.BlockSpec((tm, D), lambda i: (i, 0)),
        out_shape=jax.ShapeDtypeStruct((B * S, D), F32),
        compiler_params=_cparams(("parallel",)),
        name="moe_combine",
    )(x1.reshape(B * S, D), rt, ga, y, y).reshape(B, S, D)


def _moe(h3, rt, x1, ga, w1, w3, w2, first_expert):
    T = rt.shape[0]
    R = MOE_TILE
    n_tiles = 2 * T // R + N_EXPERTS
    pos, info = _rank(rt)
    pos_flat = pos[:, :2].reshape(-1)
    counts = info[0, :N_EXPERTS]
    start_tiles = info[1, :N_EXPERTS]
    end_tiles = start_tiles + jnp.ceil(counts * (1.0 / R))
    tile_ids = jnp.arange(n_tiles, dtype=F32)
    tile_expert = jnp.minimum(jnp.sum(tile_ids[:, None] >= end_tiles[None, :], axis=1), N_EXPERTS - 1)
    end_rows = (start_tiles * R + counts)[tile_expert]
    tile_rows = jnp.clip(end_rows - tile_ids * R, 0, R).astype(jnp.int32)
    n_valid = end_tiles[N_EXPERTS - 1:].astype(jnp.int32)
    inv = jnp.full((n_tiles * R,), -1, jnp.int32).at[pos_flat].set(jnp.arange(2 * T, dtype=jnp.int32))
    src_tok = jnp.maximum(inv, 0) >> 1
    dst_row = jnp.maximum(inv, 0) % 2 * T + src_tok
    y = _experts(tile_expert.astype(jnp.int32) + first_expert, n_valid, tile_rows, src_tok, dst_row,
                 h3, w1, w3, w2, 2 * T)
    return _combine(x1, rt, ga, y)


def kernel(x, c, w_mod, b_mod, g_norm1, w_in, gq_a, gk_a, lam_a, g_sub_a, w_pool, b_pool, pool_scale, gq_c, gk_c, w_out, g_norm2, w_rg, b_rg, w_re, b_re, w1, w3, w2):
    B, S, D = x.shape
    L = w_mod.shape[0]
    a_width = D // 2
    pool_width = D // 4
    c_width = D // 4
    W = MXU_DIM
    reps = W // HEAD_DIM

    inv = 1.0 / (ROPE_THETA ** (jnp.arange(0, HEAD_DIM, 2, dtype=F32) / HEAD_DIM))
    ang = jnp.arange(S, dtype=F32)[:, None] * inv[None, :]
    ang = jnp.concatenate([ang, ang], axis=-1)
    cos_h, sin_h = jnp.cos(ang), jnp.sin(ang)
    first = jnp.arange(HEAD_DIM) < HEAD_DIM // 2
    cos_t = jnp.tile(cos_h, (1, reps))
    sa_t = jnp.tile(jnp.where(first[None, :], 0.0, sin_h), (1, reps))
    sb_t = jnp.tile(jnp.where(first[None, :], -sin_h, 0.0), (1, reps))
    head_of = np.arange(W) // HEAD_DIM
    bd = jnp.asarray((head_of[:, None] == head_of[None, :]) / HEAD_DIM, BF16)

    qa0, ka0, va0 = 0, a_width, 2 * a_width
    ub0 = 3 * a_width
    qc0, kc0, vc0 = ub0 + pool_width, ub0 + pool_width + c_width, ub0 + pool_width + 2 * c_width
    chunks = lambda lo_, hi_: list(range(lo_ // W, hi_ // W))
    z_chunks = tuple([(j, True) for j in chunks(qa0, va0)] + [(j, False) for j in chunks(ub0, qc0)]
                     + [(j, True) for j in chunks(qc0, vc0)])
    v_chunks = tuple(chunks(va0, ub0) + chunks(vc0, w_in.shape[2]))
    zq_a, zk_a = 0, a_width // LANES
    z_ub = 2 * a_width // pool_width
    zq_c = (2 * a_width + pool_width) // LANES
    zk_c = zq_c + c_width // LANES
    scale = HEAD_DIM ** -0.5
    cnt_t, bias_t = _dilated_tables(S)

    mod = _modulation(c, w_mod, b_mod)
    for l in range(L):
        sh1, sc1, ga1, sh2, sc2, ga2 = [m[:, None, :] for m in jnp.split(mod[l], N_MOD, axis=-1)]
        gain = jnp.ones((w_in.shape[2],), F32)
        gain = gain.at[qa0:ka0].set(jnp.tile(gq_a[l], a_width // HEAD_DIM) * scale)
        gain = gain.at[ka0:va0].set(jnp.tile(gk_a[l], a_width // HEAD_DIM))
        gain = gain.at[qc0:kc0].set(jnp.tile(gq_c[l], c_width // HEAD_DIM) * scale)
        gain = gain.at[kc0:vc0].set(jnp.tile(gk_c[l], c_width // HEAD_DIM))
        z, vt = _in_projection(x, sc1, sh1, g_norm1[l][None, :], w_in[l].astype(BF16),
                               gain[None, :], cos_t, sa_t, sb_t, bd, z_chunks, v_chunks)

        lam_init = 0.8 - 0.6 * math.exp(-0.3 * l)
        n_a = a_width // LANES
        ya = _attention("diff", z, vt, zq_a, zk_a, 0, n_a,
                        [lam_a[l], g_sub_a[l][:, None]],
                        [pl.BlockSpec(lam_a[l].shape, lambda b, h: (0, 0)),
                         pl.BlockSpec((LANES, 1), lambda b, h: (0, 0))], lam_init=lam_init)
        n_c = c_width // LANES
        yc = _attention("dil", z, vt, zq_c, zk_c, n_a, n_c,
                        [cnt_t, bias_t],
                        [pl.BlockSpec(cnt_t.shape, lambda b, h: (0, 0, 0)),
                         pl.BlockSpec(bias_t.shape, lambda b, h: (0, 0, 0))])
        w_bd = jax.scipy.linalg.block_diag(*[w_pool[l, g] for g in range(w_pool.shape[1])])
        yb = _pool_mixer(z, z_ub, w_bd.astype(BF16), b_pool[l].reshape(1, -1),
                         pool_scale[l][None, :])

        wr = jnp.zeros((D, LANES), F32)
        wr = wr.at[:, :N_GROUPS].set(w_rg[l])
        wr = wr.at[:, N_GROUPS:N_GROUPS + N_EXPERTS].set(
            w_re[l].transpose(1, 0, 2).reshape(D, N_EXPERTS))
        br = jnp.zeros((1, LANES), F32)
        br = br.at[0, :N_GROUPS].set(b_rg[l])
        br = br.at[0, N_GROUPS:N_GROUPS + N_EXPERTS].set(b_re[l].reshape(-1))
        x1, h3, rt = _out_projection(ya, yb, yc, x, w_out[l].astype(BF16), ga1, sc2, sh2,
                                     g_norm2[l][None, :], wr.astype(BF16), br)
        F = w1.shape[-1]
        x = _moe(h3, rt, x1, ga2, w1.reshape(L * N_EXPERTS, D, F), w3.reshape(L * N_EXPERTS, D, F),
                 w2.reshape(L * N_EXPERTS, F, D), l * N_EXPERTS)
    return x
```

```python
import functools
import math

import jax
import jax.numpy as jnp
import numpy as np
from jax import lax
from jax.experimental import pallas as pl
from jax.experimental.pallas import tpu as pltpu

HEAD_DIM = 64
POOL_WINDOWS = (2, 4, 8, 16)
DILATED_PAIRS = ((128, 1), (512, 4), (2048, 16))
ROPE_THETA = 10000.0
N_GROUPS = 4
EXPERTS_PER_GROUP = 8
N_EXPERTS = N_GROUPS * EXPERTS_PER_GROUP
N_MOD = 6
EPS = 1e-6

LANES = 128
MXU_DIM = 256
KV_BLOCK = 256
Q_BLOCK = 2 * KV_BLOCK
MOE_TILE = 256
RT_E, RT_W = 0, 2
VMEM_LIMIT = 48 * 1024 * 1024
NEG = -1e30
F32 = jnp.float32
BF16 = jnp.bfloat16


def _cparams(sem):
    return pltpu.CompilerParams(dimension_semantics=sem, vmem_limit_bytes=VMEM_LIMIT)


def _dot(a, b):
    return jnp.dot(a, b, preferred_element_type=F32)


def _dot_nt(a, b):
    return lax.dot_general(a, b, (((1,), (1,)), ((), ())), preferred_element_type=F32)


def _mod_kernel(c_ref, w_ref, b_ref, o_ref):
    c = c_ref[...]
    cond = c * (1.0 / (1.0 + jnp.exp(-c)))
    o_ref[0] = _dot(cond.astype(BF16), w_ref[0].astype(BF16)) + b_ref[0]


def _modulation(c, w_mod, b_mod):
    L, D, N = w_mod.shape
    B = c.shape[0]
    tn = 1024
    return pl.pallas_call(
        _mod_kernel,
        grid=(L, N // tn),
        in_specs=[pl.BlockSpec((B, D), lambda l, j: (0, 0)),
                  pl.BlockSpec((1, D, tn), lambda l, j: (l, 0, j)),
                  pl.BlockSpec((1, 1, tn), lambda l, j: (l, 0, j))],
        out_specs=pl.BlockSpec((1, B, tn), lambda l, j: (l, 0, j)),
        out_shape=jax.ShapeDtypeStruct((L, B, N), F32),
        compiler_params=_cparams(("parallel", "parallel")),
        name="modulation",
    )(c, w_mod, b_mod.reshape(L, 1, N))


def _inproj_kernel(z_chunks, v_chunks, x_ref, sc_ref, sh_ref, g_ref, w_ref, gain_ref, cos_ref,
                   sa_ref, sb_ref, bd_ref, z_ref, vt_ref):
    x = x_ref[0]
    ms = jnp.mean(x * x, axis=-1, keepdims=True)
    h = x * lax.rsqrt(ms + EPS) * g_ref[...]
    h = h * (1.0 + sc_ref[0]) + sh_ref[0]
    hb = h.astype(BF16)
    W = MXU_DIM
    for dst, (src, normed) in enumerate(z_chunks):
        zc = _dot(hb, w_ref[:, src * W:(src + 1) * W])
        if normed:
            msq = _dot((zc * zc).astype(BF16), bd_ref[...])
            y = zc * lax.rsqrt(msq + EPS) * gain_ref[:, src * W:(src + 1) * W]
            r_up = pltpu.roll(y, HEAD_DIM // 2, 1)
            r_dn = pltpu.roll(y, W - HEAD_DIM // 2, 1)
            zc = y * cos_ref[...] + r_up * sa_ref[...] + r_dn * sb_ref[...]
        z_ref[0, :, dst * W:(dst + 1) * W] = zc.astype(BF16)
    for dst, src in enumerate(v_chunks):
        zt = _dot(hb, w_ref[:, src * W:(src + 1) * W]).T.astype(BF16)
        for cb in range(vt_ref.shape[1]):
            vt_ref[0, cb, dst * W:(dst + 1) * W, :] = zt[:, cb * KV_BLOCK:(cb + 1) * KV_BLOCK]


def _in_projection(x, sc, sh, g, w_bf, gain, cos_t, sa_t, sb_t, bd, z_chunks, v_chunks, tm=512):
    B, S, D = x.shape
    N = w_bf.shape[1]
    W = MXU_DIM
    nz, nv = len(z_chunks) * W, len(v_chunks) * W
    return pl.pallas_call(
        functools.partial(_inproj_kernel, z_chunks, v_chunks),
        grid=(B, S // tm),
        in_specs=[pl.BlockSpec((1, tm, D), lambda b, i: (b, i, 0)),
                  pl.BlockSpec((1, 1, D), lambda b, i: (b, 0, 0)),
                  pl.BlockSpec((1, 1, D), lambda b, i: (b, 0, 0)),
                  pl.BlockSpec((1, D), lambda b, i: (0, 0)),
                  pl.BlockSpec((D, N), lambda b, i: (0, 0)),
                  pl.BlockSpec((1, N), lambda b, i: (0, 0)),
                  pl.BlockSpec((tm, W), lambda b, i: (i, 0)),
                  pl.BlockSpec((tm, W), lambda b, i: (i, 0)),
                  pl.BlockSpec((tm, W), lambda b, i: (i, 0)),
                  pl.BlockSpec((W, W), lambda b, i: (0, 0))],
        out_specs=[pl.BlockSpec((1, tm, nz), lambda b, i: (b, i, 0)),
                   pl.BlockSpec((1, tm // KV_BLOCK, nv, KV_BLOCK), lambda b, i: (b, i, 0, 0))],
        out_shape=[jax.ShapeDtypeStruct((B, S, nz), BF16),
                   jax.ShapeDtypeStruct((B, S // KV_BLOCK, nv, KV_BLOCK), BF16)],
        compiler_params=_cparams(("parallel", "parallel")),
        name="in_projection",
    )(x, sc, sh, g, w_bf, gain, cos_t, sa_t, sb_t, bd)


def _attn_kernel(mode, lam_init, *refs):
    if mode == "diff":
        q_ref, k_ref, vt_ref, lam_ref, g_ref, o_ref, s_a, s_b, acc_ref = refs
    else:
        q_ref, k_ref, vt_ref, cnt_ref, bias_ref, o_ref, s_a, s_b, acc_ref = refs
    bq, bk = Q_BLOCK, KV_BLOCK
    S = q_ref.shape[1]
    nq = S // bq
    lane = lax.broadcasted_iota(jnp.int32, (1, LANES), 1)
    lo = lane < HEAD_DIM

    def q_block(qi, _):
        qs = pl.multiple_of(qi * bq, bq)
        q = q_ref[0, pl.ds(qs, bq), :]
        zero = jnp.zeros_like(q)
        q2 = jnp.concatenate([jnp.where(lo, q, zero), jnp.where(lo, zero, q)], axis=0)

        def scores(s_ref, kb):
            ks = pl.multiple_of(kb * bk, bk)
            s_ref[...] = _dot_nt(k_ref[0, pl.ds(ks, bk), :], q2)

        def tables(kb):
            if mode == "diff":
                return None, None
            t = jnp.minimum(2 * qi - kb + 1, cnt_ref.shape[0] - 1)
            return bias_ref[t], cnt_ref[t]

        def update(s_ref, kb, stats, bias, cnt):
            vt = vt_ref[0, kb]
            out = []
            for half in range(2):
                cols = slice(half * bq, (half + 1) * bq)
                m, l = stats[2 * half], stats[2 * half + 1]
                s = s_ref[:, cols]
                if bias is not None:
                    s = s + bias
                mn = jnp.maximum(m, jnp.max(s, axis=0, keepdims=True))
                alpha = jnp.exp2(m - mn)
                p = jnp.exp2(s - mn)
                if cnt is not None:
                    p = p * cnt
                l = alpha * l + jnp.sum(p, axis=0, keepdims=True)
                acc_ref[:, cols] = alpha * acc_ref[:, cols] + _dot(vt, p.astype(BF16))
                out += [mn, l]
            return tuple(out)

        def pair(j, stats):
            scores(s_b, 2 * j + 1)
            stats = update(s_a, 2 * j, stats, *tables(2 * j))
            scores(s_a, 2 * j + 2)
            return update(s_b, 2 * j + 1, stats, *tables(2 * j + 1))

        acc_ref[...] = jnp.zeros_like(acc_ref)
        scores(s_a, 0)
        init = (jnp.full((1, bq), NEG, F32), jnp.zeros((1, bq), F32)) * 2
        stats = lax.fori_loop(0, qi, pair, init)

        krow = lax.broadcasted_iota(jnp.int32, (bk, bq), 0)
        qcol = lax.broadcasted_iota(jnp.int32, (bk, bq), 1)
        scores(s_b, 2 * qi + 1)
        for rel, s_ref in enumerate((s_a, s_b)):
            bias, cnt = tables(2 * qi + rel)
            if mode == "diff":
                bias = jnp.where(krow + rel * bk <= qcol, 0.0, NEG).astype(F32)
            stats = update(s_ref, 2 * qi + rel, stats, bias, cnt)
        _, l_lo, _, l_hi = stats
        a_lo, a_hi = acc_ref[:, :bq], acc_ref[:, bq:]
        if mode == "diff":
            lp = lam_ref[...]
            lam = (jnp.exp(jnp.sum(lp[0:1] * lp[1:2], axis=-1, keepdims=True))
                   - jnp.exp(jnp.sum(lp[2:3] * lp[3:4], axis=-1, keepdims=True)) + lam_init)
            y = a_lo * (1.0 / l_lo) - a_hi * (lam / l_hi)
            ms = jnp.mean(y * y, axis=0, keepdims=True)
            y = y * lax.rsqrt(ms + EPS) * (g_ref[...] * (1.0 - lam_init))
        else:
            feat = lax.broadcasted_iota(jnp.int32, (LANES, 1), 0)
            y = jnp.where(feat < HEAD_DIM, a_lo * (1.0 / l_lo), a_hi * (1.0 / l_hi))
        o_ref[0, pl.ds(qs, bq), :] = y.T.astype(o_ref.dtype)
        return 0

    lax.fori_loop(0, nq, q_block, 0)


def _attention(mode, z, vt, q_col, k_col, v_row, n_blocks, extras, extra_specs, lam_init=0.0):
    B, S, _ = z.shape
    zspec = lambda col: pl.BlockSpec((1, S, LANES), lambda b, h: (b, 0, col + h))
    vspec = pl.BlockSpec((1, S // KV_BLOCK, LANES, KV_BLOCK), lambda b, h: (b, 0, v_row + h, 0))
    return pl.pallas_call(
        functools.partial(_attn_kernel, mode, lam_init),
        grid=(B, n_blocks),
        in_specs=[zspec(q_col), zspec(k_col), vspec] + extra_specs,
        out_specs=pl.BlockSpec((1, S, LANES), lambda b, h: (b, 0, h)),
        out_shape=jax.ShapeDtypeStruct((B, S, n_blocks * LANES), BF16),
        scratch_shapes=[pltpu.VMEM((KV_BLOCK, 2 * Q_BLOCK), F32), pltpu.VMEM((KV_BLOCK, 2 * Q_BLOCK), F32),
                        pltpu.VMEM((LANES, 2 * Q_BLOCK), F32)],
        compiler_params=_cparams(("parallel", "parallel")),
        name=mode + "_attention",
    )(z, z, vt, *extras)


def _dilated_tables(S):
    bq, bk = Q_BLOCK, KV_BLOCK
    dds = np.arange(-1, S // bk)
    dist = (dds[:, None, None] * bk + np.arange(bq)[None, None, :] - np.arange(bk)[None, :, None])
    cnt = np.zeros(dist.shape, np.float32)
    for window, dil in DILATED_PAIRS:
        cnt += (dist >= 0) & (dist <= window) & (dist % dil == 0)
    n = len(dds)
    while n > 1 and np.array_equal(cnt[n - 1], cnt[n - 2]):
        n -= 1
    cnt = cnt[:n]
    bias = np.where(cnt > 0, 0.0, NEG).astype(np.float32)
    return jnp.asarray(cnt), jnp.asarray(bias)


def _pool_kernel(u_ref, w_ref, b_ref, sc_ref, o_ref):
    u = u_ref[0].astype(F32)
    S, C = u.shape
    row = lax.broadcasted_iota(jnp.int32, (S, C), 0)
    lane = lax.broadcasted_iota(jnp.int32, (S, C), 1)
    grp = lane // (C // len(POOL_WINDOWS))

    def shifted(a, k):
        return jnp.where(row >= k, pltpu.roll(a, k, 0), 0.0)

    acc = u
    win = jnp.zeros_like(u)
    width = 1
    for gi, w in enumerate(POOL_WINDOWS):
        while width < w:
            acc = acc + shifted(acc, width)
            width *= 2
        win = jnp.where(grp == gi, acc, win)
    wl = jnp.zeros_like(row)
    for gi, w in enumerate(POOL_WINDOWS):
        wl = jnp.where(grp == gi, w, wl)
    cnt = jnp.minimum(row + 1, wl).astype(F32)
    d = win / cnt - u
    y = _dot(d.astype(BF16), w_ref[...]) + b_ref[...]
    o_ref[0] = (y * sc_ref[...]).astype(o_ref.dtype)


def _pool_mixer(z, col_block, w_bd, b, scale):
    B, S, _ = z.shape
    C = w_bd.shape[0]
    return pl.pallas_call(
        _pool_kernel,
        grid=(B,),
        in_specs=[pl.BlockSpec((1, S, C), lambda b_: (b_, 0, col_block)),
                  pl.BlockSpec((C, C), lambda b_: (0, 0)),
                  pl.BlockSpec((1, C), lambda b_: (0, 0)),
                  pl.BlockSpec((1, C), lambda b_: (0, 0))],
        out_specs=pl.BlockSpec((1, S, C), lambda b_: (b_, 0, 0)),
        out_shape=jax.ShapeDtypeStruct((B, S, C), BF16),
        compiler_params=_cparams(("parallel",)),
        name="pool_mixer",
    )(z, w_bd, b, scale)


def _outproj_kernel(ya_ref, yb_ref, yc_ref, x_ref, wo_ref, ga_ref, sc_ref, sh_ref, g_ref, wr_ref,
                    br_ref, x1_ref, h_ref, rt_ref):
    na = ya_ref.shape[2]
    nb = yb_ref.shape[2]
    y = (_dot(ya_ref[0], wo_ref[0:na, :]) + _dot(yb_ref[0], wo_ref[na:na + nb, :])
         + _dot(yc_ref[0], wo_ref[na + nb:, :]))
    x1 = x_ref[0] + ga_ref[0] * y
    x1_ref[0] = x1
    ms = jnp.mean(x1 * x1, axis=-1, keepdims=True)
    h = x1 * lax.rsqrt(ms + EPS) * g_ref[...]
    h = h * (1.0 + sc_ref[0]) + sh_ref[0]
    hb = h.astype(BF16)
    for s in range(h_ref.shape[0]):
        h_ref[s] = h[:, s * LANES:(s + 1) * LANES]

    logits = _dot(hb, wr_ref[...]) + br_ref[...]
    tm = logits.shape[0]
    G, E = N_GROUPS, EXPERTS_PER_GROUP
    lane = lax.broadcasted_iota(jnp.int32, (tm, LANES), 1)
    lane_f = lane.astype(F32)
    big = float(LANES)
    is_g = lane < G
    gl = jnp.where(is_g, logits, -jnp.inf)
    gmax = jnp.max(gl, axis=-1, keepdims=True)
    gidx = jnp.min(jnp.where(gl == gmax, lane_f, big), axis=-1, keepdims=True)
    gsum = jnp.sum(jnp.where(is_g, jnp.exp(gl - gmax), 0.0), axis=-1, keepdims=True)
    g_w = 1.0 / gsum
    lane_grp = ((lane - G) // E).astype(F32)
    emask = (lane >= G) & (lane < G + G * E) & (lane_grp == gidx)
    el = jnp.where(emask, logits, -jnp.inf)
    v1 = jnp.max(el, axis=-1, keepdims=True)
    i1 = jnp.min(jnp.where(el == v1, lane_f, big), axis=-1, keepdims=True)
    el2 = jnp.where(lane_f == i1, -jnp.inf, el)
    v2 = jnp.max(el2, axis=-1, keepdims=True)
    i2 = jnp.min(jnp.where(el2 == v2, lane_f, big), axis=-1, keepdims=True)
    t = jnp.exp(v2 - v1)
    w1 = g_w / (1.0 + t)
    w2 = w1 * t
    rt_ref[...] = jnp.where(lane == RT_E, i1 - G, jnp.where(lane == RT_E + 1, i2 - G,
                            jnp.where(lane == RT_W, w1, jnp.where(lane == RT_W + 1, w2, 0.0))))


def _out_projection(ya, yb, yc, x, wo_bf, ga, sc, sh, g, wr, br, tm=512):
    B, S, D = x.shape
    nt = S // tm
    tok = lambda n: pl.BlockSpec((1, tm, n), lambda b, i: (b, i, 0))
    per_b = pl.BlockSpec((1, 1, D), lambda b, i: (b, 0, 0))
    const = lambda shape: pl.BlockSpec(shape, lambda b, i: (0, 0))
    return pl.pallas_call(
        _outproj_kernel,
        grid=(B, nt),
        in_specs=[tok(ya.shape[2]), tok(yb.shape[2]), tok(yc.shape[2]), tok(D), const((D, D)),
                  per_b, per_b, per_b, const((1, D)), const((D, LANES)), const((1, LANES))],
        out_specs=[tok(D), pl.BlockSpec((D // LANES, tm, LANES), lambda b, i: (0, b * nt + i, 0)),
                   pl.BlockSpec((tm, LANES), lambda b, i: (b * nt + i, 0))],
        out_shape=[jax.ShapeDtypeStruct((B, S, D), F32),
                   jax.ShapeDtypeStruct((D // LANES, B * S, LANES), F32),
                   jax.ShapeDtypeStruct((B * S, LANES), F32)],
        compiler_params=_cparams(("parallel", "parallel")),
        name="out_projection",
    )(ya, yb, yc, x, wo_bf, ga, sc, sh, g, wr, br)


def _rank_kernel(rt_ref, tri_ref, striu_ref, pos_ref, info_ref, cnt_ref, off_ref, carry_ref):
    ph, i = pl.program_id(0), pl.program_id(1)
    rt = rt_ref[...]
    lane = lax.broadcasted_iota(jnp.int32, rt.shape, 1)
    lane_f = lane.astype(F32)
    hit1 = lane_f == rt[:, RT_E:RT_E + 1]
    hit2 = lane_f == rt[:, RT_E + 1:RT_E + 2]
    onehot = jnp.where(hit1, 1.0, jnp.where(hit2, 1.0, 0.0))
    colsum = jnp.sum(onehot, axis=0, keepdims=True)

    @pl.when((ph == 0) & (i == 0))
    def _():
        cnt_ref[...] = jnp.zeros_like(cnt_ref)

    @pl.when(ph == 0)
    def _():
        cnt_ref[0:1, :] += colsum

    @pl.when((ph == 1) & (i == 0))
    def _():
        ntile = jnp.ceil(cnt_ref[...] * (1.0 / MOE_TILE))
        off_tiles = _dot(ntile.astype(BF16), striu_ref[...])
        off_ref[...] = off_tiles * MOE_TILE
        info_ref[...] = jnp.zeros_like(info_ref)
        info_ref[0:1, :] = cnt_ref[0:1, :]
        info_ref[1:2, :] = off_tiles[0:1, :]
        carry_ref[...] = jnp.zeros_like(carry_ref)

    @pl.when(ph == 1)
    def _():
        before = _dot(tri_ref[...], onehot.astype(BF16)) + carry_ref[0:1, :] + off_ref[0:1, :]
        p1 = jnp.sum(jnp.where(hit1, before, 0.0), axis=1, keepdims=True)
        p2 = jnp.sum(jnp.where(hit2, before, 0.0), axis=1, keepdims=True)
        pos_ref[...] = jnp.where(lane == 0, p1, jnp.where(lane == 1, p2, 0.0)).astype(jnp.int32)
        carry_ref[0:1, :] += colsum


def _rank(rt, tm=512):
    T = rt.shape[0]
    tm = min(tm, T)
    idx = np.arange(tm)
    tri = jnp.asarray(idx[None, :] < idx[:, None], BF16)
    lanes = np.arange(LANES)
    striu = jnp.asarray(lanes[:, None] < lanes[None, :], BF16)
    return pl.pallas_call(
        _rank_kernel,
        grid=(2, T // tm),
        in_specs=[pl.BlockSpec((tm, LANES), lambda ph, i: (i, 0)),
                  pl.BlockSpec((tm, tm), lambda ph, i: (0, 0)),
                  pl.BlockSpec((LANES, LANES), lambda ph, i: (0, 0))],
        out_specs=[pl.BlockSpec((tm, LANES), lambda ph, i: (i * ph, 0)),
                   pl.BlockSpec((8, LANES), lambda ph, i: (0, 0))],
        out_shape=[jax.ShapeDtypeStruct((T, LANES), jnp.int32),
                   jax.ShapeDtypeStruct((8, LANES), F32)],
        scratch_shapes=[pltpu.VMEM((8, LANES), F32)] * 3,
        compiler_params=_cparams(("arbitrary", "arbitrary")),
        name="moe_rank",
    )(rt, tri, striu)


def _dispatch_kernel(pos_ref, h_ref, xs_init_ref, xs_ref, sem):
    del xs_init_ref
    tm = h_ref.shape[1]
    base = pl.program_id(0) * (2 * tm)

    def row_copy(t, k):
        return pltpu.make_async_copy(h_ref.at[:, t], xs_ref.at[:, pos_ref[base + 2 * t + k]], sem)

    def issue(t, _):
        row_copy(t, 0).start(priority=0)
        row_copy(t, 1).start(priority=1)
        return 0

    def drain(t, _):
        row_copy(t, 0).wait()
        row_copy(t, 1).wait()
        return 0

    lax.fori_loop(0, tm, issue, 0, unroll=8)
    lax.fori_loop(0, tm, drain, 0, unroll=8)


def _dispatch(pos_flat, h3, n_rows, tm=256):
    NC, T, _ = h3.shape
    tm = min(tm, T)
    return pl.pallas_call(
        _dispatch_kernel,
        grid_spec=pltpu.PrefetchScalarGridSpec(
            num_scalar_prefetch=1,
            grid=(T // tm,),
            in_specs=[pl.BlockSpec((NC, tm, LANES), lambda i, pos: (0, i, 0)),
                      pl.BlockSpec(memory_space=pl.ANY)],
            out_specs=pl.BlockSpec(memory_space=pl.ANY),
            scratch_shapes=[pltpu.SemaphoreType.DMA]),
        out_shape=jax.ShapeDtypeStruct((NC, n_rows, LANES), F32),
        input_output_aliases={2: 0},
        compiler_params=_cparams(("arbitrary",)),
        name="moe_dispatch",
    )(pos_flat, h3, jnp.zeros((NC, n_rows, LANES), F32))


def _experts_kernel(te_ref, nv_ref, xs_ref, w1_ref, w3_ref, w2_ref, ys_ref, w1b, w3b, w2b):
    j = pl.program_id(0)
    e = te_ref[j]
    e_prev = te_ref[jnp.maximum(j - 1, 0)]

    @pl.when((j == 0) | (e != e_prev))
    def _():
        w1b[...] = w1_ref[0].astype(BF16)
        w3b[...] = w3_ref[0].astype(BF16)
        w2b[...] = w2_ref[0].astype(BF16)

    @pl.when(j < nv_ref[0])
    def _():
        nc = xs_ref.shape[0]
        h = jnp.concatenate([xs_ref[s] for s in range(nc)], axis=1).astype(BF16)
        a = _dot(h, w1b[...])
        b = _dot(h, w3b[...])
        hid = (a * (1.0 / (1.0 + jnp.exp(-a))) * b).astype(BF16)
        y = _dot(hid, w2b[...])
        for s in range(nc):
            ys_ref[s] = y[:, s * LANES:(s + 1) * LANES]

    @pl.when(j >= nv_ref[0])
    def _():
        ys_ref[...] = jnp.zeros_like(ys_ref)


def _experts(tile_expert, n_valid, xs, w1, w3, w2):
    NC, P, _ = xs.shape
    NE, D, F = w1.shape
    nt = P // MOE_TILE
    tile = lambda j, te, nv: (0, jnp.minimum(j, nv[0] - 1), 0)
    wspec = lambda shape: pl.BlockSpec(shape, lambda j, te, nv: (te[j], 0, 0))
    return pl.pallas_call(
        _experts_kernel,
        grid_spec=pltpu.PrefetchScalarGridSpec(
            num_scalar_prefetch=2,
            grid=(nt,),
            in_specs=[pl.BlockSpec((NC, MOE_TILE, LANES), tile),
                      wspec((1, D, F)), wspec((1, D, F)), wspec((1, F, D))],
            out_specs=pl.BlockSpec((NC, MOE_TILE, LANES), lambda j, te, nv: (0, j, 0)),
            scratch_shapes=[pltpu.VMEM((D, F), BF16), pltpu.VMEM((D, F), BF16),
                            pltpu.VMEM((F, D), BF16)]),
        out_shape=jax.ShapeDtypeStruct((NC, P, LANES), F32),
        compiler_params=_cparams(("arbitrary",)),
        name="moe_experts",
    )(tile_expert, n_valid, xs, w1, w3, w2)


def _combine_kernel(pos_ref, x_ref, rt_ref, ga_ref, ys_ref, o_ref, buf, sems):
    i = pl.program_id(0)
    n = pl.num_programs(0)
    tm = x_ref.shape[0]
    nc = buf.shape[1]

    def row_copy(tile, slot, t, k):
        src = ys_ref.at[:, pos_ref[tile * (2 * tm) + 2 * t + k]]
        return pltpu.make_async_copy(src, buf.at[slot, :, k * tm + t], sems.at[slot])

    def issue(tile, slot):
        def body(t, _):
            row_copy(tile, slot, t, 0).start(priority=0)
            row_copy(tile, slot, t, 1).start(priority=1)
            return 0
        lax.fori_loop(0, tm, body, 0, unroll=8)

    @pl.when(i == 0)
    def _():
        issue(0, 0)

    @pl.when(i + 1 < n)
    def _():
        issue(i + 1, (i + 1) % 2)

    slot = i % 2

    def drain(t, _):
        row_copy(i, slot, t, 0).wait()
        row_copy(i, slot, t, 1).wait()
        return 0

    lax.fori_loop(0, tm, drain, 0, unroll=8)
    rt = rt_ref[...]
    w1 = rt[:, RT_W:RT_W + 1]
    w2 = rt[:, RT_W + 1:RT_W + 2]
    for s in range(nc):
        cols = slice(s * LANES, (s + 1) * LANES)
        y = w1 * buf[slot, s, 0:tm, :] + w2 * buf[slot, s, tm:2 * tm, :]
        o_ref[:, cols] = x_ref[:, cols] + ga_ref[0][:, cols] * y


def _combine(pos_flat, x1, rt, ga, ys, tm=256):
    B, S, D = x1.shape
    NC = ys.shape[0]
    tm = min(tm, S)
    nt = S // tm
    return pl.pallas_call(
        _combine_kernel,
        grid_spec=pltpu.PrefetchScalarGridSpec(
            num_scalar_prefetch=1,
            grid=(B * nt,),
            in_specs=[pl.BlockSpec((tm, D), lambda i, pos: (i, 0)),
                      pl.BlockSpec((tm, LANES), lambda i, pos: (i, 0)),
                      pl.BlockSpec((1, 1, D), lambda i, pos: (i // nt, 0, 0)),
                      pl.BlockSpec(memory_space=pl.ANY)],
            out_specs=pl.BlockSpec((tm, D), lambda i, pos: (i, 0)),
            scratch_shapes=[pltpu.VMEM((2, NC, 2 * tm, LANES), F32),
                            pltpu.SemaphoreType.DMA((2,))]),
        out_shape=jax.ShapeDtypeStruct((B * S, D), F32),
        compiler_params=_cparams(("arbitrary",)),
        name="moe_combine",
    )(pos_flat, x1.reshape(B * S, D), rt, ga, ys).reshape(B, S, D)


def _moe(h3, rt, x1, ga, w1, w3, w2, first_expert):
    T = rt.shape[0]
    n_tiles = 2 * T // MOE_TILE + N_EXPERTS
    pos, info = _rank(rt)
    pos_flat = pos[:, :2].reshape(-1)
    counts = info[0, :N_EXPERTS]
    ends = info[1, :N_EXPERTS] + jnp.ceil(counts * (1.0 / MOE_TILE))
    tile_ids = jnp.arange(n_tiles, dtype=F32)
    tile_expert = jnp.minimum(jnp.sum(tile_ids[:, None] >= ends[None, :], axis=1), N_EXPERTS - 1)
    n_valid = ends[N_EXPERTS - 1:].astype(jnp.int32)
    xs = _dispatch(pos_flat, h3, n_tiles * MOE_TILE)
    ys = _experts(tile_expert.astype(jnp.int32) + first_expert, n_valid, xs, w1, w3, w2)
    return _combine(pos_flat, x1, rt, ga, ys)


def kernel(x, c, w_mod, b_mod, g_norm1, w_in, gq_a, gk_a, lam_a, g_sub_a, w_pool, b_pool, pool_scale, gq_c, gk_c, w_out, g_norm2, w_rg, b_rg, w_re, b_re, w1, w3, w2):
    B, S, D = x.shape
    L = w_mod.shape[0]
    a_width = D // 2
    pool_width = D // 4
    c_width = D // 4
    W = MXU_DIM
    reps = W // HEAD_DIM

    inv = 1.0 / (ROPE_THETA ** (jnp.arange(0, HEAD_DIM, 2, dtype=F32) / HEAD_DIM))
    ang = jnp.arange(S, dtype=F32)[:, None] * inv[None, :]
    ang = jnp.concatenate([ang, ang], axis=-1)
    cos_h, sin_h = jnp.cos(ang), jnp.sin(ang)
    first = jnp.arange(HEAD_DIM) < HEAD_DIM // 2
    cos_t = jnp.tile(cos_h, (1, reps))
    sa_t = jnp.tile(jnp.where(first[None, :], 0.0, sin_h), (1, reps))
    sb_t = jnp.tile(jnp.where(first[None, :], -sin_h, 0.0), (1, reps))
    head_of = np.arange(W) // HEAD_DIM
    bd = jnp.asarray((head_of[:, None] == head_of[None, :]) / HEAD_DIM, BF16)

    qa0, ka0, va0 = 0, a_width, 2 * a_width
    ub0 = 3 * a_width
    qc0, kc0, vc0 = ub0 + pool_width, ub0 + pool_width + c_width, ub0 + pool_width + 2 * c_width
    chunks = lambda lo_, hi_: list(range(lo_ // W, hi_ // W))
    z_chunks = tuple([(j, True) for j in chunks(qa0, va0)] + [(j, False) for j in chunks(ub0, qc0)]
                     + [(j, True) for j in chunks(qc0, vc0)])
    v_chunks = tuple(chunks(va0, ub0) + chunks(vc0, w_in.shape[2]))
    zq_a, zk_a = 0, a_width // LANES
    z_ub = 2 * a_width // pool_width
    zq_c = (2 * a_width + pool_width) // LANES
    zk_c = zq_c + c_width // LANES
    scale = HEAD_DIM ** -0.5 * math.log2(math.e)
    cnt_t, bias_t = _dilated_tables(S)

    mod = _modulation(c, w_mod, b_mod)
    for l in range(L):
        sh1, sc1, ga1, sh2, sc2, ga2 = [m[:, None, :] for m in jnp.split(mod[l], N_MOD, axis=-1)]
        gain = jnp.ones((w_in.shape[2],), F32)
        gain = gain.at[qa0:ka0].set(jnp.tile(gq_a[l], a_width // HEAD_DIM) * scale)
        gain = gain.at[ka0:va0].set(jnp.tile(gk_a[l], a_width // HEAD_DIM))
        gain = gain.at[qc0:kc0].set(jnp.tile(gq_c[l], c_width // HEAD_DIM) * scale)
        gain = gain.at[kc0:vc0].set(jnp.tile(gk_c[l], c_width // HEAD_DIM))
        z, vt = _in_projection(x, sc1, sh1, g_norm1[l][None, :], w_in[l].astype(BF16),
                               gain[None, :], cos_t, sa_t, sb_t, bd, z_chunks, v_chunks)

        lam_init = 0.8 - 0.6 * math.exp(-0.3 * l)
        n_a = a_width // LANES
        ya = _attention("diff", z, vt, zq_a, zk_a, 0, n_a,
                        [lam_a[l], g_sub_a[l][:, None]],
                        [pl.BlockSpec(lam_a[l].shape, lambda b, h: (0, 0)),
                         pl.BlockSpec((LANES, 1), lambda b, h: (0, 0))], lam_init=lam_init)
        n_c = c_width // LANES
        yc = _attention("dil", z, vt, zq_c, zk_c, n_a, n_c,
                        [cnt_t, bias_t],
                        [pl.BlockSpec(cnt_t.shape, lambda b, h: (0, 0, 0)),
                         pl.BlockSpec(bias_t.shape, lambda b, h: (0, 0, 0))])
        w_bd = jax.scipy.linalg.block_diag(*[w_pool[l, g] for g in range(w_pool.shape[1])])
        yb = _pool_mixer(z, z_ub, w_bd.astype(BF16), b_pool[l].reshape(1, -1),
                         pool_scale[l][None, :])

        wr = jnp.zeros((D, LANES), F32)
        wr = wr.at[:, :N_GROUPS].set(w_rg[l])
        wr = wr.at[:, N_GROUPS:N_GROUPS + N_EXPERTS].set(
            w_re[l].transpose(1, 0, 2).reshape(D, N_EXPERTS))
        br = jnp.zeros((1, LANES), F32)
        br = br.at[0, :N_GROUPS].set(b_rg[l])
        br = br.at[0, N_GROUPS:N_GROUPS + N_EXPERTS].set(b_re[l].reshape(-1))
        x1, h3, rt = _out_projection(ya, yb, yc, x, w_out[l].astype(BF16), ga1, sc2, sh2,
                                     g_norm2[l][None, :], wr.astype(BF16), br)
        F = w1.shape[-1]
        x = _moe(h3, rt, x1, ga2, w1.reshape(L * N_EXPERTS, D, F), w3.reshape(L * N_EXPERTS, D, F),
                 w2.reshape(L * N_EXPERTS, F, D), l * N_EXPERTS)
    return x
```

```python
import functools
import math

import jax
import jax.numpy as jnp
import numpy as np
from jax import lax
from jax.experimental import pallas as pl
from jax.experimental.pallas import tpu as pltpu

HEAD_DIM = 64
POOL_WINDOWS = (2, 4, 8, 16)
DILATED_PAIRS = ((128, 1), (512, 4), (2048, 16))
ROPE_THETA = 10000.0
N_GROUPS = 4
EXPERTS_PER_GROUP = 8
N_EXPERTS = N_GROUPS * EXPERTS_PER_GROUP
N_MOD = 6
EPS = 1e-6

LANES = 128
BF16_ROWS = 16
MXU_DIM = 256
KV_BLOCK = 256
Q_BLOCK = 2 * KV_BLOCK
MOE_TILE = 256
RT_E, RT_W = 0, 2
VMEM_LIMIT = 48 * 1024 * 1024
NEG = -1e30
F32 = jnp.float32
BF16 = jnp.bfloat16


def _cparams(sem):
    return pltpu.CompilerParams(dimension_semantics=sem, vmem_limit_bytes=VMEM_LIMIT)


def _dot(a, b):
    return jnp.dot(a, b, preferred_element_type=F32)


def _dot_nt(a, b):
    return lax.dot_general(a, b, (((1,), (1,)), ((), ())), preferred_element_type=F32)


def _mod_kernel(c_ref, w_ref, b_ref, o_ref):
    c = c_ref[...]
    cond = c * (1.0 / (1.0 + jnp.exp(-c)))
    o_ref[0] = _dot(cond.astype(BF16), w_ref[0].astype(BF16)) + b_ref[0]


def _modulation(c, w_mod, b_mod):
    L, D, N = w_mod.shape
    B = c.shape[0]
    tn = 1024
    return pl.pallas_call(
        _mod_kernel,
        grid=(L, N // tn),
        in_specs=[pl.BlockSpec((B, D), lambda l, j: (0, 0)),
                  pl.BlockSpec((1, D, tn), lambda l, j: (l, 0, j)),
                  pl.BlockSpec((1, 1, tn), lambda l, j: (l, 0, j))],
        out_specs=pl.BlockSpec((1, B, tn), lambda l, j: (l, 0, j)),
        out_shape=jax.ShapeDtypeStruct((L, B, N), F32),
        compiler_params=_cparams(("parallel", "parallel")),
        name="modulation",
    )(c, w_mod, b_mod.reshape(L, 1, N))


def _inproj_kernel(z_chunks, v_chunks, x_ref, sc_ref, sh_ref, g_ref, w_ref, gain_ref, cos_ref,
                   sa_ref, sb_ref, bd_ref, z_ref, vt_ref):
    x = x_ref[0]
    ms = jnp.mean(x * x, axis=-1, keepdims=True)
    h = x * lax.rsqrt(ms + EPS) * g_ref[...]
    h = h * (1.0 + sc_ref[0]) + sh_ref[0]
    hb = h.astype(BF16)
    W = MXU_DIM
    for dst, (src, normed) in enumerate(z_chunks):
        zc = _dot(hb, w_ref[:, src * W:(src + 1) * W])
        if normed:
            msq = _dot((zc * zc).astype(BF16), bd_ref[...])
            y = zc * lax.rsqrt(msq + EPS) * gain_ref[:, src * W:(src + 1) * W]
            r_up = pltpu.roll(y, HEAD_DIM // 2, 1)
            r_dn = pltpu.roll(y, W - HEAD_DIM // 2, 1)
            zc = y * cos_ref[...] + r_up * sa_ref[...] + r_dn * sb_ref[...]
        z_ref[0, :, dst * W:(dst + 1) * W] = zc.astype(BF16)
    for dst, src in enumerate(v_chunks):
        zt = _dot(hb, w_ref[:, src * W:(src + 1) * W]).T.astype(BF16)
        for cb in range(vt_ref.shape[1]):
            vt_ref[0, cb, dst * W:(dst + 1) * W, :] = zt[:, cb * KV_BLOCK:(cb + 1) * KV_BLOCK]


def _in_projection(x, sc, sh, g, w_bf, gain, cos_t, sa_t, sb_t, bd, z_chunks, v_chunks, tm=512):
    B, S, D = x.shape
    N = w_bf.shape[1]
    W = MXU_DIM
    nz, nv = len(z_chunks) * W, len(v_chunks) * W
    return pl.pallas_call(
        functools.partial(_inproj_kernel, z_chunks, v_chunks),
        grid=(B, S // tm),
        in_specs=[pl.BlockSpec((1, tm, D), lambda b, i: (b, i, 0)),
                  pl.BlockSpec((1, 1, D), lambda b, i: (b, 0, 0)),
                  pl.BlockSpec((1, 1, D), lambda b, i: (b, 0, 0)),
                  pl.BlockSpec((1, D), lambda b, i: (0, 0)),
                  pl.BlockSpec((D, N), lambda b, i: (0, 0)),
                  pl.BlockSpec((1, N), lambda b, i: (0, 0)),
                  pl.BlockSpec((tm, W), lambda b, i: (i, 0)),
                  pl.BlockSpec((tm, W), lambda b, i: (i, 0)),
                  pl.BlockSpec((tm, W), lambda b, i: (i, 0)),
                  pl.BlockSpec((W, W), lambda b, i: (0, 0))],
        out_specs=[pl.BlockSpec((1, tm, nz), lambda b, i: (b, i, 0)),
                   pl.BlockSpec((1, tm // KV_BLOCK, nv, KV_BLOCK), lambda b, i: (b, i, 0, 0))],
        out_shape=[jax.ShapeDtypeStruct((B, S, nz), BF16),
                   jax.ShapeDtypeStruct((B, S // KV_BLOCK, nv, KV_BLOCK), BF16)],
        compiler_params=_cparams(("parallel", "parallel")),
        name="in_projection",
    )(x, sc, sh, g, w_bf, gain, cos_t, sa_t, sb_t, bd)


def _attn_kernel(mode, lam_init, *refs):
    if mode == "diff":
        q_ref, k_ref, vt_ref, lam_ref, g_ref, o_ref, s_a, s_b, acc_ref = refs
    else:
        q_ref, k_ref, vt_ref, bias_ref, o_ref, s_a, s_b, acc_ref = refs
    bq, bk = Q_BLOCK, KV_BLOCK
    S = q_ref.shape[1]
    nq = S // bq
    lane = lax.broadcasted_iota(jnp.int32, (1, LANES), 1)
    lo = lane < HEAD_DIM
    extra = acc_ref.shape[0] - LANES
    ones_rows = jnp.where(lax.broadcasted_iota(jnp.int32, (extra, bk), 0) == 0, 1.0, 0.0).astype(BF16)

    def q_block(qi, _):
        qs = pl.multiple_of(qi * bq, bq)
        q = q_ref[0, pl.ds(qs, bq), :]
        zero = jnp.zeros_like(q)
        q2 = jnp.concatenate([jnp.where(lo, q, zero), jnp.where(lo, zero, q)], axis=0)

        def scores(s_ref, kb):
            ks = pl.multiple_of(kb * bk, bk)
            s_ref[...] = _dot_nt(k_ref[0, pl.ds(ks, bk), :], q2)

        def table(kb):
            if mode == "diff":
                return None
            return bias_ref[jnp.minimum(2 * qi - kb + 1, bias_ref.shape[0] - 1)]

        def update(s_ref, kb, stats, bias):
            vt1 = jnp.concatenate([vt_ref[0, kb], ones_rows], axis=0)
            out = []
            for half in range(2):
                cols = slice(half * bq, (half + 1) * bq)
                m = stats[half]
                s = s_ref[:, cols]
                if bias is not None:
                    s = s + bias
                mn = jnp.maximum(m, jnp.max(s, axis=0, keepdims=True))
                p = jnp.exp2(s - mn)
                acc_ref[:, cols] = jnp.exp2(m - mn) * acc_ref[:, cols] + _dot(vt1, p.astype(BF16))
                out.append(mn)
            return tuple(out)

        def pair(j, stats):
            scores(s_b, 2 * j + 1)
            stats = update(s_a, 2 * j, stats, table(2 * j))
            scores(s_a, 2 * j + 2)
            return update(s_b, 2 * j + 1, stats, table(2 * j + 1))

        acc_ref[...] = jnp.zeros_like(acc_ref)
        scores(s_a, 0)
        stats = lax.fori_loop(0, qi, pair, (jnp.full((1, bq), NEG, F32),) * 2)

        krow = lax.broadcasted_iota(jnp.int32, (bk, bq), 0)
        qcol = lax.broadcasted_iota(jnp.int32, (bk, bq), 1)
        scores(s_b, 2 * qi + 1)
        for rel, s_ref in enumerate((s_a, s_b)):
            bias = table(2 * qi + rel)
            if mode == "diff":
                bias = jnp.where(krow + rel * bk <= qcol, 0.0, NEG).astype(F32)
            stats = update(s_ref, 2 * qi + rel, stats, bias)
        a_lo, a_hi = acc_ref[0:LANES, :bq], acc_ref[0:LANES, bq:]
        l_lo, l_hi = acc_ref[LANES:LANES + 1, :bq], acc_ref[LANES:LANES + 1, bq:]
        if mode == "diff":
            lp = lam_ref[...]
            lam = (jnp.exp(jnp.sum(lp[0:1] * lp[1:2], axis=-1, keepdims=True))
                   - jnp.exp(jnp.sum(lp[2:3] * lp[3:4], axis=-1, keepdims=True)) + lam_init)
            y = a_lo * (1.0 / l_lo) - a_hi * (lam / l_hi)
            ms = jnp.mean(y * y, axis=0, keepdims=True)
            y = y * lax.rsqrt(ms + EPS) * (g_ref[...] * (1.0 - lam_init))
        else:
            feat = lax.broadcasted_iota(jnp.int32, (LANES, 1), 0)
            y = jnp.where(feat < HEAD_DIM, a_lo * (1.0 / l_lo), a_hi * (1.0 / l_hi))
        o_ref[0, pl.ds(qs, bq), :] = y.T.astype(o_ref.dtype)
        return 0

    lax.fori_loop(0, nq, q_block, 0)


def _attention(mode, z, vt, q_col, k_col, v_row, n_blocks, extras, extra_specs, lam_init=0.0):
    B, S, _ = z.shape
    zspec = lambda col: pl.BlockSpec((1, S, LANES), lambda b, h: (b, 0, col + h))
    vspec = pl.BlockSpec((1, S // KV_BLOCK, LANES, KV_BLOCK), lambda b, h: (b, 0, v_row + h, 0))
    return pl.pallas_call(
        functools.partial(_attn_kernel, mode, lam_init),
        grid=(B, n_blocks),
        in_specs=[zspec(q_col), zspec(k_col), vspec] + extra_specs,
        out_specs=pl.BlockSpec((1, S, LANES), lambda b, h: (b, 0, h)),
        out_shape=jax.ShapeDtypeStruct((B, S, n_blocks * LANES), BF16),
        scratch_shapes=[pltpu.VMEM((KV_BLOCK, 2 * Q_BLOCK), F32), pltpu.VMEM((KV_BLOCK, 2 * Q_BLOCK), F32),
                        pltpu.VMEM((LANES + BF16_ROWS, 2 * Q_BLOCK), F32)],
        compiler_params=_cparams(("parallel", "parallel")),
        name=mode + "_attention",
    )(z, z, vt, *extras)


def _dilated_tables(S):
    bq, bk = Q_BLOCK, KV_BLOCK
    dds = np.arange(-1, S // bk)
    dist = (dds[:, None, None] * bk + np.arange(bq)[None, None, :] - np.arange(bk)[None, :, None])
    cnt = np.zeros(dist.shape, np.float32)
    for window, dil in DILATED_PAIRS:
        cnt += (dist >= 0) & (dist <= window) & (dist % dil == 0)
    n = len(dds)
    while n > 1 and np.array_equal(cnt[n - 1], cnt[n - 2]):
        n -= 1
    cnt = cnt[:n]
    return jnp.asarray(np.where(cnt > 0, np.log2(np.maximum(cnt, 1.0)), NEG).astype(np.float32))


def _pool_kernel(u_ref, w_ref, b_ref, sc_ref, o_ref):
    u = u_ref[0].astype(F32)
    S, C = u.shape
    row = lax.broadcasted_iota(jnp.int32, (S, C), 0)
    lane = lax.broadcasted_iota(jnp.int32, (S, C), 1)
    grp = lane // (C // len(POOL_WINDOWS))

    def shifted(a, k):
        return jnp.where(row >= k, pltpu.roll(a, k, 0), 0.0)

    acc = u
    win = jnp.zeros_like(u)
    width = 1
    for gi, w in enumerate(POOL_WINDOWS):
        while width < w:
            acc = acc + shifted(acc, width)
            width *= 2
        win = jnp.where(grp == gi, acc, win)
    wl = jnp.zeros_like(row)
    for gi, w in enumerate(POOL_WINDOWS):
        wl = jnp.where(grp == gi, w, wl)
    cnt = jnp.minimum(row + 1, wl).astype(F32)
    d = win / cnt - u
    y = _dot(d.astype(BF16), w_ref[...]) + b_ref[...]
    o_ref[0] = (y * sc_ref[...]).astype(o_ref.dtype)


def _pool_mixer(z, col_block, w_bd, b, scale):
    B, S, _ = z.shape
    C = w_bd.shape[0]
    return pl.pallas_call(
        _pool_kernel,
        grid=(B,),
        in_specs=[pl.BlockSpec((1, S, C), lambda b_: (b_, 0, col_block)),
                  pl.BlockSpec((C, C), lambda b_: (0, 0)),
                  pl.BlockSpec((1, C), lambda b_: (0, 0)),
                  pl.BlockSpec((1, C), lambda b_: (0, 0))],
        out_specs=pl.BlockSpec((1, S, C), lambda b_: (b_, 0, 0)),
        out_shape=jax.ShapeDtypeStruct((B, S, C), BF16),
        compiler_params=_cparams(("parallel",)),
        name="pool_mixer",
    )(z, w_bd, b, scale)


def _outproj_kernel(ya_ref, yb_ref, yc_ref, x_ref, wo_ref, ga_ref, sc_ref, sh_ref, g_ref, wr_ref,
                    br_ref, x1_ref, h_ref, rt_ref):
    na = ya_ref.shape[2]
    nb = yb_ref.shape[2]
    y = (_dot(ya_ref[0], wo_ref[0:na, :]) + _dot(yb_ref[0], wo_ref[na:na + nb, :])
         + _dot(yc_ref[0], wo_ref[na + nb:, :]))
    x1 = x_ref[0] + ga_ref[0] * y
    x1_ref[0] = x1
    ms = jnp.mean(x1 * x1, axis=-1, keepdims=True)
    h = x1 * lax.rsqrt(ms + EPS) * g_ref[...]
    h = h * (1.0 + sc_ref[0]) + sh_ref[0]
    hb = h.astype(BF16)
    for s in range(h_ref.shape[0]):
        h_ref[s] = h[:, s * LANES:(s + 1) * LANES]

    logits = _dot(hb, wr_ref[...]) + br_ref[...]
    tm = logits.shape[0]
    G, E = N_GROUPS, EXPERTS_PER_GROUP
    lane = lax.broadcasted_iota(jnp.int32, (tm, LANES), 1)
    lane_f = lane.astype(F32)
    big = float(LANES)
    is_g = lane < G
    gl = jnp.where(is_g, logits, -jnp.inf)
    gmax = jnp.max(gl, axis=-1, keepdims=True)
    gidx = jnp.min(jnp.where(gl == gmax, lane_f, big), axis=-1, keepdims=True)
    gsum = jnp.sum(jnp.where(is_g, jnp.exp(gl - gmax), 0.0), axis=-1, keepdims=True)
    g_w = 1.0 / gsum
    lane_grp = ((lane - G) // E).astype(F32)
    emask = (lane >= G) & (lane < G + G * E) & (lane_grp == gidx)
    el = jnp.where(emask, logits, -jnp.inf)
    v1 = jnp.max(el, axis=-1, keepdims=True)
    i1 = jnp.min(jnp.where(el == v1, lane_f, big), axis=-1, keepdims=True)
    el2 = jnp.where(lane_f == i1, -jnp.inf, el)
    v2 = jnp.max(el2, axis=-1, keepdims=True)
    i2 = jnp.min(jnp.where(el2 == v2, lane_f, big), axis=-1, keepdims=True)
    t = jnp.exp(v2 - v1)
    w1 = g_w / (1.0 + t)
    w2 = w1 * t
    rt_ref[...] = jnp.where(lane == RT_E, i1 - G, jnp.where(lane == RT_E + 1, i2 - G,
                            jnp.where(lane == RT_W, w1, jnp.where(lane == RT_W + 1, w2, 0.0))))


def _out_projection(ya, yb, yc, x, wo_bf, ga, sc, sh, g, wr, br, tm=512):
    B, S, D = x.shape
    nt = S // tm
    tok = lambda n: pl.BlockSpec((1, tm, n), lambda b, i: (b, i, 0))
    per_b = pl.BlockSpec((1, 1, D), lambda b, i: (b, 0, 0))
    const = lambda shape: pl.BlockSpec(shape, lambda b, i: (0, 0))
    return pl.pallas_call(
        _outproj_kernel,
        grid=(B, nt),
        in_specs=[tok(ya.shape[2]), tok(yb.shape[2]), tok(yc.shape[2]), tok(D), const((D, D)),
                  per_b, per_b, per_b, const((1, D)), const((D, LANES)), const((1, LANES))],
        out_specs=[tok(D), pl.BlockSpec((D // LANES, tm, LANES), lambda b, i: (0, b * nt + i, 0)),
                   pl.BlockSpec((tm, LANES), lambda b, i: (b * nt + i, 0))],
        out_shape=[jax.ShapeDtypeStruct((B, S, D), F32),
                   jax.ShapeDtypeStruct((D // LANES, B * S, LANES), F32),
                   jax.ShapeDtypeStruct((B * S, LANES), F32)],
        compiler_params=_cparams(("parallel", "parallel")),
        name="out_projection",
    )(ya, yb, yc, x, wo_bf, ga, sc, sh, g, wr, br)


def _rank_kernel(rt_ref, tri_ref, striu_ref, pos_ref, info_ref, cnt_ref, off_ref, carry_ref):
    ph, i = pl.program_id(0), pl.program_id(1)
    rt = rt_ref[...]
    lane = lax.broadcasted_iota(jnp.int32, rt.shape, 1)
    lane_f = lane.astype(F32)
    hit1 = lane_f == rt[:, RT_E:RT_E + 1]
    hit2 = lane_f == rt[:, RT_E + 1:RT_E + 2]
    onehot = jnp.where(hit1, 1.0, jnp.where(hit2, 1.0, 0.0))
    colsum = jnp.sum(onehot, axis=0, keepdims=True)

    @pl.when((ph == 0) & (i == 0))
    def _():
        cnt_ref[...] = jnp.zeros_like(cnt_ref)

    @pl.when(ph == 0)
    def _():
        cnt_ref[0:1, :] += colsum

    @pl.when((ph == 1) & (i == 0))
    def _():
        ntile = jnp.ceil(cnt_ref[...] * (1.0 / MOE_TILE))
        off_tiles = _dot(ntile.astype(BF16), striu_ref[...])
        off_ref[...] = off_tiles * MOE_TILE
        info_ref[...] = jnp.zeros_like(info_ref)
        info_ref[0:1, :] = cnt_ref[0:1, :]
        info_ref[1:2, :] = off_tiles[0:1, :]
        carry_ref[...] = jnp.zeros_like(carry_ref)

    @pl.when(ph == 1)
    def _():
        before = _dot(tri_ref[...], onehot.astype(BF16)) + carry_ref[0:1, :] + off_ref[0:1, :]
        p1 = jnp.sum(jnp.where(hit1, before, 0.0), axis=1, keepdims=True)
        p2 = jnp.sum(jnp.where(hit2, before, 0.0), axis=1, keepdims=True)
        pos_ref[...] = jnp.where(lane == 0, p1, jnp.where(lane == 1, p2, 0.0)).astype(jnp.int32)
        carry_ref[0:1, :] += colsum


def _rank(rt, tm=512):
    T = rt.shape[0]
    tm = min(tm, T)
    idx = np.arange(tm)
    tri = jnp.asarray(idx[None, :] < idx[:, None], BF16)
    lanes = np.arange(LANES)
    striu = jnp.asarray(lanes[:, None] < lanes[None, :], BF16)
    return pl.pallas_call(
        _rank_kernel,
        grid=(2, T // tm),
        in_specs=[pl.BlockSpec((tm, LANES), lambda ph, i: (i, 0)),
                  pl.BlockSpec((tm, tm), lambda ph, i: (0, 0)),
                  pl.BlockSpec((LANES, LANES), lambda ph, i: (0, 0))],
        out_specs=[pl.BlockSpec((tm, LANES), lambda ph, i: (i * ph, 0)),
                   pl.BlockSpec((8, LANES), lambda ph, i: (0, 0))],
        out_shape=[jax.ShapeDtypeStruct((T, LANES), jnp.int32),
                   jax.ShapeDtypeStruct((8, LANES), F32)],
        scratch_shapes=[pltpu.VMEM((8, LANES), F32)] * 3,
        compiler_params=_cparams(("arbitrary", "arbitrary")),
        name="moe_rank",
    )(rt, tri, striu)


def _dispatch_kernel(pos_ref, h_ref, xs_init_ref, xs_ref, sem):
    del xs_init_ref
    tm = h_ref.shape[1]
    base = pl.program_id(0) * (2 * tm)

    def row_copy(t, k):
        return pltpu.make_async_copy(h_ref.at[:, t], xs_ref.at[:, pos_ref[base + 2 * t + k]], sem)

    def issue(t, _):
        row_copy(t, 0).start(priority=0)
        row_copy(t, 1).start(priority=1)
        return 0

    def drain(t, _):
        row_copy(t, 0).wait()
        row_copy(t, 1).wait()
        return 0

    lax.fori_loop(0, tm, issue, 0, unroll=8)
    lax.fori_loop(0, tm, drain, 0, unroll=8)


def _dispatch(pos_flat, h3, n_rows, tm=256):
    NC, T, _ = h3.shape
    tm = min(tm, T)
    return pl.pallas_call(
        _dispatch_kernel,
        grid_spec=pltpu.PrefetchScalarGridSpec(
            num_scalar_prefetch=1,
            grid=(T // tm,),
            in_specs=[pl.BlockSpec((NC, tm, LANES), lambda i, pos: (0, i, 0)),
                      pl.BlockSpec(memory_space=pl.ANY)],
            out_specs=pl.BlockSpec(memory_space=pl.ANY),
            scratch_shapes=[pltpu.SemaphoreType.DMA]),
        out_shape=jax.ShapeDtypeStruct((NC, n_rows, LANES), F32),
        input_output_aliases={2: 0},
        compiler_params=_cparams(("arbitrary",)),
        name="moe_dispatch",
    )(pos_flat, h3, jnp.zeros((NC, n_rows, LANES), F32))


def _experts_kernel(te_ref, nv_ref, xs_ref, w1_ref, w3_ref, w2_ref, ys_ref, w1b, w3b, w2b):
    j = pl.program_id(0)
    e = te_ref[j]
    e_prev = te_ref[jnp.maximum(j - 1, 0)]

    @pl.when((j == 0) | (e != e_prev))
    def _():
        w1b[...] = w1_ref[0].astype(BF16)
        w3b[...] = w3_ref[0].astype(BF16)
        w2b[...] = w2_ref[0].astype(BF16)

    @pl.when(j < nv_ref[0])
    def _():
        nc = xs_ref.shape[0]
        h = jnp.concatenate([xs_ref[s] for s in range(nc)], axis=1).astype(BF16)
        a = _dot(h, w1b[...])
        b = _dot(h, w3b[...])
        hid = (a * (1.0 / (1.0 + jnp.exp(-a))) * b).astype(BF16)
        y = _dot(hid, w2b[...])
        for s in range(nc):
            ys_ref[s] = y[:, s * LANES:(s + 1) * LANES]

    @pl.when(j >= nv_ref[0])
    def _():
        ys_ref[...] = jnp.zeros_like(ys_ref)


def _experts(tile_expert, n_valid, xs, w1, w3, w2):
    NC, P, _ = xs.shape
    NE, D, F = w1.shape
    nt = P // MOE_TILE
    tile = lambda j, te, nv: (0, jnp.minimum(j, nv[0] - 1), 0)
    wspec = lambda shape: pl.BlockSpec(shape, lambda j, te, nv: (te[j], 0, 0))
    return pl.pallas_call(
        _experts_kernel,
        grid_spec=pltpu.PrefetchScalarGridSpec(
            num_scalar_prefetch=2,
            grid=(nt,),
            in_specs=[pl.BlockSpec((NC, MOE_TILE, LANES), tile),
                      wspec((1, D, F)), wspec((1, D, F)), wspec((1, F, D))],
            out_specs=pl.BlockSpec((NC, MOE_TILE, LANES), lambda j, te, nv: (0, j, 0)),
            scratch_shapes=[pltpu.VMEM((D, F), BF16), pltpu.VMEM((D, F), BF16),
                            pltpu.VMEM((F, D), BF16)]),
        out_shape=jax.ShapeDtypeStruct((NC, P, LANES), F32),
        compiler_params=_cparams(("arbitrary",)),
        name="moe_experts",
    )(tile_expert, n_valid, xs, w1, w3, w2)


def _combine_kernel(pos_ref, x_ref, rt_ref, ga_ref, ys_ref, o_ref, buf, sems):
    i = pl.program_id(0)
    n = pl.num_programs(0)
    tm = x_ref.shape[0]
    nc = buf.shape[1]

    def row_copy(tile, slot, t, k):
        src = ys_ref.at[:, pos_ref[tile * (2 * tm) + 2 * t + k]]
        return pltpu.make_async_copy(src, buf.at[slot, :, k * tm + t], sems.at[slot])

    def issue(tile, slot):
        def body(t, _):
            row_copy(tile, slot, t, 0).start(priority=0)
            row_copy(tile, slot, t, 1).start(priority=1)
            return 0
        lax.fori_loop(0, tm, body, 0, unroll=8)

    @pl.when(i == 0)
    def _():
        issue(0, 0)

    @pl.when(i + 1 < n)
    def _():
        issue(i + 1, (i + 1) % 2)

    slot = i % 2

    def drain(t, _):
        row_copy(i, slot, t, 0).wait()
        row_copy(i, slot, t, 1).wait()
        return 0

    lax.fori_loop(0, tm, drain, 0, unroll=8)
    rt = rt_ref[...]
    w1 = rt[:, RT_W:RT_W + 1]
    w2 = rt[:, RT_W + 1:RT_W + 2]
    for s in range(nc):
        cols = slice(s * LANES, (s + 1) * LANES)
        y = w1 * buf[slot, s, 0:tm, :] + w2 * buf[slot, s, tm:2 * tm, :]
        o_ref[:, cols] = x_ref[:, cols] + ga_ref[0][:, cols] * y


def _combine(pos_flat, x1, rt, ga, ys, tm=256):
    B, S, D = x1.shape
    NC = ys.shape[0]
    tm = min(tm, S)
    nt = S // tm
    return pl.pallas_call(
        _combine_kernel,
        grid_spec=pltpu.PrefetchScalarGridSpec(
            num_scalar_prefetch=1,
            grid=(B * nt,),
            in_specs=[pl.BlockSpec((tm, D), lambda i, pos: (i, 0)),
                      pl.BlockSpec((tm, LANES), lambda i, pos: (i, 0)),
                      pl.BlockSpec((1, 1, D), lambda i, pos: (i // nt, 0, 0)),
                      pl.BlockSpec(memory_space=pl.ANY)],
            out_specs=pl.BlockSpec((tm, D), lambda i, pos: (i, 0)),
            scratch_shapes=[pltpu.VMEM((2, NC, 2 * tm, LANES), F32),
                            pltpu.SemaphoreType.DMA((2,))]),
        out_shape=jax.ShapeDtypeStruct((B * S, D), F32),
        compiler_params=_cparams(("arbitrary",)),
        name="moe_combine",
    )(pos_flat, x1.reshape(B * S, D), rt, ga, ys).reshape(B, S, D)


def _moe(h3, rt, x1, ga, w1, w3, w2, first_expert):
    T = rt.shape[0]
    n_tiles = 2 * T // MOE_TILE + N_EXPERTS
    pos, info = _rank(rt)
    pos_flat = pos[:, :2].reshape(-1)
    counts = info[0, :N_EXPERTS]
    ends = info[1, :N_EXPERTS] + jnp.ceil(counts * (1.0 / MOE_TILE))
    tile_ids = jnp.arange(n_tiles, dtype=F32)
    tile_expert = jnp.minimum(jnp.sum(tile_ids[:, None] >= ends[None, :], axis=1), N_EXPERTS - 1)
    n_valid = ends[N_EXPERTS - 1:].astype(jnp.int32)
    xs = _dispatch(pos_flat, h3, n_tiles * MOE_TILE)
    ys = _experts(tile_expert.astype(jnp.int32) + first_expert, n_valid, xs, w1, w3, w2)
    return _combine(pos_flat, x1, rt, ga, ys)


def kernel(x, c, w_mod, b_mod, g_norm1, w_in, gq_a, gk_a, lam_a, g_sub_a, w_pool, b_pool, pool_scale, gq_c, gk_c, w_out, g_norm2, w_rg, b_rg, w_re, b_re, w1, w3, w2):
    B, S, D = x.shape
    L = w_mod.shape[0]
    a_width = D // 2
    pool_width = D // 4
    c_width = D // 4
    W = MXU_DIM
    reps = W // HEAD_DIM

    inv = 1.0 / (ROPE_THETA ** (jnp.arange(0, HEAD_DIM, 2, dtype=F32) / HEAD_DIM))
    ang = jnp.arange(S, dtype=F32)[:, None] * inv[None, :]
    ang = jnp.concatenate([ang, ang], axis=-1)
    cos_h, sin_h = jnp.cos(ang), jnp.sin(ang)
    first = jnp.arange(HEAD_DIM) < HEAD_DIM // 2
    cos_t = jnp.tile(cos_h, (1, reps))
    sa_t = jnp.tile(jnp.where(first[None, :], 0.0, sin_h), (1, reps))
    sb_t = jnp.tile(jnp.where(first[None, :], -sin_h, 0.0), (1, reps))
    head_of = np.arange(W) // HEAD_DIM
    bd = jnp.asarray((head_of[:, None] == head_of[None, :]) / HEAD_DIM, BF16)

    qa0, ka0, va0 = 0, a_width, 2 * a_width
    ub0 = 3 * a_width
    qc0, kc0, vc0 = ub0 + pool_width, ub0 + pool_width + c_width, ub0 + pool_width + 2 * c_width
    chunks = lambda lo_, hi_: list(range(lo_ // W, hi_ // W))
    z_chunks = tuple([(j, True) for j in chunks(qa0, va0)] + [(j, False) for j in chunks(ub0, qc0)]
                     + [(j, True) for j in chunks(qc0, vc0)])
    v_chunks = tuple(chunks(va0, ub0) + chunks(vc0, w_in.shape[2]))
    zq_a, zk_a = 0, a_width // LANES
    z_ub = 2 * a_width // pool_width
    zq_c = (2 * a_width + pool_width) // LANES
    zk_c = zq_c + c_width // LANES
    scale = HEAD_DIM ** -0.5 * math.log2(math.e)
    bias_t = _dilated_tables(S)

    mod = _modulation(c, w_mod, b_mod)
    for l in range(L):
        sh1, sc1, ga1, sh2, sc2, ga2 = [m[:, None, :] for m in jnp.split(mod[l], N_MOD, axis=-1)]
        gain = jnp.ones((w_in.shape[2],), F32)
        gain = gain.at[qa0:ka0].set(jnp.tile(gq_a[l], a_width // HEAD_DIM) * scale)
        gain = gain.at[ka0:va0].set(jnp.tile(gk_a[l], a_width // HEAD_DIM))
        gain = gain.at[qc0:kc0].set(jnp.tile(gq_c[l], c_width // HEAD_DIM) * scale)
        gain = gain.at[kc0:vc0].set(jnp.tile(gk_c[l], c_width // HEAD_DIM))
        z, vt = _in_projection(x, sc1, sh1, g_norm1[l][None, :], w_in[l].astype(BF16),
                               gain[None, :], cos_t, sa_t, sb_t, bd, z_chunks, v_chunks)

        lam_init = 0.8 - 0.6 * math.exp(-0.3 * l)
        n_a = a_width // LANES
        ya = _attention("diff", z, vt, zq_a, zk_a, 0, n_a,
                        [lam_a[l], g_sub_a[l][:, None]],
                        [pl.BlockSpec(lam_a[l].shape, lambda b, h: (0, 0)),
                         pl.BlockSpec((LANES, 1), lambda b, h: (0, 0))], lam_init=lam_init)
        n_c = c_width // LANES
        yc = _attention("dil", z, vt, zq_c, zk_c, n_a, n_c,
                        [bias_t], [pl.BlockSpec(bias_t.shape, lambda b, h: (0, 0, 0))])
        w_bd = jax.scipy.linalg.block_diag(*[w_pool[l, g] for g in range(w_pool.shape[1])])
        yb = _pool_mixer(z, z_ub, w_bd.astype(BF16), b_pool[l].reshape(1, -1),
                         pool_scale[l][None, :])

        wr = jnp.zeros((D, LANES), F32)
        wr = wr.at[:, :N_GROUPS].set(w_rg[l])
        wr = wr.at[:, N_GROUPS:N_GROUPS + N_EXPERTS].set(
            w_re[l].transpose(1, 0, 2).reshape(D, N_EXPERTS))
        br = jnp.zeros((1, LANES), F32)
        br = br.at[0, :N_GROUPS].set(b_rg[l])
        br = br.at[0, N_GROUPS:N_GROUPS + N_EXPERTS].set(b_re[l].reshape(-1))
        x1, h3, rt = _out_projection(ya, yb, yc, x, w_out[l].astype(BF16), ga1, sc2, sh2,
                                     g_norm2[l][None, :], wr.astype(BF16), br)
        F = w1.shape[-1]
        x = _moe(h3, rt, x1, ga2, w1.reshape(L * N_EXPERTS, D, F), w3.reshape(L * N_EXPERTS, D, F),
                 w2.reshape(L * N_EXPERTS, F, D), l * N_EXPERTS)
    return x
```

```python
import functools
import math

import jax
import jax.numpy as jnp
import numpy as np
from jax import lax
from jax.experimental import pallas as pl
from jax.experimental.pallas import tpu as pltpu

HEAD_DIM = 64
POOL_WINDOWS = (2, 4, 8, 16)
DILATED_PAIRS = ((128, 1), (512, 4), (2048, 16))
ROPE_THETA = 10000.0
N_GROUPS = 4
EXPERTS_PER_GROUP = 8
N_EXPERTS = N_GROUPS * EXPERTS_PER_GROUP
N_MOD = 6
EPS = 1e-6

LANES = 128
BF16_ROWS = 16
MXU_DIM = 256
KV_BLOCK = 256
Q_BLOCK = 2 * KV_BLOCK
MOE_TILE = 256
DISPATCH_TILE = 512
ROW_ALIGN = 8
COPY_CHUNK = 64
RT_E, RT_W = 0, 2
VMEM_LIMIT = 48 * 1024 * 1024
NEG = -1e30
F32 = jnp.float32
BF16 = jnp.bfloat16


def _cparams(sem):
    return pltpu.CompilerParams(dimension_semantics=sem, vmem_limit_bytes=VMEM_LIMIT)


def _dot(a, b):
    return jnp.dot(a, b, preferred_element_type=F32)


def _dot_nt(a, b):
    return lax.dot_general(a, b, (((1,), (1,)), ((), ())), preferred_element_type=F32)


def _mod_kernel(c_ref, w_ref, b_ref, o_ref):
    c = c_ref[...]
    cond = c * (1.0 / (1.0 + jnp.exp(-c)))
    o_ref[0] = _dot(cond.astype(BF16), w_ref[0].astype(BF16)) + b_ref[0]


def _modulation(c, w_mod, b_mod):
    L, D, N = w_mod.shape
    B = c.shape[0]
    tn = 1024
    return pl.pallas_call(
        _mod_kernel,
        grid=(L, N // tn),
        in_specs=[pl.BlockSpec((B, D), lambda l, j: (0, 0)),
                  pl.BlockSpec((1, D, tn), lambda l, j: (l, 0, j)),
                  pl.BlockSpec((1, 1, tn), lambda l, j: (l, 0, j))],
        out_specs=pl.BlockSpec((1, B, tn), lambda l, j: (l, 0, j)),
        out_shape=jax.ShapeDtypeStruct((L, B, N), F32),
        compiler_params=_cparams(("parallel", "parallel")),
        name="modulation",
    )(c, w_mod, b_mod.reshape(L, 1, N))


def _inproj_kernel(z_chunks, v_chunks, x_ref, sc_ref, sh_ref, g_ref, w_ref, gain_ref, cos_ref,
                   sa_ref, sb_ref, bd_ref, z_ref, vt_ref):
    x = x_ref[0]
    ms = jnp.mean(x * x, axis=-1, keepdims=True)
    h = x * lax.rsqrt(ms + EPS) * g_ref[...]
    h = h * (1.0 + sc_ref[0]) + sh_ref[0]
    hb = h.astype(BF16)
    W = MXU_DIM
    for dst, (src, normed) in enumerate(z_chunks):
        zc = _dot(hb, w_ref[:, src * W:(src + 1) * W])
        if normed:
            msq = _dot((zc * zc).astype(BF16), bd_ref[...])
            y = zc * lax.rsqrt(msq + EPS) * gain_ref[:, src * W:(src + 1) * W]
            r_up = pltpu.roll(y, HEAD_DIM // 2, 1)
            r_dn = pltpu.roll(y, W - HEAD_DIM // 2, 1)
            zc = y * cos_ref[...] + r_up * sa_ref[...] + r_dn * sb_ref[...]
        z_ref[0, :, dst * W:(dst + 1) * W] = zc.astype(BF16)
    for dst, src in enumerate(v_chunks):
        zt = _dot(hb, w_ref[:, src * W:(src + 1) * W]).T.astype(BF16)
        for cb in range(vt_ref.shape[1]):
            vt_ref[0, cb, dst * W:(dst + 1) * W, :] = zt[:, cb * KV_BLOCK:(cb + 1) * KV_BLOCK]


def _in_projection(x, sc, sh, g, w_bf, gain, cos_t, sa_t, sb_t, bd, z_chunks, v_chunks, tm=512):
    B, S, D = x.shape
    N = w_bf.shape[1]
    W = MXU_DIM
    nz, nv = len(z_chunks) * W, len(v_chunks) * W
    return pl.pallas_call(
        functools.partial(_inproj_kernel, z_chunks, v_chunks),
        grid=(B, S // tm),
        in_specs=[pl.BlockSpec((1, tm, D), lambda b, i: (b, i, 0)),
                  pl.BlockSpec((1, 1, D), lambda b, i: (b, 0, 0)),
                  pl.BlockSpec((1, 1, D), lambda b, i: (b, 0, 0)),
                  pl.BlockSpec((1, D), lambda b, i: (0, 0)),
                  pl.BlockSpec((D, N), lambda b, i: (0, 0)),
                  pl.BlockSpec((1, N), lambda b, i: (0, 0)),
                  pl.BlockSpec((tm, W), lambda b, i: (i, 0)),
                  pl.BlockSpec((tm, W), lambda b, i: (i, 0)),
                  pl.BlockSpec((tm, W), lambda b, i: (i, 0)),
                  pl.BlockSpec((W, W), lambda b, i: (0, 0))],
        out_specs=[pl.BlockSpec((1, tm, nz), lambda b, i: (b, i, 0)),
                   pl.BlockSpec((1, tm // KV_BLOCK, nv, KV_BLOCK), lambda b, i: (b, i, 0, 0))],
        out_shape=[jax.ShapeDtypeStruct((B, S, nz), BF16),
                   jax.ShapeDtypeStruct((B, S // KV_BLOCK, nv, KV_BLOCK), BF16)],
        compiler_params=_cparams(("parallel", "parallel")),
        name="in_projection",
    )(x, sc, sh, g, w_bf, gain, cos_t, sa_t, sb_t, bd)


def _attn_kernel(mode, lam_init, *refs):
    if mode == "diff":
        q_ref, k_ref, vt_ref, lam_ref, g_ref, o_ref, s_a, s_b, acc_ref = refs
    else:
        q_ref, k_ref, vt_ref, bias_ref, o_ref, s_a, s_b, acc_ref = refs
    bq, bk = Q_BLOCK, KV_BLOCK
    S = q_ref.shape[1]
    nq = S // bq
    lane = lax.broadcasted_iota(jnp.int32, (1, LANES), 1)
    lo = lane < HEAD_DIM
    extra = acc_ref.shape[0] - LANES
    ones_rows = jnp.where(lax.broadcasted_iota(jnp.int32, (extra, bk), 0) == 0, 1.0, 0.0).astype(BF16)

    def q_block(qi, _):
        qs = pl.multiple_of(qi * bq, bq)
        q = q_ref[0, pl.ds(qs, bq), :]
        zero = jnp.zeros_like(q)
        q2 = jnp.concatenate([jnp.where(lo, q, zero), jnp.where(lo, zero, q)], axis=0)

        def scores(s_ref, kb):
            ks = pl.multiple_of(kb * bk, bk)
            s_ref[...] = _dot_nt(k_ref[0, pl.ds(ks, bk), :], q2)

        def table(kb):
            if mode == "diff":
                return None
            return bias_ref[jnp.minimum(2 * qi - kb + 1, bias_ref.shape[0] - 1)]

        def update(s_ref, kb, stats, bias):
            vt1 = jnp.concatenate([vt_ref[0, kb], ones_rows], axis=0)
            out = []
            for half in range(2):
                cols = slice(half * bq, (half + 1) * bq)
                m = stats[half]
                s = s_ref[:, cols]
                if bias is not None:
                    s = s + bias
                mn = jnp.maximum(m, jnp.max(s, axis=0, keepdims=True))
                p = jnp.exp2(s - mn)
                acc_ref[:, cols] = jnp.exp2(m - mn) * acc_ref[:, cols] + _dot(vt1, p.astype(BF16))
                out.append(mn)
            return tuple(out)

        def pair(j, stats):
            scores(s_b, 2 * j + 1)
            stats = update(s_a, 2 * j, stats, table(2 * j))
            scores(s_a, 2 * j + 2)
            return update(s_b, 2 * j + 1, stats, table(2 * j + 1))

        acc_ref[...] = jnp.zeros_like(acc_ref)
        scores(s_a, 0)
        stats = lax.fori_loop(0, qi, pair, (jnp.full((1, bq), NEG, F32),) * 2)

        krow = lax.broadcasted_iota(jnp.int32, (bk, bq), 0)
        qcol = lax.broadcasted_iota(jnp.int32, (bk, bq), 1)
        scores(s_b, 2 * qi + 1)
        for rel, s_ref in enumerate((s_a, s_b)):
            bias = table(2 * qi + rel)
            if mode == "diff":
                bias = jnp.where(krow + rel * bk <= qcol, 0.0, NEG).astype(F32)
            stats = update(s_ref, 2 * qi + rel, stats, bias)
        a_lo, a_hi = acc_ref[0:LANES, :bq], acc_ref[0:LANES, bq:]
        l_lo, l_hi = acc_ref[LANES:LANES + 1, :bq], acc_ref[LANES:LANES + 1, bq:]
        if mode == "diff":
            lp = lam_ref[...]
            lam = (jnp.exp(jnp.sum(lp[0:1] * lp[1:2], axis=-1, keepdims=True))
                   - jnp.exp(jnp.sum(lp[2:3] * lp[3:4], axis=-1, keepdims=True)) + lam_init)
            y = a_lo * (1.0 / l_lo) - a_hi * (lam / l_hi)
            ms = jnp.mean(y * y, axis=0, keepdims=True)
            y = y * lax.rsqrt(ms + EPS) * (g_ref[...] * (1.0 - lam_init))
        else:
            feat = lax.broadcasted_iota(jnp.int32, (LANES, 1), 0)
            y = jnp.where(feat < HEAD_DIM, a_lo * (1.0 / l_lo), a_hi * (1.0 / l_hi))
        o_ref[0, pl.ds(qs, bq), :] = y.T.astype(o_ref.dtype)
        return 0

    lax.fori_loop(0, nq, q_block, 0)


def _attention(mode, z, vt, q_col, k_col, v_row, n_blocks, extras, extra_specs, lam_init=0.0):
    B, S, _ = z.shape
    zspec = lambda col: pl.BlockSpec((1, S, LANES), lambda b, h: (b, 0, col + h))
    vspec = pl.BlockSpec((1, S // KV_BLOCK, LANES, KV_BLOCK), lambda b, h: (b, 0, v_row + h, 0))
    return pl.pallas_call(
        functools.partial(_attn_kernel, mode, lam_init),
        grid=(B, n_blocks),
        in_specs=[zspec(q_col), zspec(k_col), vspec] + extra_specs,
        out_specs=pl.BlockSpec((1, S, LANES), lambda b, h: (b, 0, h)),
        out_shape=jax.ShapeDtypeStruct((B, S, n_blocks * LANES), BF16),
        scratch_shapes=[pltpu.VMEM((KV_BLOCK, 2 * Q_BLOCK), F32), pltpu.VMEM((KV_BLOCK, 2 * Q_BLOCK), F32),
                        pltpu.VMEM((LANES + BF16_ROWS, 2 * Q_BLOCK), F32)],
        compiler_params=_cparams(("parallel", "parallel")),
        name=mode + "_attention",
    )(z, z, vt, *extras)


def _dilated_tables(S):
    bq, bk = Q_BLOCK, KV_BLOCK
    dds = np.arange(-1, S // bk)
    dist = (dds[:, None, None] * bk + np.arange(bq)[None, None, :] - np.arange(bk)[None, :, None])
    cnt = np.zeros(dist.shape, np.float32)
    for window, dil in DILATED_PAIRS:
        cnt += (dist >= 0) & (dist <= window) & (dist % dil == 0)
    n = len(dds)
    while n > 1 and np.array_equal(cnt[n - 1], cnt[n - 2]):
        n -= 1
    cnt = cnt[:n]
    return jnp.asarray(np.where(cnt > 0, np.log2(np.maximum(cnt, 1.0)), NEG).astype(np.float32))


def _pool_kernel(u_ref, w_ref, b_ref, sc_ref, o_ref):
    u = u_ref[0].astype(F32)
    S, C = u.shape
    row = lax.broadcasted_iota(jnp.int32, (S, C), 0)
    lane = lax.broadcasted_iota(jnp.int32, (S, C), 1)
    grp = lane // (C // len(POOL_WINDOWS))

    def shifted(a, k):
        return jnp.where(row >= k, pltpu.roll(a, k, 0), 0.0)

    acc = u
    win = jnp.zeros_like(u)
    width = 1
    for gi, w in enumerate(POOL_WINDOWS):
        while width < w:
            acc = acc + shifted(acc, width)
            width *= 2
        win = jnp.where(grp == gi, acc, win)
    wl = jnp.zeros_like(row)
    for gi, w in enumerate(POOL_WINDOWS):
        wl = jnp.where(grp == gi, w, wl)
    cnt = jnp.minimum(row + 1, wl).astype(F32)
    d = win / cnt - u
    y = _dot(d.astype(BF16), w_ref[...]) + b_ref[...]
    o_ref[0] = (y * sc_ref[...]).astype(o_ref.dtype)


def _pool_mixer(z, col_block, w_bd, b, scale):
    B, S, _ = z.shape
    C = w_bd.shape[0]
    return pl.pallas_call(
        _pool_kernel,
        grid=(B,),
        in_specs=[pl.BlockSpec((1, S, C), lambda b_: (b_, 0, col_block)),
                  pl.BlockSpec((C, C), lambda b_: (0, 0)),
                  pl.BlockSpec((1, C), lambda b_: (0, 0)),
                  pl.BlockSpec((1, C), lambda b_: (0, 0))],
        out_specs=pl.BlockSpec((1, S, C), lambda b_: (b_, 0, 0)),
        out_shape=jax.ShapeDtypeStruct((B, S, C), BF16),
        compiler_params=_cparams(("parallel",)),
        name="pool_mixer",
    )(z, w_bd, b, scale)


def _outproj_kernel(ya_ref, yb_ref, yc_ref, x_ref, wo_ref, ga_ref, sc_ref, sh_ref, g_ref, wr_ref,
                    br_ref, x1_ref, h_ref, rt_ref):
    na = ya_ref.shape[2]
    nb = yb_ref.shape[2]
    y = (_dot(ya_ref[0], wo_ref[0:na, :]) + _dot(yb_ref[0], wo_ref[na:na + nb, :])
         + _dot(yc_ref[0], wo_ref[na + nb:, :]))
    x1 = x_ref[0] + ga_ref[0] * y
    x1_ref[0] = x1
    ms = jnp.mean(x1 * x1, axis=-1, keepdims=True)
    h = x1 * lax.rsqrt(ms + EPS) * g_ref[...]
    h = h * (1.0 + sc_ref[0]) + sh_ref[0]
    hb = h.astype(BF16)
    for s in range(h_ref.shape[0]):
        h_ref[s] = h[:, s * LANES:(s + 1) * LANES]

    logits = _dot(hb, wr_ref[...]) + br_ref[...]
    tm = logits.shape[0]
    G, E = N_GROUPS, EXPERTS_PER_GROUP
    lane = lax.broadcasted_iota(jnp.int32, (tm, LANES), 1)
    lane_f = lane.astype(F32)
    big = float(LANES)
    is_g = lane < G
    gl = jnp.where(is_g, logits, -jnp.inf)
    gmax = jnp.max(gl, axis=-1, keepdims=True)
    gidx = jnp.min(jnp.where(gl == gmax, lane_f, big), axis=-1, keepdims=True)
    gsum = jnp.sum(jnp.where(is_g, jnp.exp(gl - gmax), 0.0), axis=-1, keepdims=True)
    g_w = 1.0 / gsum
    lane_grp = ((lane - G) // E).astype(F32)
    emask = (lane >= G) & (lane < G + G * E) & (lane_grp == gidx)
    el = jnp.where(emask, logits, -jnp.inf)
    v1 = jnp.max(el, axis=-1, keepdims=True)
    i1 = jnp.min(jnp.where(el == v1, lane_f, big), axis=-1, keepdims=True)
    el2 = jnp.where(lane_f == i1, -jnp.inf, el)
    v2 = jnp.max(el2, axis=-1, keepdims=True)
    i2 = jnp.min(jnp.where(el2 == v2, lane_f, big), axis=-1, keepdims=True)
    t = jnp.exp(v2 - v1)
    w1 = g_w / (1.0 + t)
    w2 = w1 * t
    rt_ref[...] = jnp.where(lane == RT_E, i1 - G, jnp.where(lane == RT_E + 1, i2 - G,
                            jnp.where(lane == RT_W, w1, jnp.where(lane == RT_W + 1, w2, 0.0))))


def _out_projection(ya, yb, yc, x, wo_bf, ga, sc, sh, g, wr, br, tm=512):
    B, S, D = x.shape
    nt = S // tm
    tok = lambda n: pl.BlockSpec((1, tm, n), lambda b, i: (b, i, 0))
    per_b = pl.BlockSpec((1, 1, D), lambda b, i: (b, 0, 0))
    const = lambda shape: pl.BlockSpec(shape, lambda b, i: (0, 0))
    return pl.pallas_call(
        _outproj_kernel,
        grid=(B, nt),
        in_specs=[tok(ya.shape[2]), tok(yb.shape[2]), tok(yc.shape[2]), tok(D), const((D, D)),
                  per_b, per_b, per_b, const((1, D)), const((D, LANES)), const((1, LANES))],
        out_specs=[tok(D), pl.BlockSpec((D // LANES, tm, LANES), lambda b, i: (0, b * nt + i, 0)),
                   pl.BlockSpec((tm, LANES), lambda b, i: (b * nt + i, 0))],
        out_shape=[jax.ShapeDtypeStruct((B, S, D), F32),
                   jax.ShapeDtypeStruct((D // LANES, B * S, LANES), F32),
                   jax.ShapeDtypeStruct((B * S, LANES), F32)],
        compiler_params=_cparams(("parallel", "parallel")),
        name="out_projection",
    )(ya, yb, yc, x, wo_bf, ga, sc, sh, g, wr, br)


def _rank_kernel(rt_ref, before_ref, ltri_ref, pos_ref, info_ref, tinfo_ref, cnt_ref, off_ref, carry_ref):
    ph, i = pl.program_id(0), pl.program_id(1)
    ne = N_EXPERTS
    rt_t = rt_ref[...].T
    tm = rt_t.shape[1]
    expert = lax.broadcasted_iota(jnp.int32, (ne, tm), 0).astype(F32)
    hit1 = expert == rt_t[RT_E:RT_E + 1, :]
    hit2 = expert == rt_t[RT_E + 1:RT_E + 2, :]
    onehot = jnp.where(hit1, 1.0, jnp.where(hit2, 1.0, 0.0))
    n_blk = jnp.ceil(jnp.sum(onehot, axis=1, keepdims=True) * (1.0 / ROW_ALIGN)) * ROW_ALIGN
    n_blk = jnp.broadcast_to(n_blk, (ne, LANES))

    @pl.when((ph == 0) & (i == 0))
    def _():
        cnt_ref[...] = jnp.zeros_like(cnt_ref)

    @pl.when(ph == 0)
    def _():
        cnt_ref[...] += n_blk

    @pl.when((ph == 1) & (i == 0))
    def _():
        ntile = jnp.ceil(cnt_ref[...] * (1.0 / MOE_TILE))
        off_tiles = _dot(ltri_ref[...], ntile.astype(BF16))
        off_ref[...] = off_tiles * MOE_TILE
        info_ref[0] = cnt_ref[...]
        info_ref[1] = off_tiles
        carry_ref[...] = jnp.zeros_like(carry_ref)

    @pl.when(ph == 1)
    def _():
        dst0 = carry_ref[...] + off_ref[...]
        loc0 = _dot(ltri_ref[...], (n_blk * (1.0 / ROW_ALIGN)).astype(BF16)) * ROW_ALIGN
        within = _dot(onehot.astype(BF16), before_ref[...])
        in_sorted = within + dst0[:, 0:1]
        in_buffer = within + loc0[:, 0:1]
        rows = [jnp.sum(jnp.where(hit, v, 0.0), axis=0, keepdims=True)
                for v in (in_sorted, in_buffer) for hit in (hit1, hit2)]
        rows.append(jnp.zeros((pos_ref.shape[1] - len(rows), tm), F32))
        pos_ref[0] = jnp.concatenate(rows, axis=0).astype(jnp.int32)
        tinfo_ref[0, 0] = dst0
        tinfo_ref[0, 1] = n_blk
        tinfo_ref[0, 2] = loc0
        carry_ref[...] += n_blk


def _rank(rt):
    T = rt.shape[0]
    tm = min(DISPATCH_TILE, T)
    idx = np.arange(tm)
    before = jnp.asarray(idx[:, None] < idx[None, :], BF16)
    ex = np.arange(N_EXPERTS)
    ltri = jnp.asarray(ex[None, :] < ex[:, None], BF16)
    stat = pltpu.VMEM((N_EXPERTS, LANES), F32)
    return pl.pallas_call(
        _rank_kernel,
        grid=(2, T // tm),
        in_specs=[pl.BlockSpec((tm, LANES), lambda ph, i: (i, 0)),
                  pl.BlockSpec((tm, tm), lambda ph, i: (0, 0)),
                  pl.BlockSpec((N_EXPERTS, N_EXPERTS), lambda ph, i: (0, 0))],
        out_specs=[pl.BlockSpec((1, 8, tm), lambda ph, i: (i * ph, 0, 0)),
                   pl.BlockSpec((2, N_EXPERTS, LANES), lambda ph, i: (0, 0, 0)),
                   pl.BlockSpec((1, 3, N_EXPERTS, LANES), lambda ph, i: (i * ph, 0, 0, 0))],
        out_shape=[jax.ShapeDtypeStruct((T // tm, 8, tm), jnp.int32),
                   jax.ShapeDtypeStruct((2, N_EXPERTS, LANES), F32),
                   jax.ShapeDtypeStruct((T // tm, 3, N_EXPERTS, LANES), F32)],
        scratch_shapes=[stat, stat, stat],
        compiler_params=_cparams(("arbitrary", "arbitrary")),
        name="moe_rank",
    )(rt, before, ltri)


def _block_copies(n, src_of, dst_of, sem, act):
    def whole(j, _):
        off = pl.multiple_of(j * COPY_CHUNK, COPY_CHUNK)
        act(pltpu.make_async_copy(src_of(off, COPY_CHUNK), dst_of(off, COPY_CHUNK), sem))
        return 0
    lax.fori_loop(0, n // COPY_CHUNK, whole, 0)
    for b in range(COPY_CHUNK.bit_length() - 2, ROW_ALIGN.bit_length() - 2, -1):
        size = 1 << b

        @pl.when((n >> b) & 1 == 1)
        def _():
            off = pl.multiple_of((n >> (b + 1)) << (b + 1), ROW_ALIGN)
            act(pltpu.make_async_copy(src_of(off, size), dst_of(off, size), sem))


def _dispatch_kernel(dst_ref, nblk_ref, loc_ref, tail0_ref, tailn_ref, nv_ref, h_ref, lpos_ref, xs_ref,
                     obuf, zbuf, sems, zsem):
    i = pl.program_id(0)
    n = pl.num_programs(0)
    nc, tm = h_ref.shape[0], h_ref.shape[1]
    R = obuf.shape[2]
    slot = i % 2

    def tile_copies(tile, sl, act):
        def body(e, _):
            c = tile * N_EXPERTS + e
            loc = pl.multiple_of(loc_ref[c], ROW_ALIGN)
            dst = pl.multiple_of(dst_ref[c], ROW_ALIGN)
            _block_copies(nblk_ref[c],
                          lambda off, size: obuf.at[sl, :, pl.ds(loc + off, size)],
                          lambda off, size: xs_ref.at[:, pl.ds(dst + off, size)],
                          sems.at[sl], act)
            return 0
        lax.fori_loop(0, N_EXPERTS, body, 0)

    @pl.when(i >= 2)
    def _():
        tile_copies(i - 2, slot, lambda cp: cp.wait())

    hb = jnp.concatenate([h_ref[s] for s in range(nc)], axis=1).astype(BF16)
    lp = lpos_ref[0].astype(F32)
    l1, l2 = lp[2:3, :], lp[3:4, :]
    blk = MXU_DIM
    for jb in range(R // blk):
        r = (lax.broadcasted_iota(jnp.int32, (blk, tm), 0) + jb * blk).astype(F32)
        sel = jnp.where(r == l1, 1.0, jnp.where(r == l2, 1.0, 0.0)).astype(BF16)
        rows = _dot(sel, hb)
        for s in range(nc):
            obuf[slot, s, jb * blk:(jb + 1) * blk, :] = rows[:, s * LANES:(s + 1) * LANES]
    tile_copies(i, slot, lambda cp: cp.start())

    @pl.when(i == n - 1)
    def _():
        @pl.when(i >= 1)
        def _():
            tile_copies(i - 1, 1 - slot, lambda cp: cp.wait())
        tile_copies(i, slot, lambda cp: cp.wait())
        zbuf[...] = jnp.zeros_like(zbuf)

        def zero_fill(act):
            def body(e, _):
                t0 = pl.multiple_of(tail0_ref[e], ROW_ALIGN)
                _block_copies(tailn_ref[e], lambda off, size: zbuf.at[:, pl.ds(0, size)],
                              lambda off, size: xs_ref.at[:, pl.ds(t0 + off, size)], zsem, act)
                return 0
            lax.fori_loop(0, N_EXPERTS, body, 0)

            def unused(j, _):
                j0 = pl.multiple_of(j * MOE_TILE, MOE_TILE)
                act(pltpu.make_async_copy(zbuf, xs_ref.at[:, pl.ds(j0, MOE_TILE)], zsem))
                return 0
            lax.fori_loop(nv_ref[0], xs_ref.shape[1] // MOE_TILE, unused, 0)
        zero_fill(lambda cp: cp.start())
        zero_fill(lambda cp: cp.wait())


def _dispatch(dst0, nblk, loc0, tail0, tailn, n_valid, h3, pos, n_rows):
    NC, T, _ = h3.shape
    tm = min(DISPATCH_TILE, T)
    R = 2 * tm + N_EXPERTS * ROW_ALIGN
    R = -(-R // MXU_DIM) * MXU_DIM
    return pl.pallas_call(
        _dispatch_kernel,
        grid_spec=pltpu.PrefetchScalarGridSpec(
            num_scalar_prefetch=6,
            grid=(T // tm,),
            in_specs=[pl.BlockSpec((NC, tm, LANES), lambda i, *_: (0, i, 0)),
                      pl.BlockSpec((1, pos.shape[1], tm), lambda i, *_: (i, 0, 0))],
            out_specs=pl.BlockSpec(memory_space=pl.ANY),
            scratch_shapes=[pltpu.VMEM((2, NC, R, LANES), F32), pltpu.VMEM((NC, MOE_TILE, LANES), F32),
                            pltpu.SemaphoreType.DMA((2,)), pltpu.SemaphoreType.DMA]),
        out_shape=jax.ShapeDtypeStruct((NC, n_rows, LANES), F32),
        compiler_params=_cparams(("arbitrary",)),
        name="moe_dispatch",
    )(dst0, nblk, loc0, tail0, tailn, n_valid, h3, pos)


def _experts_kernel(te_ref, nv_ref, xs_ref, w1_ref, w3_ref, w2_ref, ys_ref, w1b, w3b, w2b):
    j = pl.program_id(0)
    e = te_ref[j]
    e_prev = te_ref[jnp.maximum(j - 1, 0)]

    @pl.when((j == 0) | (e != e_prev))
    def _():
        w1b[...] = w1_ref[0].astype(BF16)
        w3b[...] = w3_ref[0].astype(BF16)
        w2b[...] = w2_ref[0].astype(BF16)

    @pl.when(j < nv_ref[0])
    def _():
        nc = xs_ref.shape[0]
        h = jnp.concatenate([xs_ref[s] for s in range(nc)], axis=1).astype(BF16)
        a = _dot(h, w1b[...])
        b = _dot(h, w3b[...])
        hid = (a * (1.0 / (1.0 + jnp.exp(-a))) * b).astype(BF16)
        y = _dot(hid, w2b[...])
        for s in range(nc):
            ys_ref[s] = y[:, s * LANES:(s + 1) * LANES]

    @pl.when(j >= nv_ref[0])
    def _():
        ys_ref[...] = jnp.zeros_like(ys_ref)


def _experts(tile_expert, n_valid, xs, w1, w3, w2):
    NC, P, _ = xs.shape
    NE, D, F = w1.shape
    nt = P // MOE_TILE
    tile = lambda j, te, nv: (0, jnp.minimum(j, nv[0] - 1), 0)
    wspec = lambda shape: pl.BlockSpec(shape, lambda j, te, nv: (te[j], 0, 0))
    return pl.pallas_call(
        _experts_kernel,
        grid_spec=pltpu.PrefetchScalarGridSpec(
            num_scalar_prefetch=2,
            grid=(nt,),
            in_specs=[pl.BlockSpec((NC, MOE_TILE, LANES), tile),
                      wspec((1, D, F)), wspec((1, D, F)), wspec((1, F, D))],
            out_specs=pl.BlockSpec((NC, MOE_TILE, LANES), lambda j, te, nv: (0, j, 0)),
            scratch_shapes=[pltpu.VMEM((D, F), BF16), pltpu.VMEM((D, F), BF16),
                            pltpu.VMEM((F, D), BF16)]),
        out_shape=jax.ShapeDtypeStruct((NC, P, LANES), F32),
        compiler_params=_cparams(("arbitrary",)),
        name="moe_experts",
    )(tile_expert, n_valid, xs, w1, w3, w2)


def _combine_kernel(pos_ref, x_ref, rt_ref, ga_ref, ys_ref, o_ref, buf, sems):
    i = pl.program_id(0)
    n = pl.num_programs(0)
    tm = x_ref.shape[0]
    nc = buf.shape[1]

    def row_copy(tile, slot, t, k):
        src = ys_ref.at[:, pos_ref[tile * (2 * tm) + 2 * t + k]]
        return pltpu.make_async_copy(src, buf.at[slot, :, k * tm + t], sems.at[slot])

    def issue(tile, slot):
        def body(t, _):
            row_copy(tile, slot, t, 0).start(priority=0)
            row_copy(tile, slot, t, 1).start(priority=1)
            return 0
        lax.fori_loop(0, tm, body, 0, unroll=8)

    @pl.when(i == 0)
    def _():
        issue(0, 0)

    @pl.when(i + 1 < n)
    def _():
        issue(i + 1, (i + 1) % 2)

    slot = i % 2

    def drain(t, _):
        row_copy(i, slot, t, 0).wait()
        row_copy(i, slot, t, 1).wait()
        return 0

    lax.fori_loop(0, tm, drain, 0, unroll=8)
    rt = rt_ref[...]
    w1 = rt[:, RT_W:RT_W + 1]
    w2 = rt[:, RT_W + 1:RT_W + 2]
    for s in range(nc):
        cols = slice(s * LANES, (s + 1) * LANES)
        y = w1 * buf[slot, s, 0:tm, :] + w2 * buf[slot, s, tm:2 * tm, :]
        o_ref[:, cols] = x_ref[:, cols] + ga_ref[0][:, cols] * y


def _combine(pos_flat, x1, rt, ga, ys, tm=256):
    B, S, D = x1.shape
    NC = ys.shape[0]
    tm = min(tm, S)
    nt = S // tm
    return pl.pallas_call(
        _combine_kernel,
        grid_spec=pltpu.PrefetchScalarGridSpec(
            num_scalar_prefetch=1,
            grid=(B * nt,),
            in_specs=[pl.BlockSpec((tm, D), lambda i, pos: (i, 0)),
                      pl.BlockSpec((tm, LANES), lambda i, pos: (i, 0)),
                      pl.BlockSpec((1, 1, D), lambda i, pos: (i // nt, 0, 0)),
                      pl.BlockSpec(memory_space=pl.ANY)],
            out_specs=pl.BlockSpec((tm, D), lambda i, pos: (i, 0)),
            scratch_shapes=[pltpu.VMEM((2, NC, 2 * tm, LANES), F32),
                            pltpu.SemaphoreType.DMA((2,))]),
        out_shape=jax.ShapeDtypeStruct((B * S, D), F32),
        compiler_params=_cparams(("arbitrary",)),
        name="moe_combine",
    )(pos_flat, x1.reshape(B * S, D), rt, ga, ys).reshape(B, S, D)


def _moe(h3, rt, x1, ga, w1, w3, w2, first_expert):
    T = rt.shape[0]
    n_tok_tiles = T // min(DISPATCH_TILE, T)
    n_tiles = -(-(2 * T + n_tok_tiles * N_EXPERTS * ROW_ALIGN) // MOE_TILE) + N_EXPERTS
    pos, info, tinfo = _rank(rt)
    pos_flat = pos[:, :2, :].transpose(0, 2, 1).reshape(-1)
    counts = info[0, :, 0]
    starts = info[1, :, 0]
    ends = starts + jnp.ceil(counts * (1.0 / MOE_TILE))
    tile_ids = jnp.arange(n_tiles, dtype=F32)
    tile_expert = jnp.minimum(jnp.sum(tile_ids[:, None] >= ends[None, :], axis=1), N_EXPERTS - 1)
    n_valid = ends[N_EXPERTS - 1:].astype(jnp.int32)
    as_ints = lambda a: a.astype(jnp.int32).reshape(-1)
    xs = _dispatch(as_ints(tinfo[:, 0, :, 0]), as_ints(tinfo[:, 1, :, 0]),
                   as_ints(tinfo[:, 2, :, 0]), as_ints(starts * MOE_TILE + counts),
                   as_ints(ends * MOE_TILE - starts * MOE_TILE - counts), n_valid, h3, pos,
                   n_tiles * MOE_TILE)
    ys = _experts(tile_expert.astype(jnp.int32) + first_expert, n_valid, xs, w1, w3, w2)
    return _combine(pos_flat, x1, rt, ga, ys)


def kernel(x, c, w_mod, b_mod, g_norm1, w_in, gq_a, gk_a, lam_a, g_sub_a, w_pool, b_pool, pool_scale, gq_c, gk_c, w_out, g_norm2, w_rg, b_rg, w_re, b_re, w1, w3, w2):
    B, S, D = x.shape
    L = w_mod.shape[0]
    a_width = D // 2
    pool_width = D // 4
    c_width = D // 4
    W = MXU_DIM
    reps = W // HEAD_DIM

    inv = 1.0 / (ROPE_THETA ** (jnp.arange(0, HEAD_DIM, 2, dtype=F32) / HEAD_DIM))
    ang = jnp.arange(S, dtype=F32)[:, None] * inv[None, :]
    ang = jnp.concatenate([ang, ang], axis=-1)
    cos_h, sin_h = jnp.cos(ang), jnp.sin(ang)
    first = jnp.arange(HEAD_DIM) < HEAD_DIM // 2
    cos_t = jnp.tile(cos_h, (1, reps))
    sa_t = jnp.tile(jnp.where(first[None, :], 0.0, sin_h), (1, reps))
    sb_t = jnp.tile(jnp.where(first[None, :], -sin_h, 0.0), (1, reps))
    head_of = np.arange(W) // HEAD_DIM
    bd = jnp.asarray((head_of[:, None] == head_of[None, :]) / HEAD_DIM, BF16)

    qa0, ka0, va0 = 0, a_width, 2 * a_width
    ub0 = 3 * a_width
    qc0, kc0, vc0 = ub0 + pool_width, ub0 + pool_width + c_width, ub0 + pool_width + 2 * c_width
    chunks = lambda lo_, hi_: list(range(lo_ // W, hi_ // W))
    z_chunks = tuple([(j, True) for j in chunks(qa0, va0)] + [(j, False) for j in chunks(ub0, qc0)]
                     + [(j, True) for j in chunks(qc0, vc0)])
    v_chunks = tuple(chunks(va0, ub0) + chunks(vc0, w_in.shape[2]))
    zq_a, zk_a = 0, a_width // LANES
    z_ub = 2 * a_width // pool_width
    zq_c = (2 * a_width + pool_width) // LANES
    zk_c = zq_c + c_width // LANES
    scale = HEAD_DIM ** -0.5 * math.log2(math.e)
    bias_t = _dilated_tables(S)

    mod = _modulation(c, w_mod, b_mod)
    for l in range(L):
        sh1, sc1, ga1, sh2, sc2, ga2 = [m[:, None, :] for m in jnp.split(mod[l], N_MOD, axis=-1)]
        gain = jnp.ones((w_in.shape[2],), F32)
        gain = gain.at[qa0:ka0].set(jnp.tile(gq_a[l], a_width // HEAD_DIM) * scale)
        gain = gain.at[ka0:va0].set(jnp.tile(gk_a[l], a_width // HEAD_DIM))
        gain = gain.at[qc0:kc0].set(jnp.tile(gq_c[l], c_width // HEAD_DIM) * scale)
        gain = gain.at[kc0:vc0].set(jnp.tile(gk_c[l], c_width // HEAD_DIM))
        z, vt = _in_projection(x, sc1, sh1, g_norm1[l][None, :], w_in[l].astype(BF16),
                               gain[None, :], cos_t, sa_t, sb_t, bd, z_chunks, v_chunks)

        lam_init = 0.8 - 0.6 * math.exp(-0.3 * l)
        n_a = a_width // LANES
        ya = _attention("diff", z, vt, zq_a, zk_a, 0, n_a,
                        [lam_a[l], g_sub_a[l][:, None]],
                        [pl.BlockSpec(lam_a[l].shape, lambda b, h: (0, 0)),
                         pl.BlockSpec((LANES, 1), lambda b, h: (0, 0))], lam_init=lam_init)
        n_c = c_width // LANES
        yc = _attention("dil", z, vt, zq_c, zk_c, n_a, n_c,
                        [bias_t], [pl.BlockSpec(bias_t.shape, lambda b, h: (0, 0, 0))])
        w_bd = jax.scipy.linalg.block_diag(*[w_pool[l, g] for g in range(w_pool.shape[1])])
        yb = _pool_mixer(z, z_ub, w_bd.astype(BF16), b_pool[l].reshape(1, -1),
                         pool_scale[l][None, :])

        wr = jnp.zeros((D, LANES), F32)
        wr = wr.at[:, :N_GROUPS].set(w_rg[l])
        wr = wr.at[:, N_GROUPS:N_GROUPS + N_EXPERTS].set(
            w_re[l].transpose(1, 0, 2).reshape(D, N_EXPERTS))
        br = jnp.zeros((1, LANES), F32)
        br = br.at[0, :N_GROUPS].set(b_rg[l])
        br = br.at[0, N_GROUPS:N_GROUPS + N_EXPERTS].set(b_re[l].reshape(-1))
        x1, h3, rt = _out_projection(ya, yb, yc, x, w_out[l].astype(BF16), ga1, sc2, sh2,
                                     g_norm2[l][None, :], wr.astype(BF16), br)
        F = w1.shape[-1]
        x = _moe(h3, rt, x1, ga2, w1.reshape(L * N_EXPERTS, D, F), w3.reshape(L * N_EXPERTS, D, F),
                 w2.reshape(L * N_EXPERTS, F, D), l * N_EXPERTS)
    return x
```

```python
import functools
import math

import jax
import jax.numpy as jnp
import numpy as np
from jax import lax
from jax.experimental import pallas as pl
from jax.experimental.pallas import tpu as pltpu

HEAD_DIM = 64
POOL_WINDOWS = (2, 4, 8, 16)
DILATED_PAIRS = ((128, 1), (512, 4), (2048, 16))
ROPE_THETA = 10000.0
N_GROUPS = 4
EXPERTS_PER_GROUP = 8
N_EXPERTS = N_GROUPS * EXPERTS_PER_GROUP
N_MOD = 6
EPS = 1e-6

LANES = 128
BF16_ROWS = 16
MXU_DIM = 256
KV_BLOCK = 256
Q_BLOCK = 2 * KV_BLOCK
MOE_TILE = 256
DISPATCH_TILE = 512
ROW_ALIGN = 8
COPY_CHUNK = 64
RT_E, RT_W = 0, 2
VMEM_LIMIT = 48 * 1024 * 1024
NEG = -1e30
F32 = jnp.float32
BF16 = jnp.bfloat16


def _cparams(sem):
    return pltpu.CompilerParams(dimension_semantics=sem, vmem_limit_bytes=VMEM_LIMIT)


def _dot(a, b):
    return jnp.dot(a, b, preferred_element_type=F32)


def _dot_nt(a, b):
    return lax.dot_general(a, b, (((1,), (1,)), ((), ())), preferred_element_type=F32)


def _mod_kernel(c_ref, w_ref, b_ref, o_ref):
    c = c_ref[...]
    cond = c * (1.0 / (1.0 + jnp.exp(-c)))
    o_ref[0] = _dot(cond.astype(BF16), w_ref[0].astype(BF16)) + b_ref[0]


def _modulation(c, w_mod, b_mod):
    L, D, N = w_mod.shape
    B = c.shape[0]
    tn = 1024
    return pl.pallas_call(
        _mod_kernel,
        grid=(L, N // tn),
        in_specs=[pl.BlockSpec((B, D), lambda l, j: (0, 0)),
                  pl.BlockSpec((1, D, tn), lambda l, j: (l, 0, j)),
                  pl.BlockSpec((1, 1, tn), lambda l, j: (l, 0, j))],
        out_specs=pl.BlockSpec((1, B, tn), lambda l, j: (l, 0, j)),
        out_shape=jax.ShapeDtypeStruct((L, B, N), F32),
        compiler_params=_cparams(("parallel", "parallel")),
        name="modulation",
    )(c, w_mod, b_mod.reshape(L, 1, N))


def _inproj_kernel(z_chunks, v_chunks, x_ref, sc_ref, sh_ref, g_ref, w_ref, gain_ref, cos_ref,
                   sa_ref, sb_ref, bd_ref, z_ref, vt_ref):
    x = x_ref[0]
    ms = jnp.mean(x * x, axis=-1, keepdims=True)
    h = x * lax.rsqrt(ms + EPS) * g_ref[...]
    h = h * (1.0 + sc_ref[0]) + sh_ref[0]
    hb = h.astype(BF16)
    W = MXU_DIM
    for dst, (src, normed) in enumerate(z_chunks):
        zc = _dot(hb, w_ref[:, src * W:(src + 1) * W])
        if normed:
            msq = _dot((zc * zc).astype(BF16), bd_ref[...])
            y = zc * lax.rsqrt(msq + EPS) * gain_ref[:, src * W:(src + 1) * W]
            r_up = pltpu.roll(y, HEAD_DIM // 2, 1)
            r_dn = pltpu.roll(y, W - HEAD_DIM // 2, 1)
            zc = y * cos_ref[...] + r_up * sa_ref[...] + r_dn * sb_ref[...]
        z_ref[0, :, dst * W:(dst + 1) * W] = zc.astype(BF16)
    for dst, src in enumerate(v_chunks):
        zt = _dot(hb, w_ref[:, src * W:(src + 1) * W]).T.astype(BF16)
        for cb in range(vt_ref.shape[1]):
            vt_ref[0, cb, dst * W:(dst + 1) * W, :] = zt[:, cb * KV_BLOCK:(cb + 1) * KV_BLOCK]


def _in_projection(x, sc, sh, g, w_bf, gain, cos_t, sa_t, sb_t, bd, z_chunks, v_chunks, tm=512):
    B, S, D = x.shape
    N = w_bf.shape[1]
    W = MXU_DIM
    nz, nv = len(z_chunks) * W, len(v_chunks) * W
    return pl.pallas_call(
        functools.partial(_inproj_kernel, z_chunks, v_chunks),
        grid=(B, S // tm),
        in_specs=[pl.BlockSpec((1, tm, D), lambda b, i: (b, i, 0)),
                  pl.BlockSpec((1, 1, D), lambda b, i: (b, 0, 0)),
                  pl.BlockSpec((1, 1, D), lambda b, i: (b, 0, 0)),
                  pl.BlockSpec((1, D), lambda b, i: (0, 0)),
                  pl.BlockSpec((D, N), lambda b, i: (0, 0)),
                  pl.BlockSpec((1, N), lambda b, i: (0, 0)),
                  pl.BlockSpec((tm, W), lambda b, i: (i, 0)),
                  pl.BlockSpec((tm, W), lambda b, i: (i, 0)),
                  pl.BlockSpec((tm, W), lambda b, i: (i, 0)),
                  pl.BlockSpec((W, W), lambda b, i: (0, 0))],
        out_specs=[pl.BlockSpec((1, tm, nz), lambda b, i: (b, i, 0)),
                   pl.BlockSpec((1, tm // KV_BLOCK, nv, KV_BLOCK), lambda b, i: (b, i, 0, 0))],
        out_shape=[jax.ShapeDtypeStruct((B, S, nz), BF16),
                   jax.ShapeDtypeStruct((B, S // KV_BLOCK, nv, KV_BLOCK), BF16)],
        compiler_params=_cparams(("parallel", "parallel")),
        name="in_projection",
    )(x, sc, sh, g, w_bf, gain, cos_t, sa_t, sb_t, bd)


def _attn_kernel(mode, lam_init, *refs):
    if mode == "diff":
        q_ref, k_ref, vt_ref, lam_ref, g_ref, o_ref, s_a, s_b, acc_ref = refs
    else:
        q_ref, k_ref, vt_ref, bias_ref, o_ref, s_a, s_b, acc_ref = refs
    bq, bk = Q_BLOCK, KV_BLOCK
    S = q_ref.shape[1]
    nq = S // bq
    lane = lax.broadcasted_iota(jnp.int32, (1, LANES), 1)
    lo = lane < HEAD_DIM
    extra = acc_ref.shape[0] - LANES
    ones_rows = jnp.where(lax.broadcasted_iota(jnp.int32, (extra, bk), 0) == 0, 1.0, 0.0).astype(BF16)

    def both_maps(qi):
        q = q_ref[0, pl.ds(pl.multiple_of(qi * bq, bq), bq), :]
        zero = jnp.zeros_like(q)
        return jnp.concatenate([jnp.where(lo, q, zero), jnp.where(lo, zero, q)], axis=0)

    def scores(s_ref, kb, q2):
        ks = pl.multiple_of(kb * bk, bk)
        s_ref[...] = _dot_nt(k_ref[0, pl.ds(ks, bk), :], q2)

    def q_block(qi, _):
        qs = pl.multiple_of(qi * bq, bq)
        q2 = both_maps(qi)

        def table(kb):
            if mode == "diff":
                return None
            return bias_ref[jnp.minimum(2 * qi - kb + 1, bias_ref.shape[0] - 1)]

        def update(s_ref, kb, stats, bias, first=0):
            vt1 = jnp.concatenate([vt_ref[0, kb], ones_rows], axis=0)
            out = []
            for half in range(2):
                cols = slice(half * bq + first, (half + 1) * bq)
                m = stats[half]
                s = s_ref[:, cols]
                if bias is not None:
                    s = s + bias[:, first:]
                mn = jnp.maximum(m[:, first:], jnp.max(s, axis=0, keepdims=True))
                p = jnp.exp2(s - mn)
                acc_ref[:, cols] = (jnp.exp2(m[:, first:] - mn) * acc_ref[:, cols]
                                    + _dot(vt1, p.astype(BF16)))
                out.append(jnp.concatenate([m[:, :first], mn], axis=1) if first else mn)
            return tuple(out)

        def pair(j, stats):
            scores(s_b, 2 * j + 1, q2)
            stats = update(s_a, 2 * j, stats, table(2 * j))
            scores(s_a, 2 * j + 2, q2)
            return update(s_b, 2 * j + 1, stats, table(2 * j + 1))

        acc_ref[...] = jnp.zeros_like(acc_ref)
        stats = lax.fori_loop(0, qi, pair, (jnp.full((1, bq), NEG, F32),) * 2)

        krow = lax.broadcasted_iota(jnp.int32, (bk, bq), 0)
        qcol = lax.broadcasted_iota(jnp.int32, (bk, bq), 1)
        scores(s_b, 2 * qi + 1, q2)
        for rel, s_ref in enumerate((s_a, s_b)):
            bias = table(2 * qi + rel)
            if mode == "diff":
                bias = jnp.where(krow + rel * bk <= qcol, 0.0, NEG).astype(F32)
            stats = update(s_ref, 2 * qi + rel, stats, bias, first=rel * bk)
            if rel == 0:
                scores(s_a, 0, both_maps(jnp.minimum(qi + 1, nq - 1)))
        a_lo, a_hi = acc_ref[0:LANES, :bq], acc_ref[0:LANES, bq:]
        l_lo, l_hi = acc_ref[LANES:LANES + 1, :bq], acc_ref[LANES:LANES + 1, bq:]
        if mode == "diff":
            lp = lam_ref[...]
            lam = (jnp.exp(jnp.sum(lp[0:1] * lp[1:2], axis=-1, keepdims=True))
                   - jnp.exp(jnp.sum(lp[2:3] * lp[3:4], axis=-1, keepdims=True)) + lam_init)
            y = a_lo * (1.0 / l_lo) - a_hi * (lam / l_hi)
            ms = jnp.mean(y * y, axis=0, keepdims=True)
            y = y * lax.rsqrt(ms + EPS) * (g_ref[...] * (1.0 - lam_init))
        else:
            feat = lax.broadcasted_iota(jnp.int32, (LANES, 1), 0)
            y = jnp.where(feat < HEAD_DIM, a_lo * (1.0 / l_lo), a_hi * (1.0 / l_hi))
        o_ref[0, pl.ds(qs, bq), :] = y.T.astype(o_ref.dtype)
        return 0

    scores(s_a, 0, both_maps(0))
    lax.fori_loop(0, nq, q_block, 0)


def _attention(mode, z, vt, q_col, k_col, v_row, n_blocks, extras, extra_specs, lam_init=0.0):
    B, S, _ = z.shape
    zspec = lambda col: pl.BlockSpec((1, S, LANES), lambda b, h: (b, 0, col + h))
    vspec = pl.BlockSpec((1, S // KV_BLOCK, LANES, KV_BLOCK), lambda b, h: (b, 0, v_row + h, 0))
    return pl.pallas_call(
        functools.partial(_attn_kernel, mode, lam_init),
        grid=(B, n_blocks),
        in_specs=[zspec(q_col), zspec(k_col), vspec] + extra_specs,
        out_specs=pl.BlockSpec((1, S, LANES), lambda b, h: (b, 0, h)),
        out_shape=jax.ShapeDtypeStruct((B, S, n_blocks * LANES), BF16),
        scratch_shapes=[pltpu.VMEM((KV_BLOCK, 2 * Q_BLOCK), F32), pltpu.VMEM((KV_BLOCK, 2 * Q_BLOCK), F32),
                        pltpu.VMEM((LANES + BF16_ROWS, 2 * Q_BLOCK), F32)],
        compiler_params=_cparams(("parallel", "parallel")),
        name=mode + "_attention",
    )(z, z, vt, *extras)


def _dilated_tables(S):
    bq, bk = Q_BLOCK, KV_BLOCK
    dds = np.arange(-1, S // bk)
    dist = (dds[:, None, None] * bk + np.arange(bq)[None, None, :] - np.arange(bk)[None, :, None])
    cnt = np.zeros(dist.shape, np.float32)
    for window, dil in DILATED_PAIRS:
        cnt += (dist >= 0) & (dist <= window) & (dist % dil == 0)
    n = len(dds)
    while n > 1 and np.array_equal(cnt[n - 1], cnt[n - 2]):
        n -= 1
    cnt = cnt[:n]
    return jnp.asarray(np.where(cnt > 0, np.log2(np.maximum(cnt, 1.0)), NEG).astype(np.float32))


def _pool_kernel(u_ref, w_ref, b_ref, sc_ref, o_ref):
    u = u_ref[0].astype(F32)
    S, C = u.shape
    row = lax.broadcasted_iota(jnp.int32, (S, C), 0)
    lane = lax.broadcasted_iota(jnp.int32, (S, C), 1)
    grp = lane // (C // len(POOL_WINDOWS))

    def shifted(a, k):
        return jnp.where(row >= k, pltpu.roll(a, k, 0), 0.0)

    acc = u
    win = jnp.zeros_like(u)
    width = 1
    for gi, w in enumerate(POOL_WINDOWS):
        while width < w:
            acc = acc + shifted(acc, width)
            width *= 2
        win = jnp.where(grp == gi, acc, win)
    wl = jnp.zeros_like(row)
    for gi, w in enumerate(POOL_WINDOWS):
        wl = jnp.where(grp == gi, w, wl)
    cnt = jnp.minimum(row + 1, wl).astype(F32)
    d = win / cnt - u
    y = _dot(d.astype(BF16), w_ref[...]) + b_ref[...]
    o_ref[0] = (y * sc_ref[...]).astype(o_ref.dtype)


def _pool_mixer(z, col_block, w_bd, b, scale):
    B, S, _ = z.shape
    C = w_bd.shape[0]
    return pl.pallas_call(
        _pool_kernel,
        grid=(B,),
        in_specs=[pl.BlockSpec((1, S, C), lambda b_: (b_, 0, col_block)),
                  pl.BlockSpec((C, C), lambda b_: (0, 0)),
                  pl.BlockSpec((1, C), lambda b_: (0, 0)),
                  pl.BlockSpec((1, C), lambda b_: (0, 0))],
        out_specs=pl.BlockSpec((1, S, C), lambda b_: (b_, 0, 0)),
        out_shape=jax.ShapeDtypeStruct((B, S, C), BF16),
        compiler_params=_cparams(("parallel",)),
        name="pool_mixer",
    )(z, w_bd, b, scale)


def _outproj_kernel(ya_ref, yb_ref, yc_ref, x_ref, wo_ref, ga_ref, sc_ref, sh_ref, g_ref, wr_ref,
                    br_ref, x1_ref, h_ref, rt_ref):
    na = ya_ref.shape[2]
    nb = yb_ref.shape[2]
    y = (_dot(ya_ref[0], wo_ref[0:na, :]) + _dot(yb_ref[0], wo_ref[na:na + nb, :])
         + _dot(yc_ref[0], wo_ref[na + nb:, :]))
    x1 = x_ref[0] + ga_ref[0] * y
    x1_ref[0] = x1
    ms = jnp.mean(x1 * x1, axis=-1, keepdims=True)
    h = x1 * lax.rsqrt(ms + EPS) * g_ref[...]
    h = h * (1.0 + sc_ref[0]) + sh_ref[0]
    hb = h.astype(BF16)
    for s in range(h_ref.shape[0]):
        h_ref[s] = h[:, s * LANES:(s + 1) * LANES]

    logits = _dot(hb, wr_ref[...]) + br_ref[...]
    tm = logits.shape[0]
    G, E = N_GROUPS, EXPERTS_PER_GROUP
    lane = lax.broadcasted_iota(jnp.int32, (tm, LANES), 1)
    lane_f = lane.astype(F32)
    big = float(LANES)
    is_g = lane < G
    gl = jnp.where(is_g, logits, -jnp.inf)
    gmax = jnp.max(gl, axis=-1, keepdims=True)
    gidx = jnp.min(jnp.where(gl == gmax, lane_f, big), axis=-1, keepdims=True)
    gsum = jnp.sum(jnp.where(is_g, jnp.exp(gl - gmax), 0.0), axis=-1, keepdims=True)
    g_w = 1.0 / gsum
    lane_grp = ((lane - G) // E).astype(F32)
    emask = (lane >= G) & (lane < G + G * E) & (lane_grp == gidx)
    el = jnp.where(emask, logits, -jnp.inf)
    v1 = jnp.max(el, axis=-1, keepdims=True)
    i1 = jnp.min(jnp.where(el == v1, lane_f, big), axis=-1, keepdims=True)
    el2 = jnp.where(lane_f == i1, -jnp.inf, el)
    v2 = jnp.max(el2, axis=-1, keepdims=True)
    i2 = jnp.min(jnp.where(el2 == v2, lane_f, big), axis=-1, keepdims=True)
    t = jnp.exp(v2 - v1)
    w1 = g_w / (1.0 + t)
    w2 = w1 * t
    rt_ref[...] = jnp.where(lane == RT_E, i1 - G, jnp.where(lane == RT_E + 1, i2 - G,
                            jnp.where(lane == RT_W, w1, jnp.where(lane == RT_W + 1, w2, 0.0))))


def _out_projection(ya, yb, yc, x, wo_bf, ga, sc, sh, g, wr, br, tm=512):
    B, S, D = x.shape
    nt = S // tm
    tok = lambda n: pl.BlockSpec((1, tm, n), lambda b, i: (b, i, 0))
    per_b = pl.BlockSpec((1, 1, D), lambda b, i: (b, 0, 0))
    const = lambda shape: pl.BlockSpec(shape, lambda b, i: (0, 0))
    return pl.pallas_call(
        _outproj_kernel,
        grid=(B, nt),
        in_specs=[tok(ya.shape[2]), tok(yb.shape[2]), tok(yc.shape[2]), tok(D), const((D, D)),
                  per_b, per_b, per_b, const((1, D)), const((D, LANES)), const((1, LANES))],
        out_specs=[tok(D), pl.BlockSpec((D // LANES, tm, LANES), lambda b, i: (0, b * nt + i, 0)),
                   pl.BlockSpec((tm, LANES), lambda b, i: (b * nt + i, 0))],
        out_shape=[jax.ShapeDtypeStruct((B, S, D), F32),
                   jax.ShapeDtypeStruct((D // LANES, B * S, LANES), F32),
                   jax.ShapeDtypeStruct((B * S, LANES), F32)],
        compiler_params=_cparams(("parallel", "parallel")),
        name="out_projection",
    )(ya, yb, yc, x, wo_bf, ga, sc, sh, g, wr, br)


def _rank_kernel(rt_ref, before_ref, ltri_ref, pos_ref, info_ref, tinfo_ref, cnt_ref, off_ref, carry_ref):
    ph, i = pl.program_id(0), pl.program_id(1)
    ne = N_EXPERTS
    rt_t = rt_ref[...].T
    tm = rt_t.shape[1]
    expert = lax.broadcasted_iota(jnp.int32, (ne, tm), 0).astype(F32)
    hit1 = expert == rt_t[RT_E:RT_E + 1, :]
    hit2 = expert == rt_t[RT_E + 1:RT_E + 2, :]
    onehot = jnp.where(hit1, 1.0, jnp.where(hit2, 1.0, 0.0))
    n_blk = jnp.ceil(jnp.sum(onehot, axis=1, keepdims=True) * (1.0 / ROW_ALIGN)) * ROW_ALIGN
    n_blk = jnp.broadcast_to(n_blk, (ne, LANES))

    @pl.when((ph == 0) & (i == 0))
    def _():
        cnt_ref[...] = jnp.zeros_like(cnt_ref)

    @pl.when(ph == 0)
    def _():
        cnt_ref[...] += n_blk

    @pl.when((ph == 1) & (i == 0))
    def _():
        ntile = jnp.ceil(cnt_ref[...] * (1.0 / MOE_TILE))
        off_tiles = _dot(ltri_ref[...], ntile.astype(BF16))
        off_ref[...] = off_tiles * MOE_TILE
        info_ref[0] = cnt_ref[...]
        info_ref[1] = off_tiles
        carry_ref[...] = jnp.zeros_like(carry_ref)

    @pl.when(ph == 1)
    def _():
        dst0 = carry_ref[...] + off_ref[...]
        loc0 = _dot(ltri_ref[...], (n_blk * (1.0 / ROW_ALIGN)).astype(BF16)) * ROW_ALIGN
        within = _dot(onehot.astype(BF16), before_ref[...])
        in_sorted = within + dst0[:, 0:1]
        in_buffer = within + loc0[:, 0:1]
        rows = [jnp.sum(jnp.where(hit, v, 0.0), axis=0, keepdims=True)
                for v in (in_sorted, in_buffer) for hit in (hit1, hit2)]
        rows.append(jnp.zeros((pos_ref.shape[1] - len(rows), tm), F32))
        pos_ref[0] = jnp.concatenate(rows, axis=0).astype(jnp.int32)
        tinfo_ref[0, 0] = dst0
        tinfo_ref[0, 1] = n_blk
        tinfo_ref[0, 2] = loc0
        carry_ref[...] += n_blk


def _rank(rt):
    T = rt.shape[0]
    tm = min(DISPATCH_TILE, T)
    idx = np.arange(tm)
    before = jnp.asarray(idx[:, None] < idx[None, :], BF16)
    ex = np.arange(N_EXPERTS)
    ltri = jnp.asarray(ex[None, :] < ex[:, None], BF16)
    stat = pltpu.VMEM((N_EXPERTS, LANES), F32)
    return pl.pallas_call(
        _rank_kernel,
        grid=(2, T // tm),
        in_specs=[pl.BlockSpec((tm, LANES), lambda ph, i: (i, 0)),
                  pl.BlockSpec((tm, tm), lambda ph, i: (0, 0)),
                  pl.BlockSpec((N_EXPERTS, N_EXPERTS), lambda ph, i: (0, 0))],
        out_specs=[pl.BlockSpec((1, 8, tm), lambda ph, i: (i * ph, 0, 0)),
                   pl.BlockSpec((2, N_EXPERTS, LANES), lambda ph, i: (0, 0, 0)),
                   pl.BlockSpec((1, 3, N_EXPERTS, LANES), lambda ph, i: (i * ph, 0, 0, 0))],
        out_shape=[jax.ShapeDtypeStruct((T // tm, 8, tm), jnp.int32),
                   jax.ShapeDtypeStruct((2, N_EXPERTS, LANES), F32),
                   jax.ShapeDtypeStruct((T // tm, 3, N_EXPERTS, LANES), F32)],
        scratch_shapes=[stat, stat, stat],
        compiler_params=_cparams(("arbitrary", "arbitrary")),
        name="moe_rank",
    )(rt, before, ltri)


def _block_copies(n, src_of, dst_of, sem, act):
    def whole(j, _):
        off = pl.multiple_of(j * COPY_CHUNK, COPY_CHUNK)
        act(pltpu.make_async_copy(src_of(off, COPY_CHUNK), dst_of(off, COPY_CHUNK), sem))
        return 0
    lax.fori_loop(0, n // COPY_CHUNK, whole, 0)
    for b in range(COPY_CHUNK.bit_length() - 2, ROW_ALIGN.bit_length() - 2, -1):
        size = 1 << b

        @pl.when((n >> b) & 1 == 1)
        def _():
            off = pl.multiple_of((n >> (b + 1)) << (b + 1), ROW_ALIGN)
            act(pltpu.make_async_copy(src_of(off, size), dst_of(off, size), sem))


def _dispatch_kernel(dst_ref, nblk_ref, loc_ref, tail0_ref, tailn_ref, nv_ref, h_ref, lpos_ref, xs_ref,
                     obuf, zbuf, sems, zsem):
    i = pl.program_id(0)
    n = pl.num_programs(0)
    nc, tm = h_ref.shape[0], h_ref.shape[1]
    R = obuf.shape[2]
    slot = i % 2

    def tile_copies(tile, sl, act):
        def body(e, _):
            c = tile * N_EXPERTS + e
            loc = pl.multiple_of(loc_ref[c], ROW_ALIGN)
            dst = pl.multiple_of(dst_ref[c], ROW_ALIGN)
            _block_copies(nblk_ref[c],
                          lambda off, size: obuf.at[sl, :, pl.ds(loc + off, size)],
                          lambda off, size: xs_ref.at[:, pl.ds(dst + off, size)],
                          sems.at[sl], act)
            return 0
        lax.fori_loop(0, N_EXPERTS, body, 0)

    @pl.when(i >= 2)
    def _():
        tile_copies(i - 2, slot, lambda cp: cp.wait())

    hb = jnp.concatenate([h_ref[s] for s in range(nc)], axis=1).astype(BF16)
    lp = lpos_ref[0].astype(F32)
    l1, l2 = lp[2:3, :], lp[3:4, :]
    blk = MXU_DIM
    for jb in range(R // blk):
        r = (lax.broadcasted_iota(jnp.int32, (blk, tm), 0) + jb * blk).astype(F32)
        sel = jnp.where(r == l1, 1.0, jnp.where(r == l2, 1.0, 0.0)).astype(BF16)
        rows = _dot(sel, hb)
        for s in range(nc):
            obuf[slot, s, jb * blk:(jb + 1) * blk, :] = rows[:, s * LANES:(s + 1) * LANES]
    tile_copies(i, slot, lambda cp: cp.start())

    @pl.when(i == n - 1)
    def _():
        @pl.when(i >= 1)
        def _():
            tile_copies(i - 1, 1 - slot, lambda cp: cp.wait())
        tile_copies(i, slot, lambda cp: cp.wait())
        zbuf[...] = jnp.zeros_like(zbuf)

        def zero_fill(act):
            def body(e, _):
                t0 = pl.multiple_of(tail0_ref[e], ROW_ALIGN)
                _block_copies(tailn_ref[e], lambda off, size: zbuf.at[:, pl.ds(0, size)],
                              lambda off, size: xs_ref.at[:, pl.ds(t0 + off, size)], zsem, act)
                return 0
            lax.fori_loop(0, N_EXPERTS, body, 0)

            def unused(j, _):
                j0 = pl.multiple_of(j * MOE_TILE, MOE_TILE)
                act(pltpu.make_async_copy(zbuf, xs_ref.at[:, pl.ds(j0, MOE_TILE)], zsem))
                return 0
            lax.fori_loop(nv_ref[0], xs_ref.shape[1] // MOE_TILE, unused, 0)
        zero_fill(lambda cp: cp.start())
        zero_fill(lambda cp: cp.wait())


def _dispatch(dst0, nblk, loc0, tail0, tailn, n_valid, h3, pos, n_rows):
    NC, T, _ = h3.shape
    tm = min(DISPATCH_TILE, T)
    R = 2 * tm + N_EXPERTS * ROW_ALIGN
    R = -(-R // MXU_DIM) * MXU_DIM
    return pl.pallas_call(
        _dispatch_kernel,
        grid_spec=pltpu.PrefetchScalarGridSpec(
            num_scalar_prefetch=6,
            grid=(T // tm,),
            in_specs=[pl.BlockSpec((NC, tm, LANES), lambda i, *_: (0, i, 0)),
                      pl.BlockSpec((1, pos.shape[1], tm), lambda i, *_: (i, 0, 0))],
            out_specs=pl.BlockSpec(memory_space=pl.ANY),
            scratch_shapes=[pltpu.VMEM((2, NC, R, LANES), F32), pltpu.VMEM((NC, MOE_TILE, LANES), F32),
                            pltpu.SemaphoreType.DMA((2,)), pltpu.SemaphoreType.DMA]),
        out_shape=jax.ShapeDtypeStruct((NC, n_rows, LANES), F32),
        compiler_params=_cparams(("arbitrary",)),
        name="moe_dispatch",
    )(dst0, nblk, loc0, tail0, tailn, n_valid, h3, pos)


def _experts_kernel(te_ref, nv_ref, xs_ref, w1_ref, w3_ref, w2_ref, ys_ref, w1b, w3b, w2b):
    j = pl.program_id(0)
    e = te_ref[j]
    e_prev = te_ref[jnp.maximum(j - 1, 0)]

    @pl.when((j == 0) | (e != e_prev))
    def _():
        w1b[...] = w1_ref[0].astype(BF16)
        w3b[...] = w3_ref[0].astype(BF16)
        w2b[...] = w2_ref[0].astype(BF16)

    @pl.when(j < nv_ref[0])
    def _():
        nc = xs_ref.shape[0]
        h = jnp.concatenate([xs_ref[s] for s in range(nc)], axis=1).astype(BF16)
        a = _dot(h, w1b[...])
        b = _dot(h, w3b[...])
        hid = (a * (1.0 / (1.0 + jnp.exp(-a))) * b).astype(BF16)
        y = _dot(hid, w2b[...])
        for s in range(nc):
            ys_ref[s] = y[:, s * LANES:(s + 1) * LANES]

    @pl.when(j >= nv_ref[0])
    def _():
        ys_ref[...] = jnp.zeros_like(ys_ref)


def _experts(tile_expert, n_valid, xs, w1, w3, w2):
    NC, P, _ = xs.shape
    NE, D, F = w1.shape
    nt = P // MOE_TILE
    tile = lambda j, te, nv: (0, jnp.minimum(j, nv[0] - 1), 0)
    wspec = lambda shape: pl.BlockSpec(shape, lambda j, te, nv: (te[j], 0, 0))
    return pl.pallas_call(
        _experts_kernel,
        grid_spec=pltpu.PrefetchScalarGridSpec(
            num_scalar_prefetch=2,
            grid=(nt,),
            in_specs=[pl.BlockSpec((NC, MOE_TILE, LANES), tile),
                      wspec((1, D, F)), wspec((1, D, F)), wspec((1, F, D))],
            out_specs=pl.BlockSpec((NC, MOE_TILE, LANES), lambda j, te, nv: (0, j, 0)),
            scratch_shapes=[pltpu.VMEM((D, F), BF16), pltpu.VMEM((D, F), BF16),
                            pltpu.VMEM((F, D), BF16)]),
        out_shape=jax.ShapeDtypeStruct((NC, P, LANES), F32),
        compiler_params=_cparams(("arbitrary",)),
        name="moe_experts",
    )(tile_expert, n_valid, xs, w1, w3, w2)


def _combine_kernel(pos_ref, x_ref, rt_ref, ga_ref, ys_ref, o_ref, buf, sems):
    i = pl.program_id(0)
    n = pl.num_programs(0)
    tm = x_ref.shape[0]
    nc = buf.shape[1]

    def row_copy(tile, slot, t, k):
        src = ys_ref.at[:, pos_ref[tile * (2 * tm) + 2 * t + k]]
        return pltpu.make_async_copy(src, buf.at[slot, :, k * tm + t], sems.at[slot])

    def issue(tile, slot):
        def body(t, _):
            row_copy(tile, slot, t, 0).start(priority=0)
            row_copy(tile, slot, t, 1).start(priority=1)
            return 0
        lax.fori_loop(0, tm, body, 0, unroll=8)

    @pl.when(i == 0)
    def _():
        issue(0, 0)

    @pl.when(i + 1 < n)
    def _():
        issue(i + 1, (i + 1) % 2)

    slot = i % 2

    def drain(t, _):
        row_copy(i, slot, t, 0).wait()
        row_copy(i, slot, t, 1).wait()
        return 0

    lax.fori_loop(0, tm, drain, 0, unroll=8)
    rt = rt_ref[...]
    w1 = rt[:, RT_W:RT_W + 1]
    w2 = rt[:, RT_W + 1:RT_W + 2]
    for s in range(nc):
        cols = slice(s * LANES, (s + 1) * LANES)
        y = w1 * buf[slot, s, 0:tm, :] + w2 * buf[slot, s, tm:2 * tm, :]
        o_ref[:, cols] = x_ref[:, cols] + ga_ref[0][:, cols] * y


def _combine(pos_flat, x1, rt, ga, ys, tm=256):
    B, S, D = x1.shape
    NC = ys.shape[0]
    tm = min(tm, S)
    nt = S // tm
    return pl.pallas_call(
        _combine_kernel,
        grid_spec=pltpu.PrefetchScalarGridSpec(
            num_scalar_prefetch=1,
            grid=(B * nt,),
            in_specs=[pl.BlockSpec((tm, D), lambda i, pos: (i, 0)),
                      pl.BlockSpec((tm, LANES), lambda i, pos: (i, 0)),
                      pl.BlockSpec((1, 1, D), lambda i, pos: (i // nt, 0, 0)),
                      pl.BlockSpec(memory_space=pl.ANY)],
            out_specs=pl.BlockSpec((tm, D), lambda i, pos: (i, 0)),
            scratch_shapes=[pltpu.VMEM((2, NC, 2 * tm, LANES), F32),
                            pltpu.SemaphoreType.DMA((2,))]),
        out_shape=jax.ShapeDtypeStruct((B * S, D), F32),
        compiler_params=_cparams(("arbitrary",)),
        name="moe_combine",
    )(pos_flat, x1.reshape(B * S, D), rt, ga, ys).reshape(B, S, D)


def _moe(h3, rt, x1, ga, w1, w3, w2, first_expert):
    T = rt.shape[0]
    n_tok_tiles = T // min(DISPATCH_TILE, T)
    n_tiles = -(-(2 * T + n_tok_tiles * N_EXPERTS * ROW_ALIGN) // MOE_TILE) + N_EXPERTS
    pos, info, tinfo = _rank(rt)
    pos_flat = pos[:, :2, :].transpose(0, 2, 1).reshape(-1)
    counts = info[0, :, 0]
    starts = info[1, :, 0]
    ends = starts + jnp.ceil(counts * (1.0 / MOE_TILE))
    tile_ids = jnp.arange(n_tiles, dtype=F32)
    tile_expert = jnp.minimum(jnp.sum(tile_ids[:, None] >= ends[None, :], axis=1), N_EXPERTS - 1)
    n_valid = ends[N_EXPERTS - 1:].astype(jnp.int32)
    as_ints = lambda a: a.astype(jnp.int32).reshape(-1)
    xs = _dispatch(as_ints(tinfo[:, 0, :, 0]), as_ints(tinfo[:, 1, :, 0]),
                   as_ints(tinfo[:, 2, :, 0]), as_ints(starts * MOE_TILE + counts),
                   as_ints(ends * MOE_TILE - starts * MOE_TILE - counts), n_valid, h3, pos,
                   n_tiles * MOE_TILE)
    ys = _experts(tile_expert.astype(jnp.int32) + first_expert, n_valid, xs, w1, w3, w2)
    return _combine(pos_flat, x1, rt, ga, ys)


def kernel(x, c, w_mod, b_mod, g_norm1, w_in, gq_a, gk_a, lam_a, g_sub_a, w_pool, b_pool, pool_scale, gq_c, gk_c, w_out, g_norm2, w_rg, b_rg, w_re, b_re, w1, w3, w2):
    B, S, D = x.shape
    L = w_mod.shape[0]
    a_width = D // 2
    pool_width = D // 4
    c_width = D // 4
    W = MXU_DIM
    reps = W // HEAD_DIM

    inv = 1.0 / (ROPE_THETA ** (jnp.arange(0, HEAD_DIM, 2, dtype=F32) / HEAD_DIM))
    ang = jnp.arange(S, dtype=F32)[:, None] * inv[None, :]
    ang = jnp.concatenate([ang, ang], axis=-1)
    cos_h, sin_h = jnp.cos(ang), jnp.sin(ang)
    first = jnp.arange(HEAD_DIM) < HEAD_DIM // 2
    cos_t = jnp.tile(cos_h, (1, reps))
    sa_t = jnp.tile(jnp.where(first[None, :], 0.0, sin_h), (1, reps))
    sb_t = jnp.tile(jnp.where(first[None, :], -sin_h, 0.0), (1, reps))
    head_of = np.arange(W) // HEAD_DIM
    bd = jnp.asarray((head_of[:, None] == head_of[None, :]) / HEAD_DIM, BF16)

    qa0, ka0, va0 = 0, a_width, 2 * a_width
    ub0 = 3 * a_width
    qc0, kc0, vc0 = ub0 + pool_width, ub0 + pool_width + c_width, ub0 + pool_width + 2 * c_width
    chunks = lambda lo_, hi_: list(range(lo_ // W, hi_ // W))
    z_chunks = tuple([(j, True) for j in chunks(qa0, va0)] + [(j, False) for j in chunks(ub0, qc0)]
                     + [(j, True) for j in chunks(qc0, vc0)])
    v_chunks = tuple(chunks(va0, ub0) + chunks(vc0, w_in.shape[2]))
    zq_a, zk_a = 0, a_width // LANES
    z_ub = 2 * a_width // pool_width
    zq_c = (2 * a_width + pool_width) // LANES
    zk_c = zq_c + c_width // LANES
    scale = HEAD_DIM ** -0.5 * math.log2(math.e)
    bias_t = _dilated_tables(S)

    mod = _modulation(c, w_mod, b_mod)
    for l in range(L):
        sh1, sc1, ga1, sh2, sc2, ga2 = [m[:, None, :] for m in jnp.split(mod[l], N_MOD, axis=-1)]
        gain = jnp.ones((w_in.shape[2],), F32)
        gain = gain.at[qa0:ka0].set(jnp.tile(gq_a[l], a_width // HEAD_DIM) * scale)
        gain = gain.at[ka0:va0].set(jnp.tile(gk_a[l], a_width // HEAD_DIM))
        gain = gain.at[qc0:kc0].set(jnp.tile(gq_c[l], c_width // HEAD_DIM) * scale)
        gain = gain.at[kc0:vc0].set(jnp.tile(gk_c[l], c_width // HEAD_DIM))
        z, vt = _in_projection(x, sc1, sh1, g_norm1[l][None, :], w_in[l].astype(BF16),
                               gain[None, :], cos_t, sa_t, sb_t, bd, z_chunks, v_chunks)

        lam_init = 0.8 - 0.6 * math.exp(-0.3 * l)
        n_a = a_width // LANES
        ya = _attention("diff", z, vt, zq_a, zk_a, 0, n_a,
                        [lam_a[l], g_sub_a[l][:, None]],
                        [pl.BlockSpec(lam_a[l].shape, lambda b, h: (0, 0)),
                         pl.BlockSpec((LANES, 1), lambda b, h: (0, 0))], lam_init=lam_init)
        n_c = c_width // LANES
        yc = _attention("dil", z, vt, zq_c, zk_c, n_a, n_c,
                        [bias_t], [pl.BlockSpec(bias_t.shape, lambda b, h: (0, 0, 0))])
        w_bd = jax.scipy.linalg.block_diag(*[w_pool[l, g] for g in range(w_pool.shape[1])])
        yb = _pool_mixer(z, z_ub, w_bd.astype(BF16), b_pool[l].reshape(1, -1),
                         pool_scale[l][None, :])

        wr = jnp.zeros((D, LANES), F32)
        wr = wr.at[:, :N_GROUPS].set(w_rg[l])
        wr = wr.at[:, N_GROUPS:N_GROUPS + N_EXPERTS].set(
            w_re[l].transpose(1, 0, 2).reshape(D, N_EXPERTS))
        br = jnp.zeros((1, LANES), F32)
        br = br.at[0, :N_GROUPS].set(b_rg[l])
        br = br.at[0, N_GROUPS:N_GROUPS + N_EXPERTS].set(b_re[l].reshape(-1))
        x1, h3, rt = _out_projection(ya, yb, yc, x, w_out[l].astype(BF16), ga1, sc2, sh2,
                                     g_norm2[l][None, :], wr.astype(BF16), br)
        F = w1.shape[-1]
        x = _moe(h3, rt, x1, ga2, w1.reshape(L * N_EXPERTS, D, F), w3.reshape(L * N_EXPERTS, D, F),
                 w2.reshape(L * N_EXPERTS, F, D), l * N_EXPERTS)
    return x
```

```python
import functools
import math

import jax
import jax.numpy as jnp
import numpy as np
from jax import lax
from jax.experimental import pallas as pl
from jax.experimental.pallas import tpu as pltpu

HEAD_DIM = 64
POOL_WINDOWS = (2, 4, 8, 16)
DILATED_PAIRS = ((128, 1), (512, 4), (2048, 16))
ROPE_THETA = 10000.0
N_GROUPS = 4
EXPERTS_PER_GROUP = 8
N_EXPERTS = N_GROUPS * EXPERTS_PER_GROUP
N_MOD = 6
EPS = 1e-6

LANES = 128
BF16_ROWS = 16
MXU_DIM = 256
KV_BLOCK = 256
Q_BLOCK = 2 * KV_BLOCK
MOE_TILE = 256
DISPATCH_TILE = 512
ROW_ALIGN = 8
COPY_CHUNK = 64
RT_E, RT_W = 0, 2
VMEM_LIMIT = 48 * 1024 * 1024
NEG = -1e30
F32 = jnp.float32
BF16 = jnp.bfloat16


def _cparams(sem):
    return pltpu.CompilerParams(dimension_semantics=sem, vmem_limit_bytes=VMEM_LIMIT)


def _dot(a, b):
    return jnp.dot(a, b, preferred_element_type=F32)


def _dot_nt(a, b):
    return lax.dot_general(a, b, (((1,), (1,)), ((), ())), preferred_element_type=F32)


def _mod_kernel(c_ref, w_ref, b_ref, o_ref):
    c = c_ref[...]
    cond = c * (1.0 / (1.0 + jnp.exp(-c)))
    o_ref[0] = _dot(cond.astype(BF16), w_ref[0].astype(BF16)) + b_ref[0]


def _modulation(c, w_mod, b_mod):
    L, D, N = w_mod.shape
    B = c.shape[0]
    tn = 1024
    return pl.pallas_call(
        _mod_kernel,
        grid=(L, N // tn),
        in_specs=[pl.BlockSpec((B, D), lambda l, j: (0, 0)),
                  pl.BlockSpec((1, D, tn), lambda l, j: (l, 0, j)),
                  pl.BlockSpec((1, 1, tn), lambda l, j: (l, 0, j))],
        out_specs=pl.BlockSpec((1, B, tn), lambda l, j: (l, 0, j)),
        out_shape=jax.ShapeDtypeStruct((L, B, N), F32),
        compiler_params=_cparams(("parallel", "parallel")),
        name="modulation",
    )(c, w_mod, b_mod.reshape(L, 1, N))


def _inproj_kernel(z_chunks, v_chunks, x_ref, sc_ref, sh_ref, g_ref, w_ref, gain_ref, cos_ref,
                   sa_ref, sb_ref, bd_ref, z_ref, vt_ref):
    x = x_ref[0]
    ms = jnp.mean(x * x, axis=-1, keepdims=True)
    h = x * lax.rsqrt(ms + EPS) * g_ref[...]
    h = h * (1.0 + sc_ref[0]) + sh_ref[0]
    hb = h.astype(BF16)
    W = MXU_DIM
    for dst, (src, normed) in enumerate(z_chunks):
        zc = _dot(hb, w_ref[:, src * W:(src + 1) * W])
        if normed:
            msq = _dot((zc * zc).astype(BF16), bd_ref[...])
            y = zc * lax.rsqrt(msq + EPS) * gain_ref[:, src * W:(src + 1) * W]
            r_up = pltpu.roll(y, HEAD_DIM // 2, 1)
            r_dn = pltpu.roll(y, W - HEAD_DIM // 2, 1)
            zc = y * cos_ref[...] + r_up * sa_ref[...] + r_dn * sb_ref[...]
        z_ref[0, :, dst * W:(dst + 1) * W] = zc.astype(BF16)
    for dst, src in enumerate(v_chunks):
        zt = _dot(hb, w_ref[:, src * W:(src + 1) * W]).T.astype(BF16)
        for cb in range(vt_ref.shape[1]):
            vt_ref[0, cb, dst * W:(dst + 1) * W, :] = zt[:, cb * KV_BLOCK:(cb + 1) * KV_BLOCK]


def _in_projection(x, sc, sh, g, w_bf, gain, cos_t, sa_t, sb_t, bd, z_chunks, v_chunks, tm=512):
    B, S, D = x.shape
    N = w_bf.shape[1]
    W = MXU_DIM
    nz, nv = len(z_chunks) * W, len(v_chunks) * W
    return pl.pallas_call(
        functools.partial(_inproj_kernel, z_chunks, v_chunks),
        grid=(B, S // tm),
        in_specs=[pl.BlockSpec((1, tm, D), lambda b, i: (b, i, 0)),
                  pl.BlockSpec((1, 1, D), lambda b, i: (b, 0, 0)),
                  pl.BlockSpec((1, 1, D), lambda b, i: (b, 0, 0)),
                  pl.BlockSpec((1, D), lambda b, i: (0, 0)),
                  pl.BlockSpec((D, N), lambda b, i: (0, 0)),
                  pl.BlockSpec((1, N), lambda b, i: (0, 0)),
                  pl.BlockSpec((tm, W), lambda b, i: (i, 0)),
                  pl.BlockSpec((tm, W), lambda b, i: (i, 0)),
                  pl.BlockSpec((tm, W), lambda b, i: (i, 0)),
                  pl.BlockSpec((W, W), lambda b, i: (0, 0))],
        out_specs=[pl.BlockSpec((1, tm, nz), lambda b, i: (b, i, 0)),
                   pl.BlockSpec((1, tm // KV_BLOCK, nv, KV_BLOCK), lambda b, i: (b, i, 0, 0))],
        out_shape=[jax.ShapeDtypeStruct((B, S, nz), BF16),
                   jax.ShapeDtypeStruct((B, S // KV_BLOCK, nv, KV_BLOCK), BF16)],
        compiler_params=_cparams(("parallel", "parallel")),
        name="in_projection",
    )(x, sc, sh, g, w_bf, gain, cos_t, sa_t, sb_t, bd)


def _attn_kernel(mode, lam_init, *refs):
    if mode == "diff":
        q_ref, k_ref, vt_ref, lam_ref, g_ref, o_ref, s_a, s_b, acc_ref = refs
    else:
        q_ref, k_ref, vt_ref, bias_ref, o_ref, s_a, s_b, acc_ref = refs
    bq, bk = Q_BLOCK, KV_BLOCK
    S = q_ref.shape[1]
    nq = S // bq
    lane = lax.broadcasted_iota(jnp.int32, (1, LANES), 1)
    lo = lane < HEAD_DIM
    extra = acc_ref.shape[0] - LANES
    ones_rows = jnp.where(lax.broadcasted_iota(jnp.int32, (extra, bk), 0) == 0, 1.0, 0.0).astype(BF16)

    def both_maps(qi):
        q = q_ref[0, pl.ds(pl.multiple_of(qi * bq, bq), bq), :]
        zero = jnp.zeros_like(q)
        return jnp.concatenate([jnp.where(lo, q, zero), jnp.where(lo, zero, q)], axis=0)

    def scores(s_ref, kb, q2):
        ks = pl.multiple_of(kb * bk, bk)
        s_ref[...] = _dot_nt(k_ref[0, pl.ds(ks, bk), :], q2)

    def q_block(qi, _):
        qs = pl.multiple_of(qi * bq, bq)
        q2 = both_maps(qi)

        def table(kb):
            if mode == "diff":
                return None
            return bias_ref[jnp.minimum(2 * qi - kb + 1, bias_ref.shape[0] - 1)]

        def update(s_ref, kb, stats, bias, first=0):
            vt1 = jnp.concatenate([vt_ref[0, kb], ones_rows], axis=0)
            out = []
            for half in range(2):
                cols = slice(half * bq + first, (half + 1) * bq)
                m = stats[half]
                s = s_ref[:, cols]
                if bias is not None:
                    s = s + bias[:, first:]
                mn = jnp.maximum(m[:, first:], jnp.max(s, axis=0, keepdims=True))
                p = jnp.exp2(s - mn)
                acc_ref[:, cols] = (jnp.exp2(m[:, first:] - mn) * acc_ref[:, cols]
                                    + _dot(vt1, p.astype(BF16)))
                out.append(jnp.concatenate([m[:, :first], mn], axis=1) if first else mn)
            return tuple(out)

        def pair(j, stats):
            scores(s_b, 2 * j + 1, q2)
            stats = update(s_a, 2 * j, stats, table(2 * j))
            scores(s_a, 2 * j + 2, q2)
            return update(s_b, 2 * j + 1, stats, table(2 * j + 1))

        acc_ref[...] = jnp.zeros_like(acc_ref)
        stats = lax.fori_loop(0, qi, pair, (jnp.full((1, bq), NEG, F32),) * 2)

        krow = lax.broadcasted_iota(jnp.int32, (bk, bq), 0)
        qcol = lax.broadcasted_iota(jnp.int32, (bk, bq), 1)
        scores(s_b, 2 * qi + 1, q2)
        for rel, s_ref in enumerate((s_a, s_b)):
            bias = table(2 * qi + rel)
            if mode == "diff":
                bias = jnp.where(krow + rel * bk <= qcol, 0.0, NEG).astype(F32)
            stats = update(s_ref, 2 * qi + rel, stats, bias, first=rel * bk)
            if rel == 0:
                scores(s_a, 0, both_maps(jnp.minimum(qi + 1, nq - 1)))
        a_lo, a_hi = acc_ref[0:LANES, :bq], acc_ref[0:LANES, bq:]
        l_lo, l_hi = acc_ref[LANES:LANES + 1, :bq], acc_ref[LANES:LANES + 1, bq:]
        if mode == "diff":
            lp = lam_ref[...]
            lam = (jnp.exp(jnp.sum(lp[0:1] * lp[1:2], axis=-1, keepdims=True))
                   - jnp.exp(jnp.sum(lp[2:3] * lp[3:4], axis=-1, keepdims=True)) + lam_init)
            y = a_lo * (1.0 / l_lo) - a_hi * (lam / l_hi)
            ms = jnp.mean(y * y, axis=0, keepdims=True)
            y = y * lax.rsqrt(ms + EPS) * (g_ref[...] * (1.0 - lam_init))
        else:
            feat = lax.broadcasted_iota(jnp.int32, (LANES, 1), 0)
            y = jnp.where(feat < HEAD_DIM, a_lo * (1.0 / l_lo), a_hi * (1.0 / l_hi))
        o_ref[0, pl.ds(qs, bq), :] = y.T.astype(o_ref.dtype)
        return 0

    scores(s_a, 0, both_maps(0))
    lax.fori_loop(0, nq, q_block, 0)


def _attention(mode, z, vt, q_col, k_col, v_row, n_blocks, extras, extra_specs, lam_init=0.0):
    B, S, _ = z.shape
    zspec = lambda col: pl.BlockSpec((1, S, LANES), lambda b, h: (b, 0, col + h))
    vspec = pl.BlockSpec((1, S // KV_BLOCK, LANES, KV_BLOCK), lambda b, h: (b, 0, v_row + h, 0))
    return pl.pallas_call(
        functools.partial(_attn_kernel, mode, lam_init),
        grid=(B, n_blocks),
        in_specs=[zspec(q_col), zspec(k_col), vspec] + extra_specs,
        out_specs=pl.BlockSpec((1, S, LANES), lambda b, h: (b, 0, h)),
        out_shape=jax.ShapeDtypeStruct((B, S, n_blocks * LANES), BF16),
        scratch_shapes=[pltpu.VMEM((KV_BLOCK, 2 * Q_BLOCK), F32), pltpu.VMEM((KV_BLOCK, 2 * Q_BLOCK), F32),
                        pltpu.VMEM((LANES + BF16_ROWS, 2 * Q_BLOCK), F32)],
        compiler_params=_cparams(("parallel", "parallel")),
        name=mode + "_attention",
    )(z, z, vt, *extras)


def _dilated_tables(S):
    bq, bk = Q_BLOCK, KV_BLOCK
    dds = np.arange(-1, S // bk)
    dist = (dds[:, None, None] * bk + np.arange(bq)[None, None, :] - np.arange(bk)[None, :, None])
    cnt = np.zeros(dist.shape, np.float32)
    for window, dil in DILATED_PAIRS:
        cnt += (dist >= 0) & (dist <= window) & (dist % dil == 0)
    n = len(dds)
    while n > 1 and np.array_equal(cnt[n - 1], cnt[n - 2]):
        n -= 1
    cnt = cnt[:n]
    return jnp.asarray(np.where(cnt > 0, np.log2(np.maximum(cnt, 1.0)), NEG).astype(np.float32))


def _pool_kernel(u_ref, w_ref, b_ref, sc_ref, o_ref):
    u = u_ref[0].astype(F32)
    S, C = u.shape
    row = lax.broadcasted_iota(jnp.int32, (S, C), 0)
    lane = lax.broadcasted_iota(jnp.int32, (S, C), 1)
    grp = lane // (C // len(POOL_WINDOWS))

    def shifted(a, k):
        return jnp.where(row >= k, pltpu.roll(a, k, 0), 0.0)

    acc = u
    win = jnp.zeros_like(u)
    width = 1
    for gi, w in enumerate(POOL_WINDOWS):
        while width < w:
            acc = acc + shifted(acc, width)
            width *= 2
        win = jnp.where(grp == gi, acc, win)
    wl = jnp.zeros_like(row)
    for gi, w in enumerate(POOL_WINDOWS):
        wl = jnp.where(grp == gi, w, wl)
    cnt = jnp.minimum(row + 1, wl).astype(F32)
    d = win / cnt - u
    y = _dot(d.astype(BF16), w_ref[...]) + b_ref[...]
    o_ref[0] = (y * sc_ref[...]).astype(o_ref.dtype)


def _pool_mixer(z, col_block, w_bd, b, scale):
    B, S, _ = z.shape
    C = w_bd.shape[0]
    return pl.pallas_call(
        _pool_kernel,
        grid=(B,),
        in_specs=[pl.BlockSpec((1, S, C), lambda b_: (b_, 0, col_block)),
                  pl.BlockSpec((C, C), lambda b_: (0, 0)),
                  pl.BlockSpec((1, C), lambda b_: (0, 0)),
                  pl.BlockSpec((1, C), lambda b_: (0, 0))],
        out_specs=pl.BlockSpec((1, S, C), lambda b_: (b_, 0, 0)),
        out_shape=jax.ShapeDtypeStruct((B, S, C), BF16),
        compiler_params=_cparams(("parallel",)),
        name="pool_mixer",
    )(z, w_bd, b, scale)


def _outproj_kernel(ya_ref, yb_ref, yc_ref, x_ref, wo_ref, ga_ref, sc_ref, sh_ref, g_ref, wr_ref,
                    br_ref, x1_ref, h_ref, rt_ref):
    na = ya_ref.shape[2]
    nb = yb_ref.shape[2]
    y = (_dot(ya_ref[0], wo_ref[0:na, :]) + _dot(yb_ref[0], wo_ref[na:na + nb, :])
         + _dot(yc_ref[0], wo_ref[na + nb:, :]))
    x1 = x_ref[0] + ga_ref[0] * y
    x1_ref[0] = x1
    ms = jnp.mean(x1 * x1, axis=-1, keepdims=True)
    h = x1 * lax.rsqrt(ms + EPS) * g_ref[...]
    h = h * (1.0 + sc_ref[0]) + sh_ref[0]
    hb = h.astype(BF16)
    for s in range(h_ref.shape[0]):
        h_ref[s] = h[:, s * LANES:(s + 1) * LANES]

    logits = _dot_nt(wr_ref[...], hb) + br_ref[...]
    tm = logits.shape[1]
    G, E = N_GROUPS, EXPERTS_PER_GROUP
    big = float(LANES)
    grow = lax.broadcasted_iota(jnp.int32, (8, tm), 0)
    grow_f = grow.astype(F32)
    gl = jnp.where(grow < G, logits[G * E:G * E + 8, :], -jnp.inf)
    gmax = jnp.max(gl, axis=0, keepdims=True)
    gidx = jnp.min(jnp.where(gl == gmax, grow_f, big), axis=0, keepdims=True)
    gsum = jnp.sum(jnp.where(grow < G, jnp.exp(gl - gmax), 0.0), axis=0, keepdims=True)
    g_w = 1.0 / gsum
    erow = lax.broadcasted_iota(jnp.int32, (G * E, tm), 0)
    erow_f = erow.astype(F32)
    el = jnp.where((erow // E).astype(F32) == gidx, logits[0:G * E, :], -jnp.inf)
    v1 = jnp.max(el, axis=0, keepdims=True)
    i1 = jnp.min(jnp.where(el == v1, erow_f, big), axis=0, keepdims=True)
    el2 = jnp.where(erow_f == i1, -jnp.inf, el)
    v2 = jnp.max(el2, axis=0, keepdims=True)
    i2 = jnp.min(jnp.where(el2 == v2, erow_f, big), axis=0, keepdims=True)
    t = jnp.exp(v2 - v1)
    w1 = g_w / (1.0 + t)
    w2 = w1 * t
    rt_ref[0] = jnp.concatenate([i1, i2, w1, w2, jnp.zeros((rt_ref.shape[1] - 4, tm), F32)], axis=0)


def _out_projection(ya, yb, yc, x, wo_bf, ga, sc, sh, g, wr, br, tm=DISPATCH_TILE):
    B, S, D = x.shape
    nt = S // tm
    tok = lambda n: pl.BlockSpec((1, tm, n), lambda b, i: (b, i, 0))
    per_b = pl.BlockSpec((1, 1, D), lambda b, i: (b, 0, 0))
    const = lambda shape: pl.BlockSpec(shape, lambda b, i: (0, 0))
    return pl.pallas_call(
        _outproj_kernel,
        grid=(B, nt),
        in_specs=[tok(ya.shape[2]), tok(yb.shape[2]), tok(yc.shape[2]), tok(D), const((D, D)),
                  per_b, per_b, per_b, const((1, D)), const((LANES, D)), const((LANES, 1))],
        out_specs=[tok(D), pl.BlockSpec((D // LANES, tm, LANES), lambda b, i: (0, b * nt + i, 0)),
                   pl.BlockSpec((1, 8, tm), lambda b, i: (b * nt + i, 0, 0))],
        out_shape=[jax.ShapeDtypeStruct((B, S, D), F32),
                   jax.ShapeDtypeStruct((D // LANES, B * S, LANES), F32),
                   jax.ShapeDtypeStruct((B * nt, 8, tm), F32)],
        compiler_params=_cparams(("parallel", "parallel")),
        name="out_projection",
    )(ya, yb, yc, x, wo_bf, ga, sc, sh, g, wr, br)


def _rank_kernel(rt_ref, before_ref, ltri_ref, pos_ref, info_ref, tinfo_ref, cnt_ref, off_ref, carry_ref):
    ph, i = pl.program_id(0), pl.program_id(1)
    ne = N_EXPERTS
    rt_t = rt_ref[0]
    tm = rt_t.shape[1]
    expert = lax.broadcasted_iota(jnp.int32, (ne, tm), 0).astype(F32)
    hit1 = expert == rt_t[RT_E:RT_E + 1, :]
    hit2 = expert == rt_t[RT_E + 1:RT_E + 2, :]
    onehot = jnp.where(hit1, 1.0, jnp.where(hit2, 1.0, 0.0))
    n_blk = jnp.ceil(jnp.sum(onehot, axis=1, keepdims=True) * (1.0 / ROW_ALIGN)) * ROW_ALIGN
    n_blk = jnp.broadcast_to(n_blk, (ne, LANES))

    @pl.when((ph == 0) & (i == 0))
    def _():
        cnt_ref[...] = jnp.zeros_like(cnt_ref)

    @pl.when(ph == 0)
    def _():
        cnt_ref[...] += n_blk

    @pl.when((ph == 1) & (i == 0))
    def _():
        ntile = jnp.ceil(cnt_ref[...] * (1.0 / MOE_TILE))
        off_tiles = _dot(ltri_ref[...], ntile.astype(BF16))
        off_ref[...] = off_tiles * MOE_TILE
        info_ref[0] = cnt_ref[...]
        info_ref[1] = off_tiles
        carry_ref[...] = jnp.zeros_like(carry_ref)

    @pl.when(ph == 1)
    def _():
        dst0 = carry_ref[...] + off_ref[...]
        loc0 = _dot(ltri_ref[...], (n_blk * (1.0 / ROW_ALIGN)).astype(BF16)) * ROW_ALIGN
        within = _dot(onehot.astype(BF16), before_ref[...])
        in_sorted = within + dst0[:, 0:1]
        in_buffer = within + loc0[:, 0:1]
        rows = [jnp.sum(jnp.where(hit, v, 0.0), axis=0, keepdims=True)
                for v in (in_sorted, in_buffer) for hit in (hit1, hit2)]
        rows.append(jnp.zeros((pos_ref.shape[1] - len(rows), tm), F32))
        pos_ref[0] = jnp.concatenate(rows, axis=0).astype(jnp.int32)
        tinfo_ref[0, 0] = dst0
        tinfo_ref[0, 1] = n_blk
        tinfo_ref[0, 2] = loc0
        carry_ref[...] += n_blk


def _rank(rt):
    n_tok_tiles, _, tm = rt.shape
    T = n_tok_tiles * tm
    idx = np.arange(tm)
    before = jnp.asarray(idx[:, None] < idx[None, :], BF16)
    ex = np.arange(N_EXPERTS)
    ltri = jnp.asarray(ex[None, :] < ex[:, None], BF16)
    stat = pltpu.VMEM((N_EXPERTS, LANES), F32)
    return pl.pallas_call(
        _rank_kernel,
        grid=(2, T // tm),
        in_specs=[pl.BlockSpec((1, 8, tm), lambda ph, i: (i, 0, 0)),
                  pl.BlockSpec((tm, tm), lambda ph, i: (0, 0)),
                  pl.BlockSpec((N_EXPERTS, N_EXPERTS), lambda ph, i: (0, 0))],
        out_specs=[pl.BlockSpec((1, 8, tm), lambda ph, i: (i * ph, 0, 0)),
                   pl.BlockSpec((2, N_EXPERTS, LANES), lambda ph, i: (0, 0, 0)),
                   pl.BlockSpec((1, 3, N_EXPERTS, LANES), lambda ph, i: (i * ph, 0, 0, 0))],
        out_shape=[jax.ShapeDtypeStruct((T // tm, 8, tm), jnp.int32),
                   jax.ShapeDtypeStruct((2, N_EXPERTS, LANES), F32),
                   jax.ShapeDtypeStruct((T // tm, 3, N_EXPERTS, LANES), F32)],
        scratch_shapes=[stat, stat, stat],
        compiler_params=_cparams(("arbitrary", "arbitrary")),
        name="moe_rank",
    )(rt, before, ltri)


def _block_copies(n, src_of, dst_of, sem, act):
    def whole(j, _):
        off = pl.multiple_of(j * COPY_CHUNK, COPY_CHUNK)
        act(pltpu.make_async_copy(src_of(off, COPY_CHUNK), dst_of(off, COPY_CHUNK), sem))
        return 0
    lax.fori_loop(0, n // COPY_CHUNK, whole, 0)
    for b in range(COPY_CHUNK.bit_length() - 2, ROW_ALIGN.bit_length() - 2, -1):
        size = 1 << b

        @pl.when((n >> b) & 1 == 1)
        def _():
            off = pl.multiple_of((n >> (b + 1)) << (b + 1), ROW_ALIGN)
            act(pltpu.make_async_copy(src_of(off, size), dst_of(off, size), sem))


def _dispatch_kernel(dst_ref, nblk_ref, loc_ref, tail0_ref, tailn_ref, nv_ref, h_ref, lpos_ref, xs_ref,
                     obuf, zbuf, sems, zsem):
    i = pl.program_id(0)
    n = pl.num_programs(0)
    nc, tm = h_ref.shape[0], h_ref.shape[1]
    R = obuf.shape[2]
    slot = i % 2

    def tile_copies(tile, sl, act):
        def body(e, _):
            c = tile * N_EXPERTS + e
            loc = pl.multiple_of(loc_ref[c], ROW_ALIGN)
            dst = pl.multiple_of(dst_ref[c], ROW_ALIGN)
            _block_copies(nblk_ref[c],
                          lambda off, size: obuf.at[sl, :, pl.ds(loc + off, size)],
                          lambda off, size: xs_ref.at[:, pl.ds(dst + off, size)],
                          sems.at[sl], act)
            return 0
        lax.fori_loop(0, N_EXPERTS, body, 0)

    @pl.when(i >= 2)
    def _():
        tile_copies(i - 2, slot, lambda cp: cp.wait())

    hb = jnp.concatenate([h_ref[s] for s in range(nc)], axis=1).astype(BF16)
    lp = lpos_ref[0].astype(F32)
    l1, l2 = lp[2:3, :], lp[3:4, :]
    blk = MXU_DIM
    for jb in range(R // blk):
        r = (lax.broadcasted_iota(jnp.int32, (blk, tm), 0) + jb * blk).astype(F32)
        sel = jnp.where(r == l1, 1.0, jnp.where(r == l2, 1.0, 0.0)).astype(BF16)
        rows = _dot(sel, hb)
        for s in range(nc):
            obuf[slot, s, jb * blk:(jb + 1) * blk, :] = rows[:, s * LANES:(s + 1) * LANES]
    tile_copies(i, slot, lambda cp: cp.start())

    @pl.when(i == n - 1)
    def _():
        @pl.when(i >= 1)
        def _():
            tile_copies(i - 1, 1 - slot, lambda cp: cp.wait())
        tile_copies(i, slot, lambda cp: cp.wait())
        zbuf[...] = jnp.zeros_like(zbuf)

        def zero_fill(act):
            def body(e, _):
                t0 = pl.multiple_of(tail0_ref[e], ROW_ALIGN)
                _block_copies(tailn_ref[e], lambda off, size: zbuf.at[:, pl.ds(0, size)],
                              lambda off, size: xs_ref.at[:, pl.ds(t0 + off, size)], zsem, act)
                return 0
            lax.fori_loop(0, N_EXPERTS, body, 0)

            def unused(j, _):
                j0 = pl.multiple_of(j * MOE_TILE, MOE_TILE)
                act(pltpu.make_async_copy(zbuf, xs_ref.at[:, pl.ds(j0, MOE_TILE)], zsem))
                return 0
            lax.fori_loop(nv_ref[0], xs_ref.shape[1] // MOE_TILE, unused, 0)
        zero_fill(lambda cp: cp.start())
        zero_fill(lambda cp: cp.wait())


def _dispatch(dst0, nblk, loc0, tail0, tailn, n_valid, h3, pos, n_rows):
    NC, T, _ = h3.shape
    tm = min(DISPATCH_TILE, T)
    R = 2 * tm + N_EXPERTS * ROW_ALIGN
    R = -(-R // MXU_DIM) * MXU_DIM
    return pl.pallas_call(
        _dispatch_kernel,
        grid_spec=pltpu.PrefetchScalarGridSpec(
            num_scalar_prefetch=6,
            grid=(T // tm,),
            in_specs=[pl.BlockSpec((NC, tm, LANES), lambda i, *_: (0, i, 0)),
                      pl.BlockSpec((1, pos.shape[1], tm), lambda i, *_: (i, 0, 0))],
            out_specs=pl.BlockSpec(memory_space=pl.ANY),
            scratch_shapes=[pltpu.VMEM((2, NC, R, LANES), F32), pltpu.VMEM((NC, MOE_TILE, LANES), F32),
                            pltpu.SemaphoreType.DMA((2,)), pltpu.SemaphoreType.DMA]),
        out_shape=jax.ShapeDtypeStruct((NC, n_rows, LANES), F32),
        compiler_params=_cparams(("arbitrary",)),
        name="moe_dispatch",
    )(dst0, nblk, loc0, tail0, tailn, n_valid, h3, pos)


def _experts_kernel(te_ref, nv_ref, xs_ref, w1_ref, w3_ref, w2_ref, ys_ref, w1b, w3b, w2b):
    j = pl.program_id(0)
    e = te_ref[j]
    e_prev = te_ref[jnp.maximum(j - 1, 0)]

    @pl.when((j == 0) | (e != e_prev))
    def _():
        w1b[...] = w1_ref[0].astype(BF16)
        w3b[...] = w3_ref[0].astype(BF16)
        w2b[...] = w2_ref[0].astype(BF16)

    @pl.when(j < nv_ref[0])
    def _():
        nc = xs_ref.shape[0]
        h = jnp.concatenate([xs_ref[s] for s in range(nc)], axis=1).astype(BF16)
        a = _dot(h, w1b[...])
        b = _dot(h, w3b[...])
        hid = (a * (1.0 / (1.0 + jnp.exp(-a))) * b).astype(BF16)
        y = _dot(hid, w2b[...])
        for s in range(nc):
            ys_ref[s] = y[:, s * LANES:(s + 1) * LANES]


def _experts(tile_expert, n_valid, xs, w1, w3, w2):
    NC, P, _ = xs.shape
    NE, D, F = w1.shape
    nt = P // MOE_TILE
    tile = lambda j, te, nv: (0, jnp.minimum(j, nv[0] - 1), 0)
    wspec = lambda shape: pl.BlockSpec(shape, lambda j, te, nv: (te[j], 0, 0))
    return pl.pallas_call(
        _experts_kernel,
        grid_spec=pltpu.PrefetchScalarGridSpec(
            num_scalar_prefetch=2,
            grid=(nt,),
            in_specs=[pl.BlockSpec((NC, MOE_TILE, LANES), tile),
                      wspec((1, D, F)), wspec((1, D, F)), wspec((1, F, D))],
            out_specs=pl.BlockSpec((NC, MOE_TILE, LANES), tile),
            scratch_shapes=[pltpu.VMEM((D, F), BF16), pltpu.VMEM((D, F), BF16),
                            pltpu.VMEM((F, D), BF16)]),
        out_shape=jax.ShapeDtypeStruct((NC, P, LANES), F32),
        input_output_aliases={2: 0},
        compiler_params=_cparams(("arbitrary",)),
        name="moe_experts",
    )(tile_expert, n_valid, xs, w1, w3, w2)


def _combine_kernel(pos_ref, x_ref, rt_ref, ga_ref, ys_ref, o_ref, buf, sems):
    i = pl.program_id(0)
    n = pl.num_programs(0)
    tm = x_ref.shape[0]
    nc = buf.shape[1]

    def row_copy(tile, slot, t, k):
        src = ys_ref.at[:, pos_ref[tile * (2 * tm) + 2 * t + k]]
        return pltpu.make_async_copy(src, buf.at[slot, :, k * tm + t], sems.at[slot])

    def issue(tile, slot):
        def body(t, _):
            row_copy(tile, slot, t, 0).start(priority=0)
            row_copy(tile, slot, t, 1).start(priority=1)
            return 0
        lax.fori_loop(0, tm, body, 0, unroll=8)

    @pl.when(i == 0)
    def _():
        issue(0, 0)

    @pl.when(i + 1 < n)
    def _():
        issue(i + 1, (i + 1) % 2)

    slot = i % 2

    def drain(t, _):
        row_copy(i, slot, t, 0).wait()
        row_copy(i, slot, t, 1).wait()
        return 0

    lax.fori_loop(0, tm, drain, 0, unroll=8)
    rt_t = rt_ref[0]
    rt = jnp.concatenate([rt_t, jnp.zeros((LANES - rt_t.shape[0], tm), F32)], axis=0).T
    w1 = rt[:, RT_W:RT_W + 1]
    w2 = rt[:, RT_W + 1:RT_W + 2]
    for s in range(nc):
        cols = slice(s * LANES, (s + 1) * LANES)
        y = w1 * buf[slot, s, 0:tm, :] + w2 * buf[slot, s, tm:2 * tm, :]
        o_ref[:, cols] = x_ref[:, cols] + ga_ref[0][:, cols] * y


def _combine(pos_flat, x1, rt, ga, ys, tm=256):
    B, S, D = x1.shape
    NC = ys.shape[0]
    tm = min(tm, S)
    nt = S // tm
    per_rt = rt.shape[2] // tm
    return pl.pallas_call(
        _combine_kernel,
        grid_spec=pltpu.PrefetchScalarGridSpec(
            num_scalar_prefetch=1,
            grid=(B * nt,),
            in_specs=[pl.BlockSpec((tm, D), lambda i, pos: (i, 0)),
                      pl.BlockSpec((1, rt.shape[1], tm), lambda i, pos: (i // per_rt, 0, i % per_rt)),
                      pl.BlockSpec((1, 1, D), lambda i, pos: (i // nt, 0, 0)),
                      pl.BlockSpec(memory_space=pl.ANY)],
            out_specs=pl.BlockSpec((tm, D), lambda i, pos: (i, 0)),
            scratch_shapes=[pltpu.VMEM((2, NC, 2 * tm, LANES), F32),
                            pltpu.SemaphoreType.DMA((2,))]),
        out_shape=jax.ShapeDtypeStruct((B * S, D), F32),
        compiler_params=_cparams(("arbitrary",)),
        name="moe_combine",
    )(pos_flat, x1.reshape(B * S, D), rt, ga, ys).reshape(B, S, D)


def _moe(h3, rt, x1, ga, w1, w3, w2, first_expert):
    n_tok_tiles = rt.shape[0]
    T = n_tok_tiles * rt.shape[2]
    n_tiles = -(-(2 * T + n_tok_tiles * N_EXPERTS * ROW_ALIGN) // MOE_TILE) + N_EXPERTS
    pos, info, tinfo = _rank(rt)
    pos_flat = pos[:, :2, :].transpose(0, 2, 1).reshape(-1)
    counts = info[0, :, 0]
    starts = info[1, :, 0]
    ends = starts + jnp.ceil(counts * (1.0 / MOE_TILE))
    tile_ids = jnp.arange(n_tiles, dtype=F32)
    tile_expert = jnp.minimum(jnp.sum(tile_ids[:, None] >= ends[None, :], axis=1), N_EXPERTS - 1)
    n_valid = ends[N_EXPERTS - 1:].astype(jnp.int32)
    as_ints = lambda a: a.astype(jnp.int32).reshape(-1)
    xs = _dispatch(as_ints(tinfo[:, 0, :, 0]), as_ints(tinfo[:, 1, :, 0]),
                   as_ints(tinfo[:, 2, :, 0]), as_ints(starts * MOE_TILE + counts),
                   as_ints(ends * MOE_TILE - starts * MOE_TILE - counts), n_valid, h3, pos,
                   n_tiles * MOE_TILE)
    ys = _experts(tile_expert.astype(jnp.int32) + first_expert, n_valid, xs, w1, w3, w2)
    return _combine(pos_flat, x1, rt, ga, ys)


def kernel(x, c, w_mod, b_mod, g_norm1, w_in, gq_a, gk_a, lam_a, g_sub_a, w_pool, b_pool, pool_scale, gq_c, gk_c, w_out, g_norm2, w_rg, b_rg, w_re, b_re, w1, w3, w2):
    B, S, D = x.shape
    L = w_mod.shape[0]
    a_width = D // 2
    pool_width = D // 4
    c_width = D // 4
    W = MXU_DIM
    reps = W // HEAD_DIM

    inv = 1.0 / (ROPE_THETA ** (jnp.arange(0, HEAD_DIM, 2, dtype=F32) / HEAD_DIM))
    ang = jnp.arange(S, dtype=F32)[:, None] * inv[None, :]
    ang = jnp.concatenate([ang, ang], axis=-1)
    cos_h, sin_h = jnp.cos(ang), jnp.sin(ang)
    first = jnp.arange(HEAD_DIM) < HEAD_DIM // 2
    cos_t = jnp.tile(cos_h, (1, reps))
    sa_t = jnp.tile(jnp.where(first[None, :], 0.0, sin_h), (1, reps))
    sb_t = jnp.tile(jnp.where(first[None, :], -sin_h, 0.0), (1, reps))
    head_of = np.arange(W) // HEAD_DIM
    bd = jnp.asarray((head_of[:, None] == head_of[None, :]) / HEAD_DIM, BF16)

    qa0, ka0, va0 = 0, a_width, 2 * a_width
    ub0 = 3 * a_width
    qc0, kc0, vc0 = ub0 + pool_width, ub0 + pool_width + c_width, ub0 + pool_width + 2 * c_width
    chunks = lambda lo_, hi_: list(range(lo_ // W, hi_ // W))
    z_chunks = tuple([(j, True) for j in chunks(qa0, va0)] + [(j, False) for j in chunks(ub0, qc0)]
                     + [(j, True) for j in chunks(qc0, vc0)])
    v_chunks = tuple(chunks(va0, ub0) + chunks(vc0, w_in.shape[2]))
    zq_a, zk_a = 0, a_width // LANES
    z_ub = 2 * a_width // pool_width
    zq_c = (2 * a_width + pool_width) // LANES
    zk_c = zq_c + c_width // LANES
    scale = HEAD_DIM ** -0.5 * math.log2(math.e)
    bias_t = _dilated_tables(S)

    mod = _modulation(c, w_mod, b_mod)
    for l in range(L):
        sh1, sc1, ga1, sh2, sc2, ga2 = [m[:, None, :] for m in jnp.split(mod[l], N_MOD, axis=-1)]
        gain = jnp.ones((w_in.shape[2],), F32)
        gain = gain.at[qa0:ka0].set(jnp.tile(gq_a[l], a_width // HEAD_DIM) * scale)
        gain = gain.at[ka0:va0].set(jnp.tile(gk_a[l], a_width // HEAD_DIM))
        gain = gain.at[qc0:kc0].set(jnp.tile(gq_c[l], c_width // HEAD_DIM) * scale)
        gain = gain.at[kc0:vc0].set(jnp.tile(gk_c[l], c_width // HEAD_DIM))
        z, vt = _in_projection(x, sc1, sh1, g_norm1[l][None, :], w_in[l].astype(BF16),
                               gain[None, :], cos_t, sa_t, sb_t, bd, z_chunks, v_chunks)

        lam_init = 0.8 - 0.6 * math.exp(-0.3 * l)
        n_a = a_width // LANES
        ya = _attention("diff", z, vt, zq_a, zk_a, 0, n_a,
                        [lam_a[l], g_sub_a[l][:, None]],
                        [pl.BlockSpec(lam_a[l].shape, lambda b, h: (0, 0)),
                         pl.BlockSpec((LANES, 1), lambda b, h: (0, 0))], lam_init=lam_init)
        n_c = c_width // LANES
        yc = _attention("dil", z, vt, zq_c, zk_c, n_a, n_c,
                        [bias_t], [pl.BlockSpec(bias_t.shape, lambda b, h: (0, 0, 0))])
        w_bd = jax.scipy.linalg.block_diag(*[w_pool[l, g] for g in range(w_pool.shape[1])])
        yb = _pool_mixer(z, z_ub, w_bd.astype(BF16), b_pool[l].reshape(1, -1),
                         pool_scale[l][None, :])

        wr = jnp.zeros((LANES, D), F32)
        wr = wr.at[:N_EXPERTS].set(w_re[l].transpose(0, 2, 1).reshape(N_EXPERTS, D))
        wr = wr.at[N_EXPERTS:N_EXPERTS + N_GROUPS].set(w_rg[l].T)
        br = jnp.zeros((LANES, 1), F32)
        br = br.at[:N_EXPERTS, 0].set(b_re[l].reshape(-1))
        br = br.at[N_EXPERTS:N_EXPERTS + N_GROUPS, 0].set(b_rg[l])
        x1, h3, rt = _out_projection(ya, yb, yc, x, w_out[l].astype(BF16), ga1, sc2, sh2,
                                     g_norm2[l][None, :], wr.astype(BF16), br)
        F = w1.shape[-1]
        x = _moe(h3, rt, x1, ga2, w1.reshape(L * N_EXPERTS, D, F), w3.reshape(L * N_EXPERTS, D, F),
                 w2.reshape(L * N_EXPERTS, F, D), l * N_EXPERTS)
    return x
```

```python
import functools
import math

import jax
import jax.numpy as jnp
import numpy as np
from jax import lax
from jax.experimental import pallas as pl
from jax.experimental.pallas import tpu as pltpu

HEAD_DIM = 64
POOL_WINDOWS = (2, 4, 8, 16)
DILATED_PAIRS = ((128, 1), (512, 4), (2048, 16))
ROPE_THETA = 10000.0
N_GROUPS = 4
EXPERTS_PER_GROUP = 8
N_EXPERTS = N_GROUPS * EXPERTS_PER_GROUP
N_MOD = 6
EPS = 1e-6

LANES = 128
BF16_ROWS = 16
MXU_DIM = 256
KV_BLOCK = 256
Q_BLOCK = 2 * KV_BLOCK
MOE_TILE = 256
DISPATCH_TILE = 512
ROW_ALIGN = 8
COPY_CHUNK = 64
RT_E, RT_W = 0, 2
VMEM_LIMIT = 48 * 1024 * 1024
NEG = -1e30
F32 = jnp.float32
BF16 = jnp.bfloat16


def _cparams(sem):
    return pltpu.CompilerParams(dimension_semantics=sem, vmem_limit_bytes=VMEM_LIMIT)


def _dot(a, b):
    return jnp.dot(a, b, preferred_element_type=F32)


def _dot_nt(a, b):
    return lax.dot_general(a, b, (((1,), (1,)), ((), ())), preferred_element_type=F32)


def _mod_kernel(c_ref, w_ref, b_ref, o_ref):
    c = c_ref[...]
    cond = c * (1.0 / (1.0 + jnp.exp(-c)))
    o_ref[0] = _dot(cond.astype(BF16), w_ref[0].astype(BF16)) + b_ref[0]


def _modulation(c, w_mod, b_mod):
    L, D, N = w_mod.shape
    B = c.shape[0]
    tn = 1024
    return pl.pallas_call(
        _mod_kernel,
        grid=(L, N // tn),
        in_specs=[pl.BlockSpec((B, D), lambda l, j: (0, 0)),
                  pl.BlockSpec((1, D, tn), lambda l, j: (l, 0, j)),
                  pl.BlockSpec((1, 1, tn), lambda l, j: (l, 0, j))],
        out_specs=pl.BlockSpec((1, B, tn), lambda l, j: (l, 0, j)),
        out_shape=jax.ShapeDtypeStruct((L, B, N), F32),
        compiler_params=_cparams(("parallel", "parallel")),
        name="modulation",
    )(c, w_mod, b_mod.reshape(L, 1, N))


def _inproj_kernel(z_chunks, v_chunks, x_ref, sc_ref, sh_ref, g_ref, w_ref, gain_ref, cos_ref,
                   sa_ref, sb_ref, bd_ref, z_ref, vt_ref):
    x = x_ref[0]
    ms = jnp.mean(x * x, axis=-1, keepdims=True)
    h = x * lax.rsqrt(ms + EPS) * g_ref[...]
    h = h * (1.0 + sc_ref[0]) + sh_ref[0]
    hb = h.astype(BF16)
    W = MXU_DIM
    for dst, (src, normed) in enumerate(z_chunks):
        zc = _dot(hb, w_ref[:, src * W:(src + 1) * W])
        if normed:
            msq = _dot((zc * zc).astype(BF16), bd_ref[...])
            y = zc * lax.rsqrt(msq + EPS) * gain_ref[:, src * W:(src + 1) * W]
            r_up = pltpu.roll(y, HEAD_DIM // 2, 1)
            r_dn = pltpu.roll(y, W - HEAD_DIM // 2, 1)
            zc = y * cos_ref[...] + r_up * sa_ref[...] + r_dn * sb_ref[...]
        z_ref[0, :, dst * W:(dst + 1) * W] = zc.astype(BF16)
    for dst, src in enumerate(v_chunks):
        zt = _dot(hb, w_ref[:, src * W:(src + 1) * W]).T.astype(BF16)
        for cb in range(vt_ref.shape[1]):
            vt_ref[0, cb, dst * W:(dst + 1) * W, :] = zt[:, cb * KV_BLOCK:(cb + 1) * KV_BLOCK]


def _in_projection(x, sc, sh, g, w_bf, gain, cos_t, sa_t, sb_t, bd, z_chunks, v_chunks, tm=512):
    B, S, D = x.shape
    N = w_bf.shape[1]
    W = MXU_DIM
    nz, nv = len(z_chunks) * W, len(v_chunks) * W
    return pl.pallas_call(
        functools.partial(_inproj_kernel, z_chunks, v_chunks),
        grid=(B, S // tm),
        in_specs=[pl.BlockSpec((1, tm, D), lambda b, i: (b, i, 0)),
                  pl.BlockSpec((1, 1, D), lambda b, i: (b, 0, 0)),
                  pl.BlockSpec((1, 1, D), lambda b, i: (b, 0, 0)),
                  pl.BlockSpec((1, D), lambda b, i: (0, 0)),
                  pl.BlockSpec((D, N), lambda b, i: (0, 0)),
                  pl.BlockSpec((1, N), lambda b, i: (0, 0)),
                  pl.BlockSpec((tm, W), lambda b, i: (i, 0)),
                  pl.BlockSpec((tm, W), lambda b, i: (i, 0)),
                  pl.BlockSpec((tm, W), lambda b, i: (i, 0)),
                  pl.BlockSpec((W, W), lambda b, i: (0, 0))],
        out_specs=[pl.BlockSpec((1, tm, nz), lambda b, i: (b, i, 0)),
                   pl.BlockSpec((1, tm // KV_BLOCK, nv, KV_BLOCK), lambda b, i: (b, i, 0, 0))],
        out_shape=[jax.ShapeDtypeStruct((B, S, nz), BF16),
                   jax.ShapeDtypeStruct((B, S // KV_BLOCK, nv, KV_BLOCK), BF16)],
        compiler_params=_cparams(("parallel", "parallel")),
        name="in_projection",
    )(x, sc, sh, g, w_bf, gain, cos_t, sa_t, sb_t, bd)


def _attn_kernel(mode, lam_init, *refs):
    if mode == "diff":
        q_ref, k_ref, vt_ref, lam_ref, g_ref, o_ref, s_a, s_b, acc_ref = refs
    else:
        q_ref, k_ref, vt_ref, bias_ref, o_ref, s_a, s_b, acc_ref = refs
    bq, bk = Q_BLOCK, KV_BLOCK
    S = q_ref.shape[1]
    nq = S // bq
    lane = lax.broadcasted_iota(jnp.int32, (1, LANES), 1)
    lo = lane < HEAD_DIM
    extra = acc_ref.shape[0] - LANES
    ones_rows = jnp.where(lax.broadcasted_iota(jnp.int32, (extra, bk), 0) == 0, 1.0, 0.0).astype(BF16)

    def both_maps(qi):
        q = q_ref[0, pl.ds(pl.multiple_of(qi * bq, bq), bq), :]
        zero = jnp.zeros_like(q)
        return jnp.concatenate([jnp.where(lo, q, zero), jnp.where(lo, zero, q)], axis=0)

    def scores(s_ref, kb, q2):
        ks = pl.multiple_of(kb * bk, bk)
        s_ref[...] = _dot_nt(k_ref[0, pl.ds(ks, bk), :], q2)

    def q_block(qi, _):
        qs = pl.multiple_of(qi * bq, bq)
        q2 = both_maps(qi)

        def table(kb):
            if mode == "diff":
                return None
            return bias_ref[jnp.minimum(2 * qi - kb + 1, bias_ref.shape[0] - 1)]

        def update(s_ref, kb, stats, bias, first=0):
            vt1 = jnp.concatenate([vt_ref[0, kb], ones_rows], axis=0)
            out = []
            for half in range(2):
                cols = slice(half * bq + first, (half + 1) * bq)
                m = stats[half]
                s = s_ref[:, cols]
                if bias is not None:
                    s = s + bias[:, first:]
                mn = jnp.maximum(m[:, first:], jnp.max(s, axis=0, keepdims=True))
                p = jnp.exp2(s - mn)
                acc_ref[:, cols] = (jnp.exp2(m[:, first:] - mn) * acc_ref[:, cols]
                                    + _dot(vt1, p.astype(BF16)))
                out.append(jnp.concatenate([m[:, :first], mn], axis=1) if first else mn)
            return tuple(out)

        def pair(j, stats):
            scores(s_b, 2 * j + 1, q2)
            stats = update(s_a, 2 * j, stats, table(2 * j))
            scores(s_a, 2 * j + 2, q2)
            return update(s_b, 2 * j + 1, stats, table(2 * j + 1))

        acc_ref[...] = jnp.zeros_like(acc_ref)
        stats = lax.fori_loop(0, qi, pair, (jnp.full((1, bq), NEG, F32),) * 2)

        krow = lax.broadcasted_iota(jnp.int32, (bk, bq), 0)
        qcol = lax.broadcasted_iota(jnp.int32, (bk, bq), 1)
        scores(s_b, 2 * qi + 1, q2)
        for rel, s_ref in enumerate((s_a, s_b)):
            bias = table(2 * qi + rel)
            if mode == "diff":
                bias = jnp.where(krow + rel * bk <= qcol, 0.0, NEG).astype(F32)
            stats = update(s_ref, 2 * qi + rel, stats, bias, first=rel * bk)
            if rel == 0:
                scores(s_a, 0, both_maps(jnp.minimum(qi + 1, nq - 1)))
        a_lo, a_hi = acc_ref[0:LANES, :bq], acc_ref[0:LANES, bq:]
        l_lo, l_hi = acc_ref[LANES:LANES + 1, :bq], acc_ref[LANES:LANES + 1, bq:]
        if mode == "diff":
            lp = lam_ref[...]
            lam = (jnp.exp(jnp.sum(lp[0:1] * lp[1:2], axis=-1, keepdims=True))
                   - jnp.exp(jnp.sum(lp[2:3] * lp[3:4], axis=-1, keepdims=True)) + lam_init)
            y = a_lo * (1.0 / l_lo) - a_hi * (lam / l_hi)
            ms = jnp.mean(y * y, axis=0, keepdims=True)
            y = y * lax.rsqrt(ms + EPS) * (g_ref[...] * (1.0 - lam_init))
        else:
            feat = lax.broadcasted_iota(jnp.int32, (LANES, 1), 0)
            y = jnp.where(feat < HEAD_DIM, a_lo * (1.0 / l_lo), a_hi * (1.0 / l_hi))
        o_ref[0, pl.ds(qs, bq), :] = y.T.astype(o_ref.dtype)
        return 0

    scores(s_a, 0, both_maps(0))
    lax.fori_loop(0, nq, q_block, 0)


def _attention(mode, z, vt, q_col, k_col, v_row, n_blocks, extras, extra_specs, lam_init=0.0):
    B, S, _ = z.shape
    zspec = lambda col: pl.BlockSpec((1, S, LANES), lambda b, h: (b, 0, col + h))
    vspec = pl.BlockSpec((1, S // KV_BLOCK, LANES, KV_BLOCK), lambda b, h: (b, 0, v_row + h, 0))
    return pl.pallas_call(
        functools.partial(_attn_kernel, mode, lam_init),
        grid=(B, n_blocks),
        in_specs=[zspec(q_col), zspec(k_col), vspec] + extra_specs,
        out_specs=pl.BlockSpec((1, S, LANES), lambda b, h: (b, 0, h)),
        out_shape=jax.ShapeDtypeStruct((B, S, n_blocks * LANES), BF16),
        scratch_shapes=[pltpu.VMEM((KV_BLOCK, 2 * Q_BLOCK), F32), pltpu.VMEM((KV_BLOCK, 2 * Q_BLOCK), F32),
                        pltpu.VMEM((LANES + BF16_ROWS, 2 * Q_BLOCK), F32)],
        compiler_params=_cparams(("parallel", "parallel")),
        name=mode + "_attention",
    )(z, z, vt, *extras)


def _dilated_tables(S):
    bq, bk = Q_BLOCK, KV_BLOCK
    dds = np.arange(-1, S // bk)
    dist = (dds[:, None, None] * bk + np.arange(bq)[None, None, :] - np.arange(bk)[None, :, None])
    cnt = np.zeros(dist.shape, np.float32)
    for window, dil in DILATED_PAIRS:
        cnt += (dist >= 0) & (dist <= window) & (dist % dil == 0)
    n = len(dds)
    while n > 1 and np.array_equal(cnt[n - 1], cnt[n - 2]):
        n -= 1
    cnt = cnt[:n]
    return jnp.asarray(np.where(cnt > 0, np.log2(np.maximum(cnt, 1.0)), NEG).astype(np.float32))


def _pool_kernel(u_ref, w_ref, b_ref, sc_ref, o_ref):
    u = u_ref[0].astype(F32)
    S, C = u.shape
    row = lax.broadcasted_iota(jnp.int32, (S, C), 0)
    lane = lax.broadcasted_iota(jnp.int32, (S, C), 1)
    grp = lane // (C // len(POOL_WINDOWS))

    def shifted(a, k):
        return jnp.where(row >= k, pltpu.roll(a, k, 0), 0.0)

    acc = u
    win = jnp.zeros_like(u)
    width = 1
    for gi, w in enumerate(POOL_WINDOWS):
        while width < w:
            acc = acc + shifted(acc, width)
            width *= 2
        win = jnp.where(grp == gi, acc, win)
    wl = jnp.zeros_like(row)
    for gi, w in enumerate(POOL_WINDOWS):
        wl = jnp.where(grp == gi, w, wl)
    cnt = jnp.minimum(row + 1, wl).astype(F32)
    d = win / cnt - u
    y = _dot(d.astype(BF16), w_ref[...]) + b_ref[...]
    o_ref[0] = (y * sc_ref[...]).astype(o_ref.dtype)


def _pool_mixer(z, col_block, w_bd, b, scale):
    B, S, _ = z.shape
    C = w_bd.shape[0]
    return pl.pallas_call(
        _pool_kernel,
        grid=(B,),
        in_specs=[pl.BlockSpec((1, S, C), lambda b_: (b_, 0, col_block)),
                  pl.BlockSpec((C, C), lambda b_: (0, 0)),
                  pl.BlockSpec((1, C), lambda b_: (0, 0)),
                  pl.BlockSpec((1, C), lambda b_: (0, 0))],
        out_specs=pl.BlockSpec((1, S, C), lambda b_: (b_, 0, 0)),
        out_shape=jax.ShapeDtypeStruct((B, S, C), BF16),
        compiler_params=_cparams(("parallel",)),
        name="pool_mixer",
    )(z, w_bd, b, scale)


def _outproj_kernel(ya_ref, yb_ref, yc_ref, x_ref, wo_ref, ga_ref, sc_ref, sh_ref, g_ref, wr_ref,
                    br_ref, x1_ref, h_ref, rt_ref):
    na = ya_ref.shape[2]
    nb = yb_ref.shape[2]
    y = (_dot(ya_ref[0], wo_ref[0:na, :]) + _dot(yb_ref[0], wo_ref[na:na + nb, :])
         + _dot(yc_ref[0], wo_ref[na + nb:, :]))
    x1 = x_ref[0] + ga_ref[0] * y
    x1_ref[0] = x1
    ms = jnp.mean(x1 * x1, axis=-1, keepdims=True)
    h = x1 * lax.rsqrt(ms + EPS) * g_ref[...]
    h = h * (1.0 + sc_ref[0]) + sh_ref[0]
    hb = h.astype(BF16)
    h_ref[0] = hb

    logits = _dot_nt(wr_ref[...], hb) + br_ref[...]
    tm = logits.shape[1]
    G, E = N_GROUPS, EXPERTS_PER_GROUP
    big = float(LANES)
    grow = lax.broadcasted_iota(jnp.int32, (8, tm), 0)
    grow_f = grow.astype(F32)
    gl = jnp.where(grow < G, logits[G * E:G * E + 8, :], -jnp.inf)
    gmax = jnp.max(gl, axis=0, keepdims=True)
    gidx = jnp.min(jnp.where(gl == gmax, grow_f, big), axis=0, keepdims=True)
    gsum = jnp.sum(jnp.where(grow < G, jnp.exp(gl - gmax), 0.0), axis=0, keepdims=True)
    g_w = 1.0 / gsum
    erow = lax.broadcasted_iota(jnp.int32, (G * E, tm), 0)
    erow_f = erow.astype(F32)
    el = jnp.where((erow // E).astype(F32) == gidx, logits[0:G * E, :], -jnp.inf)
    v1 = jnp.max(el, axis=0, keepdims=True)
    i1 = jnp.min(jnp.where(el == v1, erow_f, big), axis=0, keepdims=True)
    el2 = jnp.where(erow_f == i1, -jnp.inf, el)
    v2 = jnp.max(el2, axis=0, keepdims=True)
    i2 = jnp.min(jnp.where(el2 == v2, erow_f, big), axis=0, keepdims=True)
    t = jnp.exp(v2 - v1)
    w1 = g_w / (1.0 + t)
    w2 = w1 * t
    rt_ref[0] = jnp.concatenate([i1, i2, w1, w2, jnp.zeros((rt_ref.shape[1] - 4, tm), F32)], axis=0)


def _out_projection(ya, yb, yc, x, wo_bf, ga, sc, sh, g, wr, br, tm=DISPATCH_TILE):
    B, S, D = x.shape
    nt = S // tm
    tok = lambda n: pl.BlockSpec((1, tm, n), lambda b, i: (b, i, 0))
    per_b = pl.BlockSpec((1, 1, D), lambda b, i: (b, 0, 0))
    const = lambda shape: pl.BlockSpec(shape, lambda b, i: (0, 0))
    return pl.pallas_call(
        _outproj_kernel,
        grid=(B, nt),
        in_specs=[tok(ya.shape[2]), tok(yb.shape[2]), tok(yc.shape[2]), tok(D), const((D, D)),
                  per_b, per_b, per_b, const((1, D)), const((LANES, D)), const((LANES, 1))],
        out_specs=[tok(D), tok(D),
                   pl.BlockSpec((1, 8, tm), lambda b, i: (b * nt + i, 0, 0))],
        out_shape=[jax.ShapeDtypeStruct((B, S, D), F32),
                   jax.ShapeDtypeStruct((B, S, D), BF16),
                   jax.ShapeDtypeStruct((B * nt, 8, tm), F32)],
        compiler_params=_cparams(("parallel", "parallel")),
        name="out_projection",
    )(ya, yb, yc, x, wo_bf, ga, sc, sh, g, wr, br)


def _rank_kernel(rt_ref, before_ref, ltri_ref, pos_ref, info_ref, tinfo_ref, cnt_ref, off_ref, carry_ref):
    ph, i = pl.program_id(0), pl.program_id(1)
    ne = N_EXPERTS
    rt_t = rt_ref[0]
    tm = rt_t.shape[1]
    expert = lax.broadcasted_iota(jnp.int32, (ne, tm), 0).astype(F32)
    hit1 = expert == rt_t[RT_E:RT_E + 1, :]
    hit2 = expert == rt_t[RT_E + 1:RT_E + 2, :]
    onehot = jnp.where(hit1, 1.0, jnp.where(hit2, 1.0, 0.0))
    n_blk = jnp.ceil(jnp.sum(onehot, axis=1, keepdims=True) * (1.0 / ROW_ALIGN)) * ROW_ALIGN
    n_blk = jnp.broadcast_to(n_blk, (ne, LANES))

    @pl.when((ph == 0) & (i == 0))
    def _():
        cnt_ref[...] = jnp.zeros_like(cnt_ref)

    @pl.when(ph == 0)
    def _():
        cnt_ref[...] += n_blk

    @pl.when((ph == 1) & (i == 0))
    def _():
        ntile = jnp.ceil(cnt_ref[...] * (1.0 / MOE_TILE))
        off_tiles = _dot(ltri_ref[...], ntile.astype(BF16))
        off_ref[...] = off_tiles * MOE_TILE
        info_ref[0] = cnt_ref[...]
        info_ref[1] = off_tiles
        carry_ref[...] = jnp.zeros_like(carry_ref)

    @pl.when(ph == 1)
    def _():
        dst0 = carry_ref[...] + off_ref[...]
        loc0 = _dot(ltri_ref[...], (n_blk * (1.0 / ROW_ALIGN)).astype(BF16)) * ROW_ALIGN
        within = _dot(onehot.astype(BF16), before_ref[...])
        in_sorted = within + dst0[:, 0:1]
        in_buffer = within + loc0[:, 0:1]
        rows = [jnp.sum(jnp.where(hit, v, 0.0), axis=0, keepdims=True)
                for v in (in_sorted, in_buffer) for hit in (hit1, hit2)]
        rows.append(jnp.zeros((pos_ref.shape[1] - len(rows), tm), F32))
        pos_ref[0] = jnp.concatenate(rows, axis=0).astype(jnp.int32)
        tinfo_ref[0, 0] = dst0
        tinfo_ref[0, 1] = n_blk
        tinfo_ref[0, 2] = loc0
        carry_ref[...] += n_blk


def _rank(rt):
    n_tok_tiles, _, tm = rt.shape
    T = n_tok_tiles * tm
    idx = np.arange(tm)
    before = jnp.asarray(idx[:, None] < idx[None, :], BF16)
    ex = np.arange(N_EXPERTS)
    ltri = jnp.asarray(ex[None, :] < ex[:, None], BF16)
    stat = pltpu.VMEM((N_EXPERTS, LANES), F32)
    return pl.pallas_call(
        _rank_kernel,
        grid=(2, T // tm),
        in_specs=[pl.BlockSpec((1, 8, tm), lambda ph, i: (i, 0, 0)),
                  pl.BlockSpec((tm, tm), lambda ph, i: (0, 0)),
                  pl.BlockSpec((N_EXPERTS, N_EXPERTS), lambda ph, i: (0, 0))],
        out_specs=[pl.BlockSpec((1, 8, tm), lambda ph, i: (i * ph, 0, 0)),
                   pl.BlockSpec((2, N_EXPERTS, LANES), lambda ph, i: (0, 0, 0)),
                   pl.BlockSpec((1, 3, N_EXPERTS, LANES), lambda ph, i: (i * ph, 0, 0, 0))],
        out_shape=[jax.ShapeDtypeStruct((T // tm, 8, tm), jnp.int32),
                   jax.ShapeDtypeStruct((2, N_EXPERTS, LANES), F32),
                   jax.ShapeDtypeStruct((T // tm, 3, N_EXPERTS, LANES), F32)],
        scratch_shapes=[stat, stat, stat],
        compiler_params=_cparams(("arbitrary", "arbitrary")),
        name="moe_rank",
    )(rt, before, ltri)


def _block_copies(n, src_of, dst_of, sem, act):
    def whole(j, _):
        off = pl.multiple_of(j * COPY_CHUNK, COPY_CHUNK)
        act(pltpu.make_async_copy(src_of(off, COPY_CHUNK), dst_of(off, COPY_CHUNK), sem))
        return 0
    lax.fori_loop(0, n // COPY_CHUNK, whole, 0)
    for b in range(COPY_CHUNK.bit_length() - 2, ROW_ALIGN.bit_length() - 2, -1):
        size = 1 << b

        @pl.when((n >> b) & 1 == 1)
        def _():
            off = pl.multiple_of((n >> (b + 1)) << (b + 1), ROW_ALIGN)
            act(pltpu.make_async_copy(src_of(off, size), dst_of(off, size), sem))


def _dispatch_kernel(dst_ref, nblk_ref, loc_ref, tail0_ref, tailn_ref, nv_ref, h_ref, lpos_ref, xs_ref,
                     obuf, zbuf, sems, zsem):
    i = pl.program_id(0)
    n = pl.num_programs(0)
    nc, tm = obuf.shape[1], h_ref.shape[0]
    R = obuf.shape[2]
    slot = i % 2

    def tile_copies(tile, sl, act):
        def body(e, _):
            c = tile * N_EXPERTS + e
            loc = pl.multiple_of(loc_ref[c], ROW_ALIGN)
            dst = pl.multiple_of(dst_ref[c], ROW_ALIGN)
            _block_copies(nblk_ref[c],
                          lambda off, size: obuf.at[sl, :, pl.ds(loc + off, size)],
                          lambda off, size: xs_ref.at[:, pl.ds(dst + off, size)],
                          sems.at[sl], act)
            return 0
        lax.fori_loop(0, N_EXPERTS, body, 0)

    @pl.when(i >= 2)
    def _():
        tile_copies(i - 2, slot, lambda cp: cp.wait())

    hb = h_ref[...]
    lp = lpos_ref[0].astype(F32)
    l1, l2 = lp[2:3, :], lp[3:4, :]
    blk = MXU_DIM
    for jb in range(R // blk):
        r = (lax.broadcasted_iota(jnp.int32, (blk, tm), 0) + jb * blk).astype(F32)
        sel = jnp.where(r == l1, 1.0, jnp.where(r == l2, 1.0, 0.0)).astype(BF16)
        rows = _dot(sel, hb)
        for s in range(nc):
            obuf[slot, s, jb * blk:(jb + 1) * blk, :] = rows[:, s * LANES:(s + 1) * LANES]
    tile_copies(i, slot, lambda cp: cp.start())

    @pl.when(i == n - 1)
    def _():
        @pl.when(i >= 1)
        def _():
            tile_copies(i - 1, 1 - slot, lambda cp: cp.wait())
        tile_copies(i, slot, lambda cp: cp.wait())
        zbuf[...] = jnp.zeros_like(zbuf)

        def zero_fill(act):
            def body(e, _):
                t0 = pl.multiple_of(tail0_ref[e], ROW_ALIGN)
                _block_copies(tailn_ref[e], lambda off, size: zbuf.at[:, pl.ds(0, size)],
                              lambda off, size: xs_ref.at[:, pl.ds(t0 + off, size)], zsem, act)
                return 0
            lax.fori_loop(0, N_EXPERTS, body, 0)

            def unused(j, _):
                j0 = pl.multiple_of(j * MOE_TILE, MOE_TILE)
                act(pltpu.make_async_copy(zbuf, xs_ref.at[:, pl.ds(j0, MOE_TILE)], zsem))
                return 0
            lax.fori_loop(nv_ref[0], xs_ref.shape[1] // MOE_TILE, unused, 0)
        zero_fill(lambda cp: cp.start())
        zero_fill(lambda cp: cp.wait())


def _dispatch(dst0, nblk, loc0, tail0, tailn, n_valid, h, pos, n_rows):
    T, D = h.shape
    NC = D // LANES
    tm = pos.shape[2]
    R = 2 * tm + N_EXPERTS * ROW_ALIGN
    R = -(-R // MXU_DIM) * MXU_DIM
    return pl.pallas_call(
        _dispatch_kernel,
        grid_spec=pltpu.PrefetchScalarGridSpec(
            num_scalar_prefetch=6,
            grid=(T // tm,),
            in_specs=[pl.BlockSpec((tm, D), lambda i, *_: (i, 0)),
                      pl.BlockSpec((1, pos.shape[1], tm), lambda i, *_: (i, 0, 0))],
            out_specs=pl.BlockSpec(memory_space=pl.ANY),
            scratch_shapes=[pltpu.VMEM((2, NC, R, LANES), F32), pltpu.VMEM((NC, MOE_TILE, LANES), F32),
                            pltpu.SemaphoreType.DMA((2,)), pltpu.SemaphoreType.DMA]),
        out_shape=jax.ShapeDtypeStruct((NC, n_rows, LANES), F32),
        compiler_params=_cparams(("arbitrary",)),
        name="moe_dispatch",
    )(dst0, nblk, loc0, tail0, tailn, n_valid, h, pos)


def _experts_kernel(te_ref, nv_ref, xs_ref, w1_ref, w3_ref, w2_ref, ys_ref, w1b, w3b, w2b):
    j = pl.program_id(0)
    e = te_ref[j]
    e_prev = te_ref[jnp.maximum(j - 1, 0)]

    @pl.when((j == 0) | (e != e_prev))
    def _():
        w1b[...] = w1_ref[0].astype(BF16)
        w3b[...] = w3_ref[0].astype(BF16)
        w2b[...] = w2_ref[0].astype(BF16)

    @pl.when(j < nv_ref[0])
    def _():
        nc = xs_ref.shape[0]
        h = jnp.concatenate([xs_ref[s] for s in range(nc)], axis=1).astype(BF16)
        a = _dot(h, w1b[...])
        b = _dot(h, w3b[...])
        hid = (a * (1.0 / (1.0 + jnp.exp(-a))) * b).astype(BF16)
        y = _dot(hid, w2b[...])
        for s in range(nc):
            ys_ref[s] = y[:, s * LANES:(s + 1) * LANES]


def _experts(tile_expert, n_valid, xs, w1, w3, w2):
    NC, P, _ = xs.shape
    NE, D, F = w1.shape
    nt = P // MOE_TILE
    tile = lambda j, te, nv: (0, jnp.minimum(j, nv[0] - 1), 0)
    wspec = lambda shape: pl.BlockSpec(shape, lambda j, te, nv: (te[j], 0, 0))
    return pl.pallas_call(
        _experts_kernel,
        grid_spec=pltpu.PrefetchScalarGridSpec(
            num_scalar_prefetch=2,
            grid=(nt,),
            in_specs=[pl.BlockSpec((NC, MOE_TILE, LANES), tile),
                      wspec((1, D, F)), wspec((1, D, F)), wspec((1, F, D))],
            out_specs=pl.BlockSpec((NC, MOE_TILE, LANES), tile),
            scratch_shapes=[pltpu.VMEM((D, F), BF16), pltpu.VMEM((D, F), BF16),
                            pltpu.VMEM((F, D), BF16)]),
        out_shape=jax.ShapeDtypeStruct((NC, P, LANES), F32),
        input_output_aliases={2: 0},
        compiler_params=_cparams(("arbitrary",)),
        name="moe_experts",
    )(tile_expert, n_valid, xs, w1, w3, w2)


def _combine_kernel(pos_ref, x_ref, rt_ref, ga_ref, ys_ref, o_ref, buf, sems):
    i = pl.program_id(0)
    n = pl.num_programs(0)
    tm = x_ref.shape[0]
    nc = buf.shape[1]

    def row_copy(tile, slot, t, k):
        src = ys_ref.at[:, pos_ref[tile * (2 * tm) + 2 * t + k]]
        return pltpu.make_async_copy(src, buf.at[slot, :, k * tm + t], sems.at[slot])

    def issue(tile, slot):
        def body(t, _):
            row_copy(tile, slot, t, 0).start(priority=0)
            row_copy(tile, slot, t, 1).start(priority=1)
            return 0
        lax.fori_loop(0, tm, body, 0, unroll=8)

    @pl.when(i == 0)
    def _():
        issue(0, 0)

    @pl.when(i + 1 < n)
    def _():
        issue(i + 1, (i + 1) % 2)

    slot = i % 2

    def drain(t, _):
        row_copy(i, slot, t, 0).wait()
        row_copy(i, slot, t, 1).wait()
        return 0

    lax.fori_loop(0, tm, drain, 0, unroll=8)
    rt_t = rt_ref[0]
    rt = jnp.concatenate([rt_t, jnp.zeros((LANES - rt_t.shape[0], tm), F32)], axis=0).T
    w1 = rt[:, RT_W:RT_W + 1]
    w2 = rt[:, RT_W + 1:RT_W + 2]
    for s in range(nc):
        cols = slice(s * LANES, (s + 1) * LANES)
        y = w1 * buf[slot, s, 0:tm, :] + w2 * buf[slot, s, tm:2 * tm, :]
        o_ref[:, cols] = x_ref[:, cols] + ga_ref[0][:, cols] * y


def _combine(pos_flat, x1, rt, ga, ys, tm=256):
    B, S, D = x1.shape
    NC = ys.shape[0]
    tm = min(tm, S)
    nt = S // tm
    per_rt = rt.shape[2] // tm
    return pl.pallas_call(
        _combine_kernel,
        grid_spec=pltpu.PrefetchScalarGridSpec(
            num_scalar_prefetch=1,
            grid=(B * nt,),
            in_specs=[pl.BlockSpec((tm, D), lambda i, pos: (i, 0)),
                      pl.BlockSpec((1, rt.shape[1], tm), lambda i, pos: (i // per_rt, 0, i % per_rt)),
                      pl.BlockSpec((1, 1, D), lambda i, pos: (i // nt, 0, 0)),
                      pl.BlockSpec(memory_space=pl.ANY)],
            out_specs=pl.BlockSpec((tm, D), lambda i, pos: (i, 0)),
            scratch_shapes=[pltpu.VMEM((2, NC, 2 * tm, LANES), F32),
                            pltpu.SemaphoreType.DMA((2,))]),
        out_shape=jax.ShapeDtypeStruct((B * S, D), F32),
        compiler_params=_cparams(("arbitrary",)),
        name="moe_combine",
    )(pos_flat, x1.reshape(B * S, D), rt, ga, ys).reshape(B, S, D)


def _moe(h3, rt, x1, ga, w1, w3, w2, first_expert):
    n_tok_tiles = rt.shape[0]
    T = n_tok_tiles * rt.shape[2]
    n_tiles = -(-(2 * T + n_tok_tiles * N_EXPERTS * ROW_ALIGN) // MOE_TILE) + N_EXPERTS
    pos, info, tinfo = _rank(rt)
    pos_flat = pos[:, :2, :].transpose(0, 2, 1).reshape(-1)
    counts = info[0, :, 0]
    starts = info[1, :, 0]
    ends = starts + jnp.ceil(counts * (1.0 / MOE_TILE))
    tile_ids = jnp.arange(n_tiles, dtype=F32)
    tile_expert = jnp.minimum(jnp.sum(tile_ids[:, None] >= ends[None, :], axis=1), N_EXPERTS - 1)
    n_valid = ends[N_EXPERTS - 1:].astype(jnp.int32)
    as_ints = lambda a: a.astype(jnp.int32).reshape(-1)
    xs = _dispatch(as_ints(tinfo[:, 0, :, 0]), as_ints(tinfo[:, 1, :, 0]),
                   as_ints(tinfo[:, 2, :, 0]), as_ints(starts * MOE_TILE + counts),
                   as_ints(ends * MOE_TILE - starts * MOE_TILE - counts), n_valid,
                   h3.reshape(T, h3.shape[-1]), pos, n_tiles * MOE_TILE)
    ys = _experts(tile_expert.astype(jnp.int32) + first_expert, n_valid, xs, w1, w3, w2)
    return _combine(pos_flat, x1, rt, ga, ys)


def kernel(x, c, w_mod, b_mod, g_norm1, w_in, gq_a, gk_a, lam_a, g_sub_a, w_pool, b_pool, pool_scale, gq_c, gk_c, w_out, g_norm2, w_rg, b_rg, w_re, b_re, w1, w3, w2):
    B, S, D = x.shape
    L = w_mod.shape[0]
    a_width = D // 2
    pool_width = D // 4
    c_width = D // 4
    W = MXU_DIM
    reps = W // HEAD_DIM

    inv = 1.0 / (ROPE_THETA ** (jnp.arange(0, HEAD_DIM, 2, dtype=F32) / HEAD_DIM))
    ang = jnp.arange(S, dtype=F32)[:, None] * inv[None, :]
    ang = jnp.concatenate([ang, ang], axis=-1)
    cos_h, sin_h = jnp.cos(ang), jnp.sin(ang)
    first = jnp.arange(HEAD_DIM) < HEAD_DIM // 2
    cos_t = jnp.tile(cos_h, (1, reps))
    sa_t = jnp.tile(jnp.where(first[None, :], 0.0, sin_h), (1, reps))
    sb_t = jnp.tile(jnp.where(first[None, :], -sin_h, 0.0), (1, reps))
    head_of = np.arange(W) // HEAD_DIM
    bd = jnp.asarray((head_of[:, None] == head_of[None, :]) / HEAD_DIM, BF16)

    qa0, ka0, va0 = 0, a_width, 2 * a_width
    ub0 = 3 * a_width
    qc0, kc0, vc0 = ub0 + pool_width, ub0 + pool_width + c_width, ub0 + pool_width + 2 * c_width
    chunks = lambda lo_, hi_: list(range(lo_ // W, hi_ // W))
    z_chunks = tuple([(j, True) for j in chunks(qa0, va0)] + [(j, False) for j in chunks(ub0, qc0)]
                     + [(j, True) for j in chunks(qc0, vc0)])
    v_chunks = tuple(chunks(va0, ub0) + chunks(vc0, w_in.shape[2]))
    zq_a, zk_a = 0, a_width // LANES
    z_ub = 2 * a_width // pool_width
    zq_c = (2 * a_width + pool_width) // LANES
    zk_c = zq_c + c_width // LANES
    scale = HEAD_DIM ** -0.5 * math.log2(math.e)
    bias_t = _dilated_tables(S)

    mod = _modulation(c, w_mod, b_mod)
    for l in range(L):
        sh1, sc1, ga1, sh2, sc2, ga2 = [m[:, None, :] for m in jnp.split(mod[l], N_MOD, axis=-1)]
        gain = jnp.ones((w_in.shape[2],), F32)
        gain = gain.at[qa0:ka0].set(jnp.tile(gq_a[l], a_width // HEAD_DIM) * scale)
        gain = gain.at[ka0:va0].set(jnp.tile(gk_a[l], a_width // HEAD_DIM))
        gain = gain.at[qc0:kc0].set(jnp.tile(gq_c[l], c_width // HEAD_DIM) * scale)
        gain = gain.at[kc0:vc0].set(jnp.tile(gk_c[l], c_width // HEAD_DIM))
        z, vt = _in_projection(x, sc1, sh1, g_norm1[l][None, :], w_in[l].astype(BF16),
                               gain[None, :], cos_t, sa_t, sb_t, bd, z_chunks, v_chunks)

        lam_init = 0.8 - 0.6 * math.exp(-0.3 * l)
        n_a = a_width // LANES
        ya = _attention("diff", z, vt, zq_a, zk_a, 0, n_a,
                        [lam_a[l], g_sub_a[l][:, None]],
                        [pl.BlockSpec(lam_a[l].shape, lambda b, h: (0, 0)),
                         pl.BlockSpec((LANES, 1), lambda b, h: (0, 0))], lam_init=lam_init)
        n_c = c_width // LANES
        yc = _attention("dil", z, vt, zq_c, zk_c, n_a, n_c,
                        [bias_t], [pl.BlockSpec(bias_t.shape, lambda b, h: (0, 0, 0))])
        w_bd = jax.scipy.linalg.block_diag(*[w_pool[l, g] for g in range(w_pool.shape[1])])
        yb = _pool_mixer(z, z_ub, w_bd.astype(BF16), b_pool[l].reshape(1, -1),
                         pool_scale[l][None, :])

        wr = jnp.zeros((LANES, D), F32)
        wr = wr.at[:N_EXPERTS].set(w_re[l].transpose(0, 2, 1).reshape(N_EXPERTS, D))
        wr = wr.at[N_EXPERTS:N_EXPERTS + N_GROUPS].set(w_rg[l].T)
        br = jnp.zeros((LANES, 1), F32)
        br = br.at[:N_EXPERTS, 0].set(b_re[l].reshape(-1))
        br = br.at[N_EXPERTS:N_EXPERTS + N_GROUPS, 0].set(b_rg[l])
        x1, h3, rt = _out_projection(ya, yb, yc, x, w_out[l].astype(BF16), ga1, sc2, sh2,
                                     g_norm2[l][None, :], wr.astype(BF16), br)
        F = w1.shape[-1]
        x = _moe(h3, rt, x1, ga2, w1.reshape(L * N_EXPERTS, D, F), w3.reshape(L * N_EXPERTS, D, F),
                 w2.reshape(L * N_EXPERTS, F, D), l * N_EXPERTS)
    return x
```

```python
import functools
import math

import jax
import jax.numpy as jnp
import numpy as np
from jax import lax
from jax.experimental import pallas as pl
from jax.experimental.pallas import tpu as pltpu

HEAD_DIM = 64
POOL_WINDOWS = (2, 4, 8, 16)
DILATED_PAIRS = ((128, 1), (512, 4), (2048, 16))
ROPE_THETA = 10000.0
N_GROUPS = 4
EXPERTS_PER_GROUP = 8
N_EXPERTS = N_GROUPS * EXPERTS_PER_GROUP
N_MOD = 6
EPS = 1e-6

LANES = 128
BF16_ROWS = 16
MXU_DIM = 256
KV_BLOCK = 256
Q_BLOCK = 2 * KV_BLOCK
MOE_TILE = 256
DISPATCH_TILE = 512
ROW_ALIGN = 8
COPY_CHUNK = 64
RT_E, RT_W = 0, 2
VMEM_LIMIT = 48 * 1024 * 1024
NEG = -1e30
F32 = jnp.float32
BF16 = jnp.bfloat16


def _cparams(sem):
    return pltpu.CompilerParams(dimension_semantics=sem, vmem_limit_bytes=VMEM_LIMIT)


def _dot(a, b):
    return jnp.dot(a, b, preferred_element_type=F32)


def _dot_nt(a, b):
    return lax.dot_general(a, b, (((1,), (1,)), ((), ())), preferred_element_type=F32)


def _mod_kernel(c_ref, w_ref, b_ref, o_ref):
    c = c_ref[...]
    cond = c * (1.0 / (1.0 + jnp.exp(-c)))
    o_ref[0] = _dot(cond.astype(BF16), w_ref[0].astype(BF16)) + b_ref[0]


def _modulation(c, w_mod, b_mod):
    L, D, N = w_mod.shape
    B = c.shape[0]
    tn = 1024
    return pl.pallas_call(
        _mod_kernel,
        grid=(L, N // tn),
        in_specs=[pl.BlockSpec((B, D), lambda l, j: (0, 0)),
                  pl.BlockSpec((1, D, tn), lambda l, j: (l, 0, j)),
                  pl.BlockSpec((1, 1, tn), lambda l, j: (l, 0, j))],
        out_specs=pl.BlockSpec((1, B, tn), lambda l, j: (l, 0, j)),
        out_shape=jax.ShapeDtypeStruct((L, B, N), F32),
        compiler_params=_cparams(("parallel", "parallel")),
        name="modulation",
    )(c, w_mod, b_mod.reshape(L, 1, N))


def _inproj_kernel(z_chunks, v_chunks, x_ref, sc_ref, sh_ref, g_ref, w_ref, gain_ref, cos_ref,
                   sa_ref, sb_ref, bd_ref, z_ref, vt_ref):
    x = x_ref[0]
    ms = jnp.mean(x * x, axis=-1, keepdims=True)
    h = x * lax.rsqrt(ms + EPS) * g_ref[...]
    h = h * (1.0 + sc_ref[0]) + sh_ref[0]
    hb = h.astype(BF16)
    W = MXU_DIM

    def project(src):
        return _dot(hb, w_ref[:, src * W:(src + 1) * W])

    def finish_z(dst, src, normed, zc):
        if normed:
            msq = _dot((zc * zc).astype(BF16), bd_ref[...])
            y = zc * lax.rsqrt(msq + EPS) * gain_ref[:, src * W:(src + 1) * W]
            r_up = pltpu.roll(y, HEAD_DIM // 2, 1)
            r_dn = pltpu.roll(y, W - HEAD_DIM // 2, 1)
            zc = y * cos_ref[...] + r_up * sa_ref[...] + r_dn * sb_ref[...]
        z_ref[0, :, dst * W:(dst + 1) * W] = zc.astype(BF16)

    def finish_v(dst, zc):
        zt = zc.T.astype(BF16)
        for cb in range(vt_ref.shape[1]):
            vt_ref[0, cb, dst * W:(dst + 1) * W, :] = zt[:, cb * KV_BLOCK:(cb + 1) * KV_BLOCK]

    work = ([(src, functools.partial(finish_z, dst, src, normed)) for dst, (src, normed) in enumerate(z_chunks)]
            + [(src, functools.partial(finish_v, dst)) for dst, src in enumerate(v_chunks)])
    raw = project(work[0][0])
    for idx, (_, finish) in enumerate(work):
        nxt = project(work[idx + 1][0]) if idx + 1 < len(work) else None
        finish(raw)
        raw = nxt


def _in_projection(x, sc, sh, g, w_bf, gain, cos_t, sa_t, sb_t, bd, z_chunks, v_chunks, tm=512):
    B, S, D = x.shape
    N = w_bf.shape[1]
    W = MXU_DIM
    nz, nv = len(z_chunks) * W, len(v_chunks) * W
    return pl.pallas_call(
        functools.partial(_inproj_kernel, z_chunks, v_chunks),
        grid=(B, S // tm),
        in_specs=[pl.BlockSpec((1, tm, D), lambda b, i: (b, i, 0)),
                  pl.BlockSpec((1, 1, D), lambda b, i: (b, 0, 0)),
                  pl.BlockSpec((1, 1, D), lambda b, i: (b, 0, 0)),
                  pl.BlockSpec((1, D), lambda b, i: (0, 0)),
                  pl.BlockSpec((D, N), lambda b, i: (0, 0)),
                  pl.BlockSpec((1, N), lambda b, i: (0, 0)),
                  pl.BlockSpec((tm, W), lambda b, i: (i, 0)),
                  pl.BlockSpec((tm, W), lambda b, i: (i, 0)),
                  pl.BlockSpec((tm, W), lambda b, i: (i, 0)),
                  pl.BlockSpec((W, W), lambda b, i: (0, 0))],
        out_specs=[pl.BlockSpec((1, tm, nz), lambda b, i: (b, i, 0)),
                   pl.BlockSpec((1, tm // KV_BLOCK, nv, KV_BLOCK), lambda b, i: (b, i, 0, 0))],
        out_shape=[jax.ShapeDtypeStruct((B, S, nz), BF16),
                   jax.ShapeDtypeStruct((B, S // KV_BLOCK, nv, KV_BLOCK), BF16)],
        compiler_params=_cparams(("parallel", "parallel")),
        name="in_projection",
    )(x, sc, sh, g, w_bf, gain, cos_t, sa_t, sb_t, bd)


def _attn_kernel(mode, lam_init, *refs):
    if mode == "diff":
        q_ref, k_ref, vt_ref, lam_ref, g_ref, o_ref, s_a, s_b, acc_ref = refs
    else:
        q_ref, k_ref, vt_ref, bias_ref, o_ref, s_a, s_b, acc_ref = refs
    bq, bk = Q_BLOCK, KV_BLOCK
    S = q_ref.shape[1]
    nq = S // bq
    lane = lax.broadcasted_iota(jnp.int32, (1, LANES), 1)
    lo = lane < HEAD_DIM
    extra = acc_ref.shape[0] - LANES
    ones_rows = jnp.where(lax.broadcasted_iota(jnp.int32, (extra, bk), 0) == 0, 1.0, 0.0).astype(BF16)

    def both_maps(qi):
        q = q_ref[0, pl.ds(pl.multiple_of(qi * bq, bq), bq), :]
        zero = jnp.zeros_like(q)
        return jnp.concatenate([jnp.where(lo, q, zero), jnp.where(lo, zero, q)], axis=0)

    def scores(s_ref, kb, q2):
        ks = pl.multiple_of(kb * bk, bk)
        s_ref[...] = _dot_nt(k_ref[0, pl.ds(ks, bk), :], q2)

    def q_block(qi, _):
        qs = pl.multiple_of(qi * bq, bq)
        q2 = both_maps(qi)

        def table(kb):
            if mode == "diff":
                return None
            return bias_ref[jnp.minimum(2 * qi - kb + 1, bias_ref.shape[0] - 1)]

        def update(s_ref, kb, stats, bias, first=0):
            vt1 = jnp.concatenate([vt_ref[0, kb], ones_rows], axis=0)
            out = []
            for half in range(2):
                cols = slice(half * bq + first, (half + 1) * bq)
                m = stats[half]
                s = s_ref[:, cols]
                if bias is not None:
                    s = s + bias[:, first:]
                mn = jnp.maximum(m[:, first:], jnp.max(s, axis=0, keepdims=True))
                p = jnp.exp2(s - mn)
                acc_ref[:, cols] = (jnp.exp2(m[:, first:] - mn) * acc_ref[:, cols]
                                    + _dot(vt1, p.astype(BF16)))
                out.append(jnp.concatenate([m[:, :first], mn], axis=1) if first else mn)
            return tuple(out)

        def pair(j, stats):
            scores(s_b, 2 * j + 1, q2)
            stats = update(s_a, 2 * j, stats, table(2 * j))
            scores(s_a, 2 * j + 2, q2)
            return update(s_b, 2 * j + 1, stats, table(2 * j + 1))

        acc_ref[...] = jnp.zeros_like(acc_ref)
        stats = lax.fori_loop(0, qi, pair, (jnp.full((1, bq), NEG, F32),) * 2)

        krow = lax.broadcasted_iota(jnp.int32, (bk, bq), 0)
        qcol = lax.broadcasted_iota(jnp.int32, (bk, bq), 1)
        scores(s_b, 2 * qi + 1, q2)
        for rel, s_ref in enumerate((s_a, s_b)):
            bias = table(2 * qi + rel)
            if mode == "diff":
                bias = jnp.where(krow + rel * bk <= qcol, 0.0, NEG).astype(F32)
            stats = update(s_ref, 2 * qi + rel, stats, bias, first=rel * bk)
            if rel == 0:
                scores(s_a, 0, both_maps(jnp.minimum(qi + 1, nq - 1)))
        a_lo, a_hi = acc_ref[0:LANES, :bq], acc_ref[0:LANES, bq:]
        l_lo, l_hi = acc_ref[LANES:LANES + 1, :bq], acc_ref[LANES:LANES + 1, bq:]
        if mode == "diff":
            lp = lam_ref[...]
            lam = (jnp.exp(jnp.sum(lp[0:1] * lp[1:2], axis=-1, keepdims=True))
                   - jnp.exp(jnp.sum(lp[2:3] * lp[3:4], axis=-1, keepdims=True)) + lam_init)
            y = a_lo * (1.0 / l_lo) - a_hi * (lam / l_hi)
            ms = jnp.mean(y * y, axis=0, keepdims=True)
            y = y * lax.rsqrt(ms + EPS) * (g_ref[...] * (1.0 - lam_init))
        else:
            feat = lax.broadcasted_iota(jnp.int32, (LANES, 1), 0)
            y = jnp.where(feat < HEAD_DIM, a_lo * (1.0 / l_lo), a_hi * (1.0 / l_hi))
        o_ref[0, pl.ds(qs, bq), :] = y.T.astype(o_ref.dtype)
        return 0

    scores(s_a, 0, both_maps(0))
    lax.fori_loop(0, nq, q_block, 0)


def _attention(mode, z, vt, q_col, k_col, v_row, n_blocks, extras, extra_specs, lam_init=0.0):
    B, S, _ = z.shape
    zspec = lambda col: pl.BlockSpec((1, S, LANES), lambda b, h: (b, 0, col + h))
    vspec = pl.BlockSpec((1, S // KV_BLOCK, LANES, KV_BLOCK), lambda b, h: (b, 0, v_row + h, 0))
    return pl.pallas_call(
        functools.partial(_attn_kernel, mode, lam_init),
        grid=(B, n_blocks),
        in_specs=[zspec(q_col), zspec(k_col), vspec] + extra_specs,
        out_specs=pl.BlockSpec((1, S, LANES), lambda b, h: (b, 0, h)),
        out_shape=jax.ShapeDtypeStruct((B, S, n_blocks * LANES), BF16),
        scratch_shapes=[pltpu.VMEM((KV_BLOCK, 2 * Q_BLOCK), F32), pltpu.VMEM((KV_BLOCK, 2 * Q_BLOCK), F32),
                        pltpu.VMEM((LANES + BF16_ROWS, 2 * Q_BLOCK), F32)],
        compiler_params=_cparams(("parallel", "parallel")),
        name=mode + "_attention",
    )(z, z, vt, *extras)


def _dilated_tables(S):
    bq, bk = Q_BLOCK, KV_BLOCK
    dds = np.arange(-1, S // bk)
    dist = (dds[:, None, None] * bk + np.arange(bq)[None, None, :] - np.arange(bk)[None, :, None])
    cnt = np.zeros(dist.shape, np.float32)
    for window, dil in DILATED_PAIRS:
        cnt += (dist >= 0) & (dist <= window) & (dist % dil == 0)
    n = len(dds)
    while n > 1 and np.array_equal(cnt[n - 1], cnt[n - 2]):
        n -= 1
    cnt = cnt[:n]
    return jnp.asarray(np.where(cnt > 0, np.log2(np.maximum(cnt, 1.0)), NEG).astype(np.float32))


def _pool_kernel(u_ref, w_ref, b_ref, sc_ref, o_ref):
    u = u_ref[0].astype(F32)
    S, C = u.shape
    row = lax.broadcasted_iota(jnp.int32, (S, C), 0)
    lane = lax.broadcasted_iota(jnp.int32, (S, C), 1)
    grp = lane // (C // len(POOL_WINDOWS))

    def shifted(a, k):
        return jnp.where(row >= k, pltpu.roll(a, k, 0), 0.0)

    acc = u
    win = jnp.zeros_like(u)
    width = 1
    for gi, w in enumerate(POOL_WINDOWS):
        while width < w:
            acc = acc + shifted(acc, width)
            width *= 2
        win = jnp.where(grp == gi, acc, win)
    wl = jnp.zeros_like(row)
    for gi, w in enumerate(POOL_WINDOWS):
        wl = jnp.where(grp == gi, w, wl)
    cnt = jnp.minimum(row + 1, wl).astype(F32)
    d = win / cnt - u
    y = _dot(d.astype(BF16), w_ref[...]) + b_ref[...]
    o_ref[0] = (y * sc_ref[...]).astype(o_ref.dtype)


def _pool_mixer(z, col_block, w_bd, b, scale):
    B, S, _ = z.shape
    C = w_bd.shape[0]
    return pl.pallas_call(
        _pool_kernel,
        grid=(B,),
        in_specs=[pl.BlockSpec((1, S, C), lambda b_: (b_, 0, col_block)),
                  pl.BlockSpec((C, C), lambda b_: (0, 0)),
                  pl.BlockSpec((1, C), lambda b_: (0, 0)),
                  pl.BlockSpec((1, C), lambda b_: (0, 0))],
        out_specs=pl.BlockSpec((1, S, C), lambda b_: (b_, 0, 0)),
        out_shape=jax.ShapeDtypeStruct((B, S, C), BF16),
        compiler_params=_cparams(("parallel",)),
        name="pool_mixer",
    )(z, w_bd, b, scale)


def _outproj_kernel(ya_ref, yb_ref, yc_ref, x_ref, wo_ref, ga_ref, sc_ref, sh_ref, g_ref, wr_ref,
                    br_ref, x1_ref, h_ref, rt_ref):
    na = ya_ref.shape[2]
    nb = yb_ref.shape[2]
    halves = [slice(r * (x_ref.shape[1] // 2), (r + 1) * (x_ref.shape[1] // 2)) for r in range(2)]

    def mix(rows):
        return (_dot(ya_ref[0, rows, :], wo_ref[0:na, :]) + _dot(yb_ref[0, rows, :], wo_ref[na:na + nb, :])
                + _dot(yc_ref[0, rows, :], wo_ref[na + nb:, :]))

    def residual_norm(rows, y):
        x1 = x_ref[0, rows, :] + ga_ref[0] * y
        x1_ref[0, rows, :] = x1
        ms = jnp.mean(x1 * x1, axis=-1, keepdims=True)
        h = x1 * lax.rsqrt(ms + EPS) * g_ref[...]
        h = h * (1.0 + sc_ref[0]) + sh_ref[0]
        hb = h.astype(BF16)
        h_ref[0, rows, :] = hb
        return hb

    ys = [mix(rows) for rows in halves]
    hb = jnp.concatenate([residual_norm(rows, y) for rows, y in zip(halves, ys)], axis=0)

    logits = _dot_nt(wr_ref[...], hb) + br_ref[...]
    tm = logits.shape[1]
    G, E = N_GROUPS, EXPERTS_PER_GROUP
    big = float(LANES)
    grow = lax.broadcasted_iota(jnp.int32, (8, tm), 0)
    grow_f = grow.astype(F32)
    gl = jnp.where(grow < G, logits[G * E:G * E + 8, :], -jnp.inf)
    gmax = jnp.max(gl, axis=0, keepdims=True)
    gidx = jnp.min(jnp.where(gl == gmax, grow_f, big), axis=0, keepdims=True)
    gsum = jnp.sum(jnp.where(grow < G, jnp.exp(gl - gmax), 0.0), axis=0, keepdims=True)
    g_w = 1.0 / gsum
    erow = lax.broadcasted_iota(jnp.int32, (G * E, tm), 0)
    erow_f = erow.astype(F32)
    el = jnp.where((erow // E).astype(F32) == gidx, logits[0:G * E, :], -jnp.inf)
    v1 = jnp.max(el, axis=0, keepdims=True)
    i1 = jnp.min(jnp.where(el == v1, erow_f, big), axis=0, keepdims=True)
    el2 = jnp.where(erow_f == i1, -jnp.inf, el)
    v2 = jnp.max(el2, axis=0, keepdims=True)
    i2 = jnp.min(jnp.where(el2 == v2, erow_f, big), axis=0, keepdims=True)
    t = jnp.exp(v2 - v1)
    w1 = g_w / (1.0 + t)
    w2 = w1 * t
    rt_ref[0] = jnp.concatenate([i1, i2, w1, w2, jnp.zeros((rt_ref.shape[1] - 4, tm), F32)], axis=0)


def _out_projection(ya, yb, yc, x, wo_bf, ga, sc, sh, g, wr, br, tm=DISPATCH_TILE):
    B, S, D = x.shape
    nt = S // tm
    tok = lambda n: pl.BlockSpec((1, tm, n), lambda b, i: (b, i, 0))
    per_b = pl.BlockSpec((1, 1, D), lambda b, i: (b, 0, 0))
    const = lambda shape: pl.BlockSpec(shape, lambda b, i: (0, 0))
    return pl.pallas_call(
        _outproj_kernel,
        grid=(B, nt),
        in_specs=[tok(ya.shape[2]), tok(yb.shape[2]), tok(yc.shape[2]), tok(D), const((D, D)),
                  per_b, per_b, per_b, const((1, D)), const((LANES, D)), const((LANES, 1))],
        out_specs=[tok(D), tok(D),
                   pl.BlockSpec((1, 8, tm), lambda b, i: (b * nt + i, 0, 0))],
        out_shape=[jax.ShapeDtypeStruct((B, S, D), F32),
                   jax.ShapeDtypeStruct((B, S, D), BF16),
                   jax.ShapeDtypeStruct((B * nt, 8, tm), F32)],
        compiler_params=_cparams(("parallel", "parallel")),
        name="out_projection",
    )(ya, yb, yc, x, wo_bf, ga, sc, sh, g, wr, br)


def _rank_kernel(rt_ref, before_ref, ltri_ref, pos_ref, info_ref, tinfo_ref, cnt_ref, off_ref, carry_ref):
    ph, i = pl.program_id(0), pl.program_id(1)
    ne = N_EXPERTS
    rt_t = rt_ref[0]
    tm = rt_t.shape[1]
    expert = lax.broadcasted_iota(jnp.int32, (ne, tm), 0).astype(F32)
    hit1 = expert == rt_t[RT_E:RT_E + 1, :]
    hit2 = expert == rt_t[RT_E + 1:RT_E + 2, :]
    onehot = jnp.where(hit1, 1.0, jnp.where(hit2, 1.0, 0.0))
    n_blk = jnp.ceil(jnp.sum(onehot, axis=1, keepdims=True) * (1.0 / ROW_ALIGN)) * ROW_ALIGN
    n_blk = jnp.broadcast_to(n_blk, (ne, LANES))

    @pl.when((ph == 0) & (i == 0))
    def _():
        cnt_ref[...] = jnp.zeros_like(cnt_ref)

    @pl.when(ph == 0)
    def _():
        cnt_ref[...] += n_blk

    @pl.when((ph == 1) & (i == 0))
    def _():
        ntile = jnp.ceil(cnt_ref[...] * (1.0 / MOE_TILE))
        off_tiles = _dot(ltri_ref[...], ntile.astype(BF16))
        off_ref[...] = off_tiles * MOE_TILE
        info_ref[0] = cnt_ref[...]
        info_ref[1] = off_tiles
        carry_ref[...] = jnp.zeros_like(carry_ref)

    @pl.when(ph == 1)
    def _():
        dst0 = carry_ref[...] + off_ref[...]
        loc0 = _dot(ltri_ref[...], (n_blk * (1.0 / ROW_ALIGN)).astype(BF16)) * ROW_ALIGN
        within = _dot(onehot.astype(BF16), before_ref[...])
        in_sorted = within + dst0[:, 0:1]
        in_buffer = within + loc0[:, 0:1]
        rows = [jnp.sum(jnp.where(hit, v, 0.0), axis=0, keepdims=True)
                for v in (in_sorted, in_buffer) for hit in (hit1, hit2)]
        rows.append(jnp.zeros((pos_ref.shape[1] - len(rows), tm), F32))
        pos_ref[0] = jnp.concatenate(rows, axis=0).astype(jnp.int32)
        tinfo_ref[0, 0] = dst0
        tinfo_ref[0, 1] = n_blk
        tinfo_ref[0, 2] = loc0
        carry_ref[...] += n_blk


def _rank(rt):
    n_tok_tiles, _, tm = rt.shape
    T = n_tok_tiles * tm
    idx = np.arange(tm)
    before = jnp.asarray(idx[:, None] < idx[None, :], BF16)
    ex = np.arange(N_EXPERTS)
    ltri = jnp.asarray(ex[None, :] < ex[:, None], BF16)
    stat = pltpu.VMEM((N_EXPERTS, LANES), F32)
    return pl.pallas_call(
        _rank_kernel,
        grid=(2, T // tm),
        in_specs=[pl.BlockSpec((1, 8, tm), lambda ph, i: (i, 0, 0)),
                  pl.BlockSpec((tm, tm), lambda ph, i: (0, 0)),
                  pl.BlockSpec((N_EXPERTS, N_EXPERTS), lambda ph, i: (0, 0))],
        out_specs=[pl.BlockSpec((1, 8, tm), lambda ph, i: (i * ph, 0, 0)),
                   pl.BlockSpec((2, N_EXPERTS, LANES), lambda ph, i: (0, 0, 0)),
                   pl.BlockSpec((1, 3, N_EXPERTS, LANES), lambda ph, i: (i * ph, 0, 0, 0))],
        out_shape=[jax.ShapeDtypeStruct((T // tm, 8, tm), jnp.int32),
                   jax.ShapeDtypeStruct((2, N_EXPERTS, LANES), F32),
                   jax.ShapeDtypeStruct((T // tm, 3, N_EXPERTS, LANES), F32)],
        scratch_shapes=[stat, stat, stat],
        compiler_params=_cparams(("arbitrary", "arbitrary")),
        name="moe_rank",
    )(rt, before, ltri)


def _block_copies(n, src_of, dst_of, sem, act):
    def whole(j, _):
        off = pl.multiple_of(j * COPY_CHUNK, COPY_CHUNK)
        act(pltpu.make_async_copy(src_of(off, COPY_CHUNK), dst_of(off, COPY_CHUNK), sem))
        return 0
    lax.fori_loop(0, n // COPY_CHUNK, whole, 0)
    for b in range(COPY_CHUNK.bit_length() - 2, ROW_ALIGN.bit_length() - 2, -1):
        size = 1 << b

        @pl.when((n >> b) & 1 == 1)
        def _():
            off = pl.multiple_of((n >> (b + 1)) << (b + 1), ROW_ALIGN)
            act(pltpu.make_async_copy(src_of(off, size), dst_of(off, size), sem))


def _dispatch_kernel(dst_ref, nblk_ref, loc_ref, tail0_ref, tailn_ref, nv_ref, h_ref, lpos_ref, xs_ref,
                     obuf, zbuf, sems, zsem):
    i = pl.program_id(0)
    n = pl.num_programs(0)
    nc, tm = obuf.shape[1], h_ref.shape[0]
    R = obuf.shape[2]
    slot = i % 2

    def tile_copies(tile, sl, act):
        def body(e, _):
            c = tile * N_EXPERTS + e
            loc = pl.multiple_of(loc_ref[c], ROW_ALIGN)
            dst = pl.multiple_of(dst_ref[c], ROW_ALIGN)
            _block_copies(nblk_ref[c],
                          lambda off, size: obuf.at[sl, :, pl.ds(loc + off, size)],
                          lambda off, size: xs_ref.at[:, pl.ds(dst + off, size)],
                          sems.at[sl], act)
            return 0
        lax.fori_loop(0, N_EXPERTS, body, 0)

    @pl.when(i >= 2)
    def _():
        tile_copies(i - 2, slot, lambda cp: cp.wait())

    hb = h_ref[...]
    lp = lpos_ref[0].astype(F32)
    l1, l2 = lp[2:3, :], lp[3:4, :]
    blk = MXU_DIM
    for jb in range(R // blk):
        r = (lax.broadcasted_iota(jnp.int32, (blk, tm), 0) + jb * blk).astype(F32)
        sel = jnp.where(r == l1, 1.0, jnp.where(r == l2, 1.0, 0.0)).astype(BF16)
        rows = _dot(sel, hb)
        for s in range(nc):
            obuf[slot, s, jb * blk:(jb + 1) * blk, :] = rows[:, s * LANES:(s + 1) * LANES]
    tile_copies(i, slot, lambda cp: cp.start())

    @pl.when(i == n - 1)
    def _():
        @pl.when(i >= 1)
        def _():
            tile_copies(i - 1, 1 - slot, lambda cp: cp.wait())
        tile_copies(i, slot, lambda cp: cp.wait())
        zbuf[...] = jnp.zeros_like(zbuf)

        def zero_fill(act):
            def body(e, _):
                t0 = pl.multiple_of(tail0_ref[e], ROW_ALIGN)
                _block_copies(tailn_ref[e], lambda off, size: zbuf.at[:, pl.ds(0, size)],
                              lambda off, size: xs_ref.at[:, pl.ds(t0 + off, size)], zsem, act)
                return 0
            lax.fori_loop(0, N_EXPERTS, body, 0)

            def unused(j, _):
                j0 = pl.multiple_of(j * MOE_TILE, MOE_TILE)
                act(pltpu.make_async_copy(zbuf, xs_ref.at[:, pl.ds(j0, MOE_TILE)], zsem))
                return 0
            lax.fori_loop(nv_ref[0], xs_ref.shape[1] // MOE_TILE, unused, 0)
        zero_fill(lambda cp: cp.start())
        zero_fill(lambda cp: cp.wait())


def _dispatch(dst0, nblk, loc0, tail0, tailn, n_valid, h, pos, n_rows):
    T, D = h.shape
    NC = D // LANES
    tm = pos.shape[2]
    R = 2 * tm + N_EXPERTS * ROW_ALIGN
    R = -(-R // MXU_DIM) * MXU_DIM
    return pl.pallas_call(
        _dispatch_kernel,
        grid_spec=pltpu.PrefetchScalarGridSpec(
            num_scalar_prefetch=6,
            grid=(T // tm,),
            in_specs=[pl.BlockSpec((tm, D), lambda i, *_: (i, 0)),
                      pl.BlockSpec((1, pos.shape[1], tm), lambda i, *_: (i, 0, 0))],
            out_specs=pl.BlockSpec(memory_space=pl.ANY),
            scratch_shapes=[pltpu.VMEM((2, NC, R, LANES), F32), pltpu.VMEM((NC, MOE_TILE, LANES), F32),
                            pltpu.SemaphoreType.DMA((2,)), pltpu.SemaphoreType.DMA]),
        out_shape=jax.ShapeDtypeStruct((NC, n_rows, LANES), F32),
        compiler_params=_cparams(("arbitrary",)),
        name="moe_dispatch",
    )(dst0, nblk, loc0, tail0, tailn, n_valid, h, pos)


def _experts_kernel(te_ref, nv_ref, xs_ref, w1_ref, w3_ref, w2_ref, ys_ref, w1b, w3b, w2b):
    j = pl.program_id(0)
    e = te_ref[j]
    e_prev = te_ref[jnp.maximum(j - 1, 0)]

    @pl.when((j == 0) | (e != e_prev))
    def _():
        w1b[...] = w1_ref[0].astype(BF16)
        w3b[...] = w3_ref[0].astype(BF16)
        w2b[...] = w2_ref[0].astype(BF16)

    @pl.when(j < nv_ref[0])
    def _():
        nc = xs_ref.shape[0]
        half = xs_ref.shape[1] // 2

        def up(rows):
            h = jnp.concatenate([xs_ref[s, rows, :] for s in range(nc)], axis=1).astype(BF16)
            return _dot(h, w1b[...]), _dot(h, w3b[...])

        def down(rows, ab):
            a, b = ab
            hid = (a * (1.0 / (1.0 + jnp.exp(-a))) * b).astype(BF16)
            y = _dot(hid, w2b[...])
            for s in range(nc):
                ys_ref[s, rows, :] = y[:, s * LANES:(s + 1) * LANES]

        top, bottom = slice(0, half), slice(half, 2 * half)
        ab_top = up(top)
        ab_bottom = up(bottom)
        down(top, ab_top)
        down(bottom, ab_bottom)


def _experts(tile_expert, n_valid, xs, w1, w3, w2):
    NC, P, _ = xs.shape
    NE, D, F = w1.shape
    nt = P // MOE_TILE
    tile = lambda j, te, nv: (0, jnp.minimum(j, nv[0] - 1), 0)
    wspec = lambda shape: pl.BlockSpec(shape, lambda j, te, nv: (te[j], 0, 0))
    return pl.pallas_call(
        _experts_kernel,
        grid_spec=pltpu.PrefetchScalarGridSpec(
            num_scalar_prefetch=2,
            grid=(nt,),
            in_specs=[pl.BlockSpec((NC, MOE_TILE, LANES), tile),
                      wspec((1, D, F)), wspec((1, D, F)), wspec((1, F, D))],
            out_specs=pl.BlockSpec((NC, MOE_TILE, LANES), tile),
            scratch_shapes=[pltpu.VMEM((D, F), BF16), pltpu.VMEM((D, F), BF16),
                            pltpu.VMEM((F, D), BF16)]),
        out_shape=jax.ShapeDtypeStruct((NC, P, LANES), F32),
        input_output_aliases={2: 0},
        compiler_params=_cparams(("arbitrary",)),
        name="moe_experts",
    )(tile_expert, n_valid, xs, w1, w3, w2)


def _combine_kernel(pos_ref, x_ref, rt_ref, ga_ref, ys_ref, o_ref, buf, sems):
    i = pl.program_id(0)
    n = pl.num_programs(0)
    tm = x_ref.shape[0]
    nc = buf.shape[1]

    def row_copy(tile, slot, t, k):
        src = ys_ref.at[:, pos_ref[tile * (2 * tm) + 2 * t + k]]
        return pltpu.make_async_copy(src, buf.at[slot, :, k * tm + t], sems.at[slot])

    def issue(tile, slot):
        def body(t, _):
            row_copy(tile, slot, t, 0).start(priority=0)
            row_copy(tile, slot, t, 1).start(priority=1)
            return 0
        lax.fori_loop(0, tm, body, 0, unroll=8)

    @pl.when(i == 0)
    def _():
        issue(0, 0)

    @pl.when(i + 1 < n)
    def _():
        issue(i + 1, (i + 1) % 2)

    slot = i % 2

    def drain(t, _):
        row_copy(i, slot, t, 0).wait()
        row_copy(i, slot, t, 1).wait()
        return 0

    lax.fori_loop(0, tm, drain, 0, unroll=8)
    rt_t = rt_ref[0]
    rt = jnp.concatenate([rt_t, jnp.zeros((LANES - rt_t.shape[0], tm), F32)], axis=0).T
    w1 = rt[:, RT_W:RT_W + 1]
    w2 = rt[:, RT_W + 1:RT_W + 2]
    for s in range(nc):
        cols = slice(s * LANES, (s + 1) * LANES)
        y = w1 * buf[slot, s, 0:tm, :] + w2 * buf[slot, s, tm:2 * tm, :]
        o_ref[:, cols] = x_ref[:, cols] + ga_ref[0][:, cols] * y


def _combine(pos_flat, x1, rt, ga, ys, tm=256):
    B, S, D = x1.shape
    NC = ys.shape[0]
    tm = min(tm, S)
    nt = S // tm
    per_rt = rt.shape[2] // tm
    return pl.pallas_call(
        _combine_kernel,
        grid_spec=pltpu.PrefetchScalarGridSpec(
            num_scalar_prefetch=1,
            grid=(B * nt,),
            in_specs=[pl.BlockSpec((tm, D), lambda i, pos: (i, 0)),
                      pl.BlockSpec((1, rt.shape[1], tm), lambda i, pos: (i // per_rt, 0, i % per_rt)),
                      pl.BlockSpec((1, 1, D), lambda i, pos: (i // nt, 0, 0)),
                      pl.BlockSpec(memory_space=pl.ANY)],
            out_specs=pl.BlockSpec((tm, D), lambda i, pos: (i, 0)),
            scratch_shapes=[pltpu.VMEM((2, NC, 2 * tm, LANES), F32),
                            pltpu.SemaphoreType.DMA((2,))]),
        out_shape=jax.ShapeDtypeStruct((B * S, D), F32),
        compiler_params=_cparams(("arbitrary",)),
        name="moe_combine",
    )(pos_flat, x1.reshape(B * S, D), rt, ga, ys).reshape(B, S, D)


def _moe(h3, rt, x1, ga, w1, w3, w2, first_expert):
    n_tok_tiles = rt.shape[0]
    T = n_tok_tiles * rt.shape[2]
    n_tiles = -(-(2 * T + n_tok_tiles * N_EXPERTS * ROW_ALIGN) // MOE_TILE) + N_EXPERTS
    pos, info, tinfo = _rank(rt)
    pos_flat = pos[:, :2, :].transpose(0, 2, 1).reshape(-1)
    counts = info[0, :, 0]
    starts = info[1, :, 0]
    ends = starts + jnp.ceil(counts * (1.0 / MOE_TILE))
    tile_ids = jnp.arange(n_tiles, dtype=F32)
    tile_expert = jnp.minimum(jnp.sum(tile_ids[:, None] >= ends[None, :], axis=1), N_EXPERTS - 1)
    n_valid = ends[N_EXPERTS - 1:].astype(jnp.int32)
    as_ints = lambda a: a.astype(jnp.int32).reshape(-1)
    xs = _dispatch(as_ints(tinfo[:, 0, :, 0]), as_ints(tinfo[:, 1, :, 0]),
                   as_ints(tinfo[:, 2, :, 0]), as_ints(starts * MOE_TILE + counts),
                   as_ints(ends * MOE_TILE - starts * MOE_TILE - counts), n_valid,
                   h3.reshape(T, h3.shape[-1]), pos, n_tiles * MOE_TILE)
    ys = _experts(tile_expert.astype(jnp.int32) + first_expert, n_valid, xs, w1, w3, w2)
    return _combine(pos_flat, x1, rt, ga, ys)


def kernel(x, c, w_mod, b_mod, g_norm1, w_in, gq_a, gk_a, lam_a, g_sub_a, w_pool, b_pool, pool_scale, gq_c, gk_c, w_out, g_norm2, w_rg, b_rg, w_re, b_re, w1, w3, w2):
    B, S, D = x.shape
    L = w_mod.shape[0]
    a_width = D // 2
    pool_width = D // 4
    c_width = D // 4
    W = MXU_DIM
    reps = W // HEAD_DIM

    inv = 1.0 / (ROPE_THETA ** (jnp.arange(0, HEAD_DIM, 2, dtype=F32) / HEAD_DIM))
    ang = jnp.arange(S, dtype=F32)[:, None] * inv[None, :]
    ang = jnp.concatenate([ang, ang], axis=-1)
    cos_h, sin_h = jnp.cos(ang), jnp.sin(ang)
    first = jnp.arange(HEAD_DIM) < HEAD_DIM // 2
    cos_t = jnp.tile(cos_h, (1, reps))
    sa_t = jnp.tile(jnp.where(first[None, :], 0.0, sin_h), (1, reps))
    sb_t = jnp.tile(jnp.where(first[None, :], -sin_h, 0.0), (1, reps))
    head_of = np.arange(W) // HEAD_DIM
    bd = jnp.asarray((head_of[:, None] == head_of[None, :]) / HEAD_DIM, BF16)

    qa0, ka0, va0 = 0, a_width, 2 * a_width
    ub0 = 3 * a_width
    qc0, kc0, vc0 = ub0 + pool_width, ub0 + pool_width + c_width, ub0 + pool_width + 2 * c_width
    chunks = lambda lo_, hi_: list(range(lo_ // W, hi_ // W))
    z_chunks = tuple([(j, True) for j in chunks(qa0, va0)] + [(j, False) for j in chunks(ub0, qc0)]
                     + [(j, True) for j in chunks(qc0, vc0)])
    v_chunks = tuple(chunks(va0, ub0) + chunks(vc0, w_in.shape[2]))
    zq_a, zk_a = 0, a_width // LANES
    z_ub = 2 * a_width // pool_width
    zq_c = (2 * a_width + pool_width) // LANES
    zk_c = zq_c + c_width // LANES
    scale = HEAD_DIM ** -0.5 * math.log2(math.e)
    bias_t = _dilated_tables(S)

    mod = _modulation(c, w_mod, b_mod)
    for l in range(L):
        sh1, sc1, ga1, sh2, sc2, ga2 = [m[:, None, :] for m in jnp.split(mod[l], N_MOD, axis=-1)]
        gain = jnp.ones((w_in.shape[2],), F32)
        gain = gain.at[qa0:ka0].set(jnp.tile(gq_a[l], a_width // HEAD_DIM) * scale)
        gain = gain.at[ka0:va0].set(jnp.tile(gk_a[l], a_width // HEAD_DIM))
        gain = gain.at[qc0:kc0].set(jnp.tile(gq_c[l], c_width // HEAD_DIM) * scale)
        gain = gain.at[kc0:vc0].set(jnp.tile(gk_c[l], c_width // HEAD_DIM))
        z, vt = _in_projection(x, sc1, sh1, g_norm1[l][None, :], w_in[l].astype(BF16),
                               gain[None, :], cos_t, sa_t, sb_t, bd, z_chunks, v_chunks)

        lam_init = 0.8 - 0.6 * math.exp(-0.3 * l)
        n_a = a_width // LANES
        ya = _attention("diff", z, vt, zq_a, zk_a, 0, n_a,
                        [lam_a[l], g_sub_a[l][:, None]],
                        [pl.BlockSpec(lam_a[l].shape, lambda b, h: (0, 0)),
                         pl.BlockSpec((LANES, 1), lambda b, h: (0, 0))], lam_init=lam_init)
        n_c = c_width // LANES
        yc = _attention("dil", z, vt, zq_c, zk_c, n_a, n_c,
                        [bias_t], [pl.BlockSpec(bias_t.shape, lambda b, h: (0, 0, 0))])
        w_bd = jax.scipy.linalg.block_diag(*[w_pool[l, g] for g in range(w_pool.shape[1])])
        yb = _pool_mixer(z, z_ub, w_bd.astype(BF16), b_pool[l].reshape(1, -1),
                         pool_scale[l][None, :])

        wr = jnp.zeros((LANES, D), F32)
        wr = wr.at[:N_EXPERTS].set(w_re[l].transpose(0, 2, 1).reshape(N_EXPERTS, D))
        wr = wr.at[N_EXPERTS:N_EXPERTS + N_GROUPS].set(w_rg[l].T)
        br = jnp.zeros((LANES, 1), F32)
        br = br.at[:N_EXPERTS, 0].set(b_re[l].reshape(-1))
        br = br.at[N_EXPERTS:N_EXPERTS + N_GROUPS, 0].set(b_rg[l])
        x1, h3, rt = _out_projection(ya, yb, yc, x, w_out[l].astype(BF16), ga1, sc2, sh2,
                                     g_norm2[l][None, :], wr.astype(BF16), br)
        F = w1.shape[-1]
        x = _moe(h3, rt, x1, ga2, w1.reshape(L * N_EXPERTS, D, F), w3.reshape(L * N_EXPERTS, D, F),
                 w2.reshape(L * N_EXPERTS, F, D), l * N_EXPERTS)
    return x
```

```python
import functools
import math

import jax
import jax.numpy as jnp
import numpy as np
from jax import lax
from jax.experimental import pallas as pl
from jax.experimental.pallas import tpu as pltpu

HEAD_DIM = 64
POOL_WINDOWS = (2, 4, 8, 16)
DILATED_PAIRS = ((128, 1), (512, 4), (2048, 16))
ROPE_THETA = 10000.0
N_GROUPS = 4
EXPERTS_PER_GROUP = 8
N_EXPERTS = N_GROUPS * EXPERTS_PER_GROUP
N_MOD = 6
EPS = 1e-6

LANES = 128
BF16_ROWS = 16
MXU_DIM = 256
KV_BLOCK = 256
Q_BLOCK = 2 * KV_BLOCK
MOE_TILE = 512
DISPATCH_TILE = 512
ROW_ALIGN = 8
COPY_CHUNK = 64
RT_E, RT_W = 0, 2
VMEM_LIMIT = 48 * 1024 * 1024
NEG = -1e30
F32 = jnp.float32
BF16 = jnp.bfloat16


def _cparams(sem):
    return pltpu.CompilerParams(dimension_semantics=sem, vmem_limit_bytes=VMEM_LIMIT)


def _dot(a, b):
    return jnp.dot(a, b, preferred_element_type=F32)


def _dot_nt(a, b):
    return lax.dot_general(a, b, (((1,), (1,)), ((), ())), preferred_element_type=F32)


def _mod_kernel(c_ref, w_ref, b_ref, o_ref):
    c = c_ref[...]
    cond = c * (1.0 / (1.0 + jnp.exp(-c)))
    o_ref[0] = _dot(cond.astype(BF16), w_ref[0].astype(BF16)) + b_ref[0]


def _modulation(c, w_mod, b_mod):
    L, D, N = w_mod.shape
    B = c.shape[0]
    tn = 1024
    return pl.pallas_call(
        _mod_kernel,
        grid=(L, N // tn),
        in_specs=[pl.BlockSpec((B, D), lambda l, j: (0, 0)),
                  pl.BlockSpec((1, D, tn), lambda l, j: (l, 0, j)),
                  pl.BlockSpec((1, 1, tn), lambda l, j: (l, 0, j))],
        out_specs=pl.BlockSpec((1, B, tn), lambda l, j: (l, 0, j)),
        out_shape=jax.ShapeDtypeStruct((L, B, N), F32),
        compiler_params=_cparams(("parallel", "parallel")),
        name="modulation",
    )(c, w_mod, b_mod.reshape(L, 1, N))


def _inproj_kernel(z_chunks, v_chunks, x_ref, sc_ref, sh_ref, g_ref, w_ref, gain_ref, cos_ref,
                   sa_ref, sb_ref, bd_ref, z_ref, vt_ref):
    x = x_ref[0]
    ms = jnp.mean(x * x, axis=-1, keepdims=True)
    h = x * lax.rsqrt(ms + EPS) * g_ref[...]
    h = h * (1.0 + sc_ref[0]) + sh_ref[0]
    hb = h.astype(BF16)
    W = MXU_DIM

    def project(src):
        return _dot(hb, w_ref[:, src * W:(src + 1) * W])

    def finish_z(dst, src, normed, zc):
        if normed:
            msq = _dot((zc * zc).astype(BF16), bd_ref[...])
            y = zc * lax.rsqrt(msq + EPS) * gain_ref[:, src * W:(src + 1) * W]
            r_up = pltpu.roll(y, HEAD_DIM // 2, 1)
            r_dn = pltpu.roll(y, W - HEAD_DIM // 2, 1)
            zc = y * cos_ref[...] + r_up * sa_ref[...] + r_dn * sb_ref[...]
        z_ref[0, :, dst * W:(dst + 1) * W] = zc.astype(BF16)

    def finish_v(dst, zc):
        zt = zc.T.astype(BF16)
        for cb in range(vt_ref.shape[1]):
            vt_ref[0, cb, dst * W:(dst + 1) * W, :] = zt[:, cb * KV_BLOCK:(cb + 1) * KV_BLOCK]

    work = ([(src, functools.partial(finish_z, dst, src, normed)) for dst, (src, normed) in enumerate(z_chunks)]
            + [(src, functools.partial(finish_v, dst)) for dst, src in enumerate(v_chunks)])
    raw = project(work[0][0])
    for idx, (_, finish) in enumerate(work):
        nxt = project(work[idx + 1][0]) if idx + 1 < len(work) else None
        finish(raw)
        raw = nxt


def _in_projection(x, sc, sh, g, w_bf, gain, cos_t, sa_t, sb_t, bd, z_chunks, v_chunks, tm=512):
    B, S, D = x.shape
    N = w_bf.shape[1]
    W = MXU_DIM
    nz, nv = len(z_chunks) * W, len(v_chunks) * W
    return pl.pallas_call(
        functools.partial(_inproj_kernel, z_chunks, v_chunks),
        grid=(B, S // tm),
        in_specs=[pl.BlockSpec((1, tm, D), lambda b, i: (b, i, 0)),
                  pl.BlockSpec((1, 1, D), lambda b, i: (b, 0, 0)),
                  pl.BlockSpec((1, 1, D), lambda b, i: (b, 0, 0)),
                  pl.BlockSpec((1, D), lambda b, i: (0, 0)),
                  pl.BlockSpec((D, N), lambda b, i: (0, 0)),
                  pl.BlockSpec((1, N), lambda b, i: (0, 0)),
                  pl.BlockSpec((tm, W), lambda b, i: (i, 0)),
                  pl.BlockSpec((tm, W), lambda b, i: (i, 0)),
                  pl.BlockSpec((tm, W), lambda b, i: (i, 0)),
                  pl.BlockSpec((W, W), lambda b, i: (0, 0))],
        out_specs=[pl.BlockSpec((1, tm, nz), lambda b, i: (b, i, 0)),
                   pl.BlockSpec((1, tm // KV_BLOCK, nv, KV_BLOCK), lambda b, i: (b, i, 0, 0))],
        out_shape=[jax.ShapeDtypeStruct((B, S, nz), BF16),
                   jax.ShapeDtypeStruct((B, S // KV_BLOCK, nv, KV_BLOCK), BF16)],
        compiler_params=_cparams(("parallel", "parallel")),
        name="in_projection",
    )(x, sc, sh, g, w_bf, gain, cos_t, sa_t, sb_t, bd)


def _attn_kernel(mode, lam_init, *refs):
    if mode == "diff":
        q_ref, k_ref, vt_ref, lam_ref, g_ref, o_ref, s_a, s_b, acc_ref = refs
    else:
        q_ref, k_ref, vt_ref, bias_ref, o_ref, s_a, s_b, acc_ref = refs
    bq, bk = Q_BLOCK, KV_BLOCK
    S = q_ref.shape[1]
    nq = S // bq
    lane = lax.broadcasted_iota(jnp.int32, (1, LANES), 1)
    lo = lane < HEAD_DIM
    extra = acc_ref.shape[0] - LANES
    ones_rows = jnp.where(lax.broadcasted_iota(jnp.int32, (extra, bk), 0) == 0, 1.0, 0.0).astype(BF16)

    def both_maps(qi):
        q = q_ref[0, pl.ds(pl.multiple_of(qi * bq, bq), bq), :]
        zero = jnp.zeros_like(q)
        return jnp.concatenate([jnp.where(lo, q, zero), jnp.where(lo, zero, q)], axis=0)

    def scores(s_ref, kb, q2):
        ks = pl.multiple_of(kb * bk, bk)
        s_ref[...] = _dot_nt(k_ref[0, pl.ds(ks, bk), :], q2)

    def q_block(qi, _):
        qs = pl.multiple_of(qi * bq, bq)
        q2 = both_maps(qi)

        def table(kb):
            if mode == "diff":
                return None
            return bias_ref[jnp.minimum(2 * qi - kb + 1, bias_ref.shape[0] - 1)]

        def update(s_ref, kb, stats, bias, first=0):
            vt1 = jnp.concatenate([vt_ref[0, kb], ones_rows], axis=0)
            out = []
            for half in range(2):
                cols = slice(half * bq + first, (half + 1) * bq)
                m = stats[half]
                s = s_ref[:, cols]
                if bias is not None:
                    s = s + bias[:, first:]
                mn = jnp.maximum(m[:, first:], jnp.max(s, axis=0, keepdims=True))
                p = jnp.exp2(s - mn)
                acc_ref[:, cols] = (jnp.exp2(m[:, first:] - mn) * acc_ref[:, cols]
                                    + _dot(vt1, p.astype(BF16)))
                out.append(jnp.concatenate([m[:, :first], mn], axis=1) if first else mn)
            return tuple(out)

        def pair(j, stats):
            scores(s_b, 2 * j + 1, q2)
            stats = update(s_a, 2 * j, stats, table(2 * j))
            scores(s_a, 2 * j + 2, q2)
            return update(s_b, 2 * j + 1, stats, table(2 * j + 1))

        acc_ref[...] = jnp.zeros_like(acc_ref)
        stats = lax.fori_loop(0, qi, pair, (jnp.full((1, bq), NEG, F32),) * 2)

        krow = lax.broadcasted_iota(jnp.int32, (bk, bq), 0)
        qcol = lax.broadcasted_iota(jnp.int32, (bk, bq), 1)
        scores(s_b, 2 * qi + 1, q2)
        for rel, s_ref in enumerate((s_a, s_b)):
            bias = table(2 * qi + rel)
            if mode == "diff":
                bias = jnp.where(krow + rel * bk <= qcol, 0.0, NEG).astype(F32)
            stats = update(s_ref, 2 * qi + rel, stats, bias, first=rel * bk)
            if rel == 0:
                scores(s_a, 0, both_maps(jnp.minimum(qi + 1, nq - 1)))
        a_lo, a_hi = acc_ref[0:LANES, :bq], acc_ref[0:LANES, bq:]
        l_lo, l_hi = acc_ref[LANES:LANES + 1, :bq], acc_ref[LANES:LANES + 1, bq:]
        if mode == "diff":
            lp = lam_ref[...]
            lam = (jnp.exp(jnp.sum(lp[0:1] * lp[1:2], axis=-1, keepdims=True))
                   - jnp.exp(jnp.sum(lp[2:3] * lp[3:4], axis=-1, keepdims=True)) + lam_init)
            y = a_lo * (1.0 / l_lo) - a_hi * (lam / l_hi)
            ms = jnp.mean(y * y, axis=0, keepdims=True)
            y = y * lax.rsqrt(ms + EPS) * (g_ref[...] * (1.0 - lam_init))
        else:
            feat = lax.broadcasted_iota(jnp.int32, (LANES, 1), 0)
            y = jnp.where(feat < HEAD_DIM, a_lo * (1.0 / l_lo), a_hi * (1.0 / l_hi))
        o_ref[0, pl.ds(qs, bq), :] = y.T.astype(o_ref.dtype)
        return 0

    scores(s_a, 0, both_maps(0))
    lax.fori_loop(0, nq, q_block, 0)


def _attention(mode, z, vt, q_col, k_col, v_row, n_blocks, extras, extra_specs, lam_init=0.0):
    B, S, _ = z.shape
    zspec = lambda col: pl.BlockSpec((1, S, LANES), lambda b, h: (b, 0, col + h))
    vspec = pl.BlockSpec((1, S // KV_BLOCK, LANES, KV_BLOCK), lambda b, h: (b, 0, v_row + h, 0))
    return pl.pallas_call(
        functools.partial(_attn_kernel, mode, lam_init),
        grid=(B, n_blocks),
        in_specs=[zspec(q_col), zspec(k_col), vspec] + extra_specs,
        out_specs=pl.BlockSpec((1, S, LANES), lambda b, h: (b, 0, h)),
        out_shape=jax.ShapeDtypeStruct((B, S, n_blocks * LANES), BF16),
        scratch_shapes=[pltpu.VMEM((KV_BLOCK, 2 * Q_BLOCK), F32), pltpu.VMEM((KV_BLOCK, 2 * Q_BLOCK), F32),
                        pltpu.VMEM((LANES + BF16_ROWS, 2 * Q_BLOCK), F32)],
        compiler_params=_cparams(("parallel", "parallel")),
        name=mode + "_attention",
    )(z, z, vt, *extras)


def _dilated_tables(S):
    bq, bk = Q_BLOCK, KV_BLOCK
    dds = np.arange(-1, S // bk)
    dist = (dds[:, None, None] * bk + np.arange(bq)[None, None, :] - np.arange(bk)[None, :, None])
    cnt = np.zeros(dist.shape, np.float32)
    for window, dil in DILATED_PAIRS:
        cnt += (dist >= 0) & (dist <= window) & (dist % dil == 0)
    n = len(dds)
    while n > 1 and np.array_equal(cnt[n - 1], cnt[n - 2]):
        n -= 1
    cnt = cnt[:n]
    return jnp.asarray(np.where(cnt > 0, np.log2(np.maximum(cnt, 1.0)), NEG).astype(np.float32))


def _pool_kernel(u_ref, w_ref, b_ref, sc_ref, o_ref):
    u = u_ref[0].astype(F32)
    S, C = u.shape
    row = lax.broadcasted_iota(jnp.int32, (S, C), 0)
    lane = lax.broadcasted_iota(jnp.int32, (S, C), 1)
    grp = lane // (C // len(POOL_WINDOWS))

    def shifted(a, k):
        return jnp.where(row >= k, pltpu.roll(a, k, 0), 0.0)

    acc = u
    win = jnp.zeros_like(u)
    width = 1
    for gi, w in enumerate(POOL_WINDOWS):
        while width < w:
            acc = acc + shifted(acc, width)
            width *= 2
        win = jnp.where(grp == gi, acc, win)
    wl = jnp.zeros_like(row)
    for gi, w in enumerate(POOL_WINDOWS):
        wl = jnp.where(grp == gi, w, wl)
    cnt = jnp.minimum(row + 1, wl).astype(F32)
    d = win / cnt - u
    y = _dot(d.astype(BF16), w_ref[...]) + b_ref[...]
    o_ref[0] = (y * sc_ref[...]).astype(o_ref.dtype)


def _pool_mixer(z, col_block, w_bd, b, scale):
    B, S, _ = z.shape
    C = w_bd.shape[0]
    return pl.pallas_call(
        _pool_kernel,
        grid=(B,),
        in_specs=[pl.BlockSpec((1, S, C), lambda b_: (b_, 0, col_block)),
                  pl.BlockSpec((C, C), lambda b_: (0, 0)),
                  pl.BlockSpec((1, C), lambda b_: (0, 0)),
                  pl.BlockSpec((1, C), lambda b_: (0, 0))],
        out_specs=pl.BlockSpec((1, S, C), lambda b_: (b_, 0, 0)),
        out_shape=jax.ShapeDtypeStruct((B, S, C), BF16),
        compiler_params=_cparams(("parallel",)),
        name="pool_mixer",
    )(z, w_bd, b, scale)


def _outproj_kernel(ya_ref, yb_ref, yc_ref, x_ref, wo_ref, ga_ref, sc_ref, sh_ref, g_ref, wr_ref,
                    br_ref, x1_ref, h_ref, rt_ref):
    na = ya_ref.shape[2]
    nb = yb_ref.shape[2]
    halves = [slice(r * (x_ref.shape[1] // 2), (r + 1) * (x_ref.shape[1] // 2)) for r in range(2)]

    def mix(rows):
        return (_dot(ya_ref[0, rows, :], wo_ref[0:na, :]) + _dot(yb_ref[0, rows, :], wo_ref[na:na + nb, :])
                + _dot(yc_ref[0, rows, :], wo_ref[na + nb:, :]))

    def residual_norm(rows, y):
        x1 = x_ref[0, rows, :] + ga_ref[0] * y
        x1_ref[0, rows, :] = x1
        ms = jnp.mean(x1 * x1, axis=-1, keepdims=True)
        h = x1 * lax.rsqrt(ms + EPS) * g_ref[...]
        h = h * (1.0 + sc_ref[0]) + sh_ref[0]
        hb = h.astype(BF16)
        h_ref[0, rows, :] = hb
        return hb

    ys = [mix(rows) for rows in halves]
    hb = jnp.concatenate([residual_norm(rows, y) for rows, y in zip(halves, ys)], axis=0)

    logits = _dot_nt(wr_ref[...], hb) + br_ref[...]
    tm = logits.shape[1]
    G, E = N_GROUPS, EXPERTS_PER_GROUP
    big = float(LANES)
    grow = lax.broadcasted_iota(jnp.int32, (8, tm), 0)
    grow_f = grow.astype(F32)
    gl = jnp.where(grow < G, logits[G * E:G * E + 8, :], -jnp.inf)
    gmax = jnp.max(gl, axis=0, keepdims=True)
    gidx = jnp.min(jnp.where(gl == gmax, grow_f, big), axis=0, keepdims=True)
    gsum = jnp.sum(jnp.where(grow < G, jnp.exp(gl - gmax), 0.0), axis=0, keepdims=True)
    g_w = 1.0 / gsum
    erow = lax.broadcasted_iota(jnp.int32, (G * E, tm), 0)
    erow_f = erow.astype(F32)
    el = jnp.where((erow // E).astype(F32) == gidx, logits[0:G * E, :], -jnp.inf)
    v1 = jnp.max(el, axis=0, keepdims=True)
    i1 = jnp.min(jnp.where(el == v1, erow_f, big), axis=0, keepdims=True)
    el2 = jnp.where(erow_f == i1, -jnp.inf, el)
    v2 = jnp.max(el2, axis=0, keepdims=True)
    i2 = jnp.min(jnp.where(el2 == v2, erow_f, big), axis=0, keepdims=True)
    t = jnp.exp(v2 - v1)
    w1 = g_w / (1.0 + t)
    w2 = w1 * t
    rt_ref[0] = jnp.concatenate([i1, i2, w1, w2, jnp.zeros((rt_ref.shape[1] - 4, tm), F32)], axis=0)


def _out_projection(ya, yb, yc, x, wo_bf, ga, sc, sh, g, wr, br, tm=DISPATCH_TILE):
    B, S, D = x.shape
    nt = S // tm
    tok = lambda n: pl.BlockSpec((1, tm, n), lambda b, i: (b, i, 0))
    per_b = pl.BlockSpec((1, 1, D), lambda b, i: (b, 0, 0))
    const = lambda shape: pl.BlockSpec(shape, lambda b, i: (0, 0))
    return pl.pallas_call(
        _outproj_kernel,
        grid=(B, nt),
        in_specs=[tok(ya.shape[2]), tok(yb.shape[2]), tok(yc.shape[2]), tok(D), const((D, D)),
                  per_b, per_b, per_b, const((1, D)), const((LANES, D)), const((LANES, 1))],
        out_specs=[tok(D), tok(D),
                   pl.BlockSpec((1, 8, tm), lambda b, i: (b * nt + i, 0, 0))],
        out_shape=[jax.ShapeDtypeStruct((B, S, D), F32),
                   jax.ShapeDtypeStruct((B, S, D), BF16),
                   jax.ShapeDtypeStruct((B * nt, 8, tm), F32)],
        compiler_params=_cparams(("parallel", "parallel")),
        name="out_projection",
    )(ya, yb, yc, x, wo_bf, ga, sc, sh, g, wr, br)


def _rank_kernel(rt_ref, before_ref, ltri_ref, pos_ref, info_ref, tinfo_ref, cnt_ref, off_ref, carry_ref):
    ph, i = pl.program_id(0), pl.program_id(1)
    ne = N_EXPERTS
    rt_t = rt_ref[0]
    tm = rt_t.shape[1]
    expert = lax.broadcasted_iota(jnp.int32, (ne, tm), 0).astype(F32)
    hit1 = expert == rt_t[RT_E:RT_E + 1, :]
    hit2 = expert == rt_t[RT_E + 1:RT_E + 2, :]
    onehot = jnp.where(hit1, 1.0, jnp.where(hit2, 1.0, 0.0))
    n_blk = jnp.ceil(jnp.sum(onehot, axis=1, keepdims=True) * (1.0 / ROW_ALIGN)) * ROW_ALIGN
    n_blk = jnp.broadcast_to(n_blk, (ne, LANES))

    @pl.when((ph == 0) & (i == 0))
    def _():
        cnt_ref[...] = jnp.zeros_like(cnt_ref)

    @pl.when(ph == 0)
    def _():
        cnt_ref[...] += n_blk

    @pl.when((ph == 1) & (i == 0))
    def _():
        ntile = jnp.ceil(cnt_ref[...] * (1.0 / MOE_TILE))
        off_tiles = _dot(ltri_ref[...], ntile.astype(BF16))
        off_ref[...] = off_tiles * MOE_TILE
        info_ref[0] = cnt_ref[...]
        info_ref[1] = off_tiles
        carry_ref[...] = jnp.zeros_like(carry_ref)

    @pl.when(ph == 1)
    def _():
        dst0 = carry_ref[...] + off_ref[...]
        loc0 = _dot(ltri_ref[...], (n_blk * (1.0 / ROW_ALIGN)).astype(BF16)) * ROW_ALIGN
        within = _dot(onehot.astype(BF16), before_ref[...])
        in_sorted = within + dst0[:, 0:1]
        in_buffer = within + loc0[:, 0:1]
        rows = [jnp.sum(jnp.where(hit, v, 0.0), axis=0, keepdims=True)
                for v in (in_sorted, in_buffer) for hit in (hit1, hit2)]
        rows.append(jnp.zeros((pos_ref.shape[1] - len(rows), tm), F32))
        pos_ref[0] = jnp.concatenate(rows, axis=0).astype(jnp.int32)
        tinfo_ref[0, 0] = dst0
        tinfo_ref[0, 1] = n_blk
        tinfo_ref[0, 2] = loc0
        carry_ref[...] += n_blk


def _rank(rt):
    n_tok_tiles, _, tm = rt.shape
    T = n_tok_tiles * tm
    idx = np.arange(tm)
    before = jnp.asarray(idx[:, None] < idx[None, :], BF16)
    ex = np.arange(N_EXPERTS)
    ltri = jnp.asarray(ex[None, :] < ex[:, None], BF16)
    stat = pltpu.VMEM((N_EXPERTS, LANES), F32)
    return pl.pallas_call(
        _rank_kernel,
        grid=(2, T // tm),
        in_specs=[pl.BlockSpec((1, 8, tm), lambda ph, i: (i, 0, 0)),
                  pl.BlockSpec((tm, tm), lambda ph, i: (0, 0)),
                  pl.BlockSpec((N_EXPERTS, N_EXPERTS), lambda ph, i: (0, 0))],
        out_specs=[pl.BlockSpec((1, 8, tm), lambda ph, i: (i * ph, 0, 0)),
                   pl.BlockSpec((2, N_EXPERTS, LANES), lambda ph, i: (0, 0, 0)),
                   pl.BlockSpec((1, 3, N_EXPERTS, LANES), lambda ph, i: (i * ph, 0, 0, 0))],
        out_shape=[jax.ShapeDtypeStruct((T // tm, 8, tm), jnp.int32),
                   jax.ShapeDtypeStruct((2, N_EXPERTS, LANES), F32),
                   jax.ShapeDtypeStruct((T // tm, 3, N_EXPERTS, LANES), F32)],
        scratch_shapes=[stat, stat, stat],
        compiler_params=_cparams(("arbitrary", "arbitrary")),
        name="moe_rank",
    )(rt, before, ltri)


def _block_copies(n, src_of, dst_of, sem, act):
    def whole(j, _):
        off = pl.multiple_of(j * COPY_CHUNK, COPY_CHUNK)
        act(pltpu.make_async_copy(src_of(off, COPY_CHUNK), dst_of(off, COPY_CHUNK), sem))
        return 0
    lax.fori_loop(0, n // COPY_CHUNK, whole, 0)
    for b in range(COPY_CHUNK.bit_length() - 2, ROW_ALIGN.bit_length() - 2, -1):
        size = 1 << b

        @pl.when((n >> b) & 1 == 1)
        def _():
            off = pl.multiple_of((n >> (b + 1)) << (b + 1), ROW_ALIGN)
            act(pltpu.make_async_copy(src_of(off, size), dst_of(off, size), sem))


def _dispatch_kernel(dst_ref, nblk_ref, loc_ref, tail0_ref, tailn_ref, nv_ref, h_ref, lpos_ref, xs_ref,
                     obuf, zbuf, sems, zsem):
    i = pl.program_id(0)
    n = pl.num_programs(0)
    nc, tm = obuf.shape[1], h_ref.shape[0]
    R = obuf.shape[2]
    slot = i % 2

    def tile_copies(tile, sl, act):
        def body(e, _):
            c = tile * N_EXPERTS + e
            loc = pl.multiple_of(loc_ref[c], ROW_ALIGN)
            dst = pl.multiple_of(dst_ref[c], ROW_ALIGN)
            _block_copies(nblk_ref[c],
                          lambda off, size: obuf.at[sl, :, pl.ds(loc + off, size)],
                          lambda off, size: xs_ref.at[:, pl.ds(dst + off, size)],
                          sems.at[sl], act)
            return 0
        lax.fori_loop(0, N_EXPERTS, body, 0)

    @pl.when(i >= 2)
    def _():
        tile_copies(i - 2, slot, lambda cp: cp.wait())

    hb = h_ref[...]
    lp = lpos_ref[0].astype(F32)
    l1, l2 = lp[2:3, :], lp[3:4, :]
    blk = MXU_DIM
    for jb in range(R // blk):
        r = (lax.broadcasted_iota(jnp.int32, (blk, tm), 0) + jb * blk).astype(F32)
        sel = jnp.where(r == l1, 1.0, jnp.where(r == l2, 1.0, 0.0)).astype(BF16)
        rows = _dot(sel, hb)
        for s in range(nc):
            obuf[slot, s, jb * blk:(jb + 1) * blk, :] = rows[:, s * LANES:(s + 1) * LANES]
    tile_copies(i, slot, lambda cp: cp.start())

    @pl.when(i == n - 1)
    def _():
        @pl.when(i >= 1)
        def _():
            tile_copies(i - 1, 1 - slot, lambda cp: cp.wait())
        tile_copies(i, slot, lambda cp: cp.wait())
        zbuf[...] = jnp.zeros_like(zbuf)

        def zero_fill(act):
            def body(e, _):
                t0 = pl.multiple_of(tail0_ref[e], ROW_ALIGN)
                _block_copies(tailn_ref[e], lambda off, size: zbuf.at[:, pl.ds(0, size)],
                              lambda off, size: xs_ref.at[:, pl.ds(t0 + off, size)], zsem, act)
                return 0
            lax.fori_loop(0, N_EXPERTS, body, 0)

            def unused(j, _):
                j0 = pl.multiple_of(j * MOE_TILE, MOE_TILE)
                act(pltpu.make_async_copy(zbuf, xs_ref.at[:, pl.ds(j0, MOE_TILE)], zsem))
                return 0
            lax.fori_loop(nv_ref[0], xs_ref.shape[1] // MOE_TILE, unused, 0)
        zero_fill(lambda cp: cp.start())
        zero_fill(lambda cp: cp.wait())


def _dispatch(dst0, nblk, loc0, tail0, tailn, n_valid, h, pos, n_rows):
    T, D = h.shape
    NC = D // LANES
    tm = pos.shape[2]
    R = 2 * tm + N_EXPERTS * ROW_ALIGN
    R = -(-R // MXU_DIM) * MXU_DIM
    return pl.pallas_call(
        _dispatch_kernel,
        grid_spec=pltpu.PrefetchScalarGridSpec(
            num_scalar_prefetch=6,
            grid=(T // tm,),
            in_specs=[pl.BlockSpec((tm, D), lambda i, *_: (i, 0)),
                      pl.BlockSpec((1, pos.shape[1], tm), lambda i, *_: (i, 0, 0))],
            out_specs=pl.BlockSpec(memory_space=pl.ANY),
            scratch_shapes=[pltpu.VMEM((2, NC, R, LANES), F32), pltpu.VMEM((NC, MOE_TILE, LANES), F32),
                            pltpu.SemaphoreType.DMA((2,)), pltpu.SemaphoreType.DMA]),
        out_shape=jax.ShapeDtypeStruct((NC, n_rows, LANES), F32),
        compiler_params=_cparams(("arbitrary",)),
        name="moe_dispatch",
    )(dst0, nblk, loc0, tail0, tailn, n_valid, h, pos)


def _experts_kernel(te_ref, nv_ref, xs_ref, w1_ref, w3_ref, w2_ref, ys_ref, w1b, w3b, w2b):
    j = pl.program_id(0)
    e = te_ref[j]
    e_prev = te_ref[jnp.maximum(j - 1, 0)]

    @pl.when((j == 0) | (e != e_prev))
    def _():
        w1b[...] = w1_ref[0].astype(BF16)
        w3b[...] = w3_ref[0].astype(BF16)
        w2b[...] = w2_ref[0].astype(BF16)

    @pl.when(j < nv_ref[0])
    def _():
        nc = xs_ref.shape[0]
        half = xs_ref.shape[1] // 2

        def up(rows):
            h = jnp.concatenate([xs_ref[s, rows, :] for s in range(nc)], axis=1).astype(BF16)
            return _dot(h, w1b[...]), _dot(h, w3b[...])

        def down(rows, ab):
            a, b = ab
            hid = (a * (1.0 / (1.0 + jnp.exp(-a))) * b).astype(BF16)
            y = _dot(hid, w2b[...])
            for s in range(nc):
                ys_ref[s, rows, :] = y[:, s * LANES:(s + 1) * LANES]

        top, bottom = slice(0, half), slice(half, 2 * half)
        ab_top = up(top)
        ab_bottom = up(bottom)
        down(top, ab_top)
        down(bottom, ab_bottom)


def _experts(tile_expert, n_valid, xs, w1, w3, w2):
    NC, P, _ = xs.shape
    NE, D, F = w1.shape
    nt = P // MOE_TILE
    tile = lambda j, te, nv: (0, jnp.minimum(j, nv[0] - 1), 0)
    wspec = lambda shape: pl.BlockSpec(shape, lambda j, te, nv: (te[j], 0, 0))
    return pl.pallas_call(
        _experts_kernel,
        grid_spec=pltpu.PrefetchScalarGridSpec(
            num_scalar_prefetch=2,
            grid=(nt,),
            in_specs=[pl.BlockSpec((NC, MOE_TILE, LANES), tile),
                      wspec((1, D, F)), wspec((1, D, F)), wspec((1, F, D))],
            out_specs=pl.BlockSpec((NC, MOE_TILE, LANES), tile),
            scratch_shapes=[pltpu.VMEM((D, F), BF16), pltpu.VMEM((D, F), BF16),
                            pltpu.VMEM((F, D), BF16)]),
        out_shape=jax.ShapeDtypeStruct((NC, P, LANES), F32),
        input_output_aliases={2: 0},
        compiler_params=_cparams(("arbitrary",)),
        name="moe_experts",
    )(tile_expert, n_valid, xs, w1, w3, w2)


def _combine_kernel(pos_ref, x_ref, rt_ref, ga_ref, ys_ref, o_ref, buf, sems):
    i = pl.program_id(0)
    n = pl.num_programs(0)
    tm = x_ref.shape[0]
    nc = buf.shape[1]

    def row_copy(tile, slot, t, k):
        src = ys_ref.at[:, pos_ref[tile * (2 * tm) + 2 * t + k]]
        return pltpu.make_async_copy(src, buf.at[slot, :, k * tm + t], sems.at[slot])

    def issue(tile, slot):
        def body(t, _):
            row_copy(tile, slot, t, 0).start(priority=0)
            row_copy(tile, slot, t, 1).start(priority=1)
            return 0
        lax.fori_loop(0, tm, body, 0, unroll=8)

    @pl.when(i == 0)
    def _():
        issue(0, 0)

    @pl.when(i + 1 < n)
    def _():
        issue(i + 1, (i + 1) % 2)

    slot = i % 2

    def drain(t, _):
        row_copy(i, slot, t, 0).wait()
        row_copy(i, slot, t, 1).wait()
        return 0

    lax.fori_loop(0, tm, drain, 0, unroll=8)
    rt_t = rt_ref[0]
    rt = jnp.concatenate([rt_t, jnp.zeros((LANES - rt_t.shape[0], tm), F32)], axis=0).T
    w1 = rt[:, RT_W:RT_W + 1]
    w2 = rt[:, RT_W + 1:RT_W + 2]
    for s in range(nc):
        cols = slice(s * LANES, (s + 1) * LANES)
        y = w1 * buf[slot, s, 0:tm, :] + w2 * buf[slot, s, tm:2 * tm, :]
        o_ref[:, cols] = x_ref[:, cols] + ga_ref[0][:, cols] * y


def _combine(pos_flat, x1, rt, ga, ys, tm=256):
    B, S, D = x1.shape
    NC = ys.shape[0]
    tm = min(tm, S)
    nt = S // tm
    per_rt = rt.shape[2] // tm
    return pl.pallas_call(
        _combine_kernel,
        grid_spec=pltpu.PrefetchScalarGridSpec(
            num_scalar_prefetch=1,
            grid=(B * nt,),
            in_specs=[pl.BlockSpec((tm, D), lambda i, pos: (i, 0)),
                      pl.BlockSpec((1, rt.shape[1], tm), lambda i, pos: (i // per_rt, 0, i % per_rt)),
                      pl.BlockSpec((1, 1, D), lambda i, pos: (i // nt, 0, 0)),
                      pl.BlockSpec(memory_space=pl.ANY)],
            out_specs=pl.BlockSpec((tm, D), lambda i, pos: (i, 0)),
            scratch_shapes=[pltpu.VMEM((2, NC, 2 * tm, LANES), F32),
                            pltpu.SemaphoreType.DMA((2,))]),
        out_shape=jax.ShapeDtypeStruct((B * S, D), F32),
        compiler_params=_cparams(("arbitrary",)),
        name="moe_combine",
    )(pos_flat, x1.reshape(B * S, D), rt, ga, ys).reshape(B, S, D)


def _moe(h3, rt, x1, ga, w1, w3, w2, first_expert):
    n_tok_tiles = rt.shape[0]
    T = n_tok_tiles * rt.shape[2]
    n_tiles = -(-(2 * T + n_tok_tiles * N_EXPERTS * ROW_ALIGN) // MOE_TILE) + N_EXPERTS
    pos, info, tinfo = _rank(rt)
    pos_flat = pos[:, :2, :].transpose(0, 2, 1).reshape(-1)
    counts = info[0, :, 0]
    starts = info[1, :, 0]
    ends = starts + jnp.ceil(counts * (1.0 / MOE_TILE))
    tile_ids = jnp.arange(n_tiles, dtype=F32)
    tile_expert = jnp.minimum(jnp.sum(tile_ids[:, None] >= ends[None, :], axis=1), N_EXPERTS - 1)
    n_valid = ends[N_EXPERTS - 1:].astype(jnp.int32)
    as_ints = lambda a: a.astype(jnp.int32).reshape(-1)
    xs = _dispatch(as_ints(tinfo[:, 0, :, 0]), as_ints(tinfo[:, 1, :, 0]),
                   as_ints(tinfo[:, 2, :, 0]), as_ints(starts * MOE_TILE + counts),
                   as_ints(ends * MOE_TILE - starts * MOE_TILE - counts), n_valid,
                   h3.reshape(T, h3.shape[-1]), pos, n_tiles * MOE_TILE)
    ys = _experts(tile_expert.astype(jnp.int32) + first_expert, n_valid, xs, w1, w3, w2)
    return _combine(pos_flat, x1, rt, ga, ys)


def kernel(x, c, w_mod, b_mod, g_norm1, w_in, gq_a, gk_a, lam_a, g_sub_a, w_pool, b_pool, pool_scale, gq_c, gk_c, w_out, g_norm2, w_rg, b_rg, w_re, b_re, w1, w3, w2):
    B, S, D = x.shape
    L = w_mod.shape[0]
    a_width = D // 2
    pool_width = D // 4
    c_width = D // 4
    W = MXU_DIM
    reps = W // HEAD_DIM

    inv = 1.0 / (ROPE_THETA ** (jnp.arange(0, HEAD_DIM, 2, dtype=F32) / HEAD_DIM))
    ang = jnp.arange(S, dtype=F32)[:, None] * inv[None, :]
    ang = jnp.concatenate([ang, ang], axis=-1)
    cos_h, sin_h = jnp.cos(ang), jnp.sin(ang)
    first = jnp.arange(HEAD_DIM) < HEAD_DIM // 2
    cos_t = jnp.tile(cos_h, (1, reps))
    sa_t = jnp.tile(jnp.where(first[None, :], 0.0, sin_h), (1, reps))
    sb_t = jnp.tile(jnp.where(first[None, :], -sin_h, 0.0), (1, reps))
    head_of = np.arange(W) // HEAD_DIM
    bd = jnp.asarray((head_of[:, None] == head_of[None, :]) / HEAD_DIM, BF16)

    qa0, ka0, va0 = 0, a_width, 2 * a_width
    ub0 = 3 * a_width
    qc0, kc0, vc0 = ub0 + pool_width, ub0 + pool_width + c_width, ub0 + pool_width + 2 * c_width
    chunks = lambda lo_, hi_: list(range(lo_ // W, hi_ // W))
    z_chunks = tuple([(j, True) for j in chunks(qa0, va0)] + [(j, False) for j in chunks(ub0, qc0)]
                     + [(j, True) for j in chunks(qc0, vc0)])
    v_chunks = tuple(chunks(va0, ub0) + chunks(vc0, w_in.shape[2]))
    zq_a, zk_a = 0, a_width // LANES
    z_ub = 2 * a_width // pool_width
    zq_c = (2 * a_width + pool_width) // LANES
    zk_c = zq_c + c_width // LANES
    scale = HEAD_DIM ** -0.5 * math.log2(math.e)
    bias_t = _dilated_tables(S)

    mod = _modulation(c, w_mod, b_mod)
    for l in range(L):
        sh1, sc1, ga1, sh2, sc2, ga2 = [m[:, None, :] for m in jnp.split(mod[l], N_MOD, axis=-1)]
        gain = jnp.ones((w_in.shape[2],), F32)
        gain = gain.at[qa0:ka0].set(jnp.tile(gq_a[l], a_width // HEAD_DIM) * scale)
        gain = gain.at[ka0:va0].set(jnp.tile(gk_a[l], a_width // HEAD_DIM))
        gain = gain.at[qc0:kc0].set(jnp.tile(gq_c[l], c_width // HEAD_DIM) * scale)
        gain = gain.at[kc0:vc0].set(jnp.tile(gk_c[l], c_width // HEAD_DIM))
        z, vt = _in_projection(x, sc1, sh1, g_norm1[l][None, :], w_in[l].astype(BF16),
                               gain[None, :], cos_t, sa_t, sb_t, bd, z_chunks, v_chunks)

        lam_init = 0.8 - 0.6 * math.exp(-0.3 * l)
        n_a = a_width // LANES
        ya = _attention("diff", z, vt, zq_a, zk_a, 0, n_a,
                        [lam_a[l], g_sub_a[l][:, None]],
                        [pl.BlockSpec(lam_a[l].shape, lambda b, h: (0, 0)),
                         pl.BlockSpec((LANES, 1), lambda b, h: (0, 0))], lam_init=lam_init)
        n_c = c_width // LANES
        yc = _attention("dil", z, vt, zq_c, zk_c, n_a, n_c,
                        [bias_t], [pl.BlockSpec(bias_t.shape, lambda b, h: (0, 0, 0))])
        w_bd = jax.scipy.linalg.block_diag(*[w_pool[l, g] for g in range(w_pool.shape[1])])
        yb = _pool_mixer(z, z_ub, w_bd.astype(BF16), b_pool[l].reshape(1, -1),
                         pool_scale[l][None, :])

        wr = jnp.zeros((LANES, D), F32)
        wr = wr.at[:N_EXPERTS].set(w_re[l].transpose(0, 2, 1).reshape(N_EXPERTS, D))
        wr = wr.at[N_EXPERTS:N_EXPERTS + N_GROUPS].set(w_rg[l].T)
        br = jnp.zeros((LANES, 1), F32)
        br = br.at[:N_EXPERTS, 0].set(b_re[l].reshape(-1))
        br = br.at[N_EXPERTS:N_EXPERTS + N_GROUPS, 0].set(b_rg[l])
        x1, h3, rt = _out_projection(ya, yb, yc, x, w_out[l].astype(BF16), ga1, sc2, sh2,
                                     g_norm2[l][None, :], wr.astype(BF16), br)
        F = w1.shape[-1]
        x = _moe(h3, rt, x1, ga2, w1.reshape(L * N_EXPERTS, D, F), w3.reshape(L * N_EXPERTS, D, F),
                 w2.reshape(L * N_EXPERTS, F, D), l * N_EXPERTS)
    return x
```

```python
import functools
import math

import jax
import jax.numpy as jnp
import numpy as np
from jax import lax
from jax.experimental import pallas as pl
from jax.experimental.pallas import tpu as pltpu

HEAD_DIM = 64
POOL_WINDOWS = (2, 4, 8, 16)
DILATED_PAIRS = ((128, 1), (512, 4), (2048, 16))
ROPE_THETA = 10000.0
N_GROUPS = 4
EXPERTS_PER_GROUP = 8
N_EXPERTS = N_GROUPS * EXPERTS_PER_GROUP
N_MOD = 6
EPS = 1e-6

LANES = 128
BF16_ROWS = 16
MXU_DIM = 256
KV_BLOCK = 256
Q_BLOCK = 2 * KV_BLOCK
MOE_TILE = 512
DISPATCH_TILE = 512
ROW_ALIGN = 8
COPY_CHUNK = 64
RT_E, RT_W = 0, 2
VMEM_LIMIT = 48 * 1024 * 1024
NEG = -1e30
F32 = jnp.float32
BF16 = jnp.bfloat16


def _cparams(sem):
    return pltpu.CompilerParams(dimension_semantics=sem, vmem_limit_bytes=VMEM_LIMIT)


def _dot(a, b):
    return jnp.dot(a, b, preferred_element_type=F32)


def _dot_nt(a, b):
    return lax.dot_general(a, b, (((1,), (1,)), ((), ())), preferred_element_type=F32)


def _mod_kernel(c_ref, w_ref, b_ref, o_ref):
    c = c_ref[...]
    cond = c * (1.0 / (1.0 + jnp.exp(-c)))
    o_ref[0] = _dot(cond.astype(BF16), w_ref[0].astype(BF16)) + b_ref[0]


def _modulation(c, w_mod, b_mod):
    L, D, N = w_mod.shape
    B = c.shape[0]
    tn = 1024
    return pl.pallas_call(
        _mod_kernel,
        grid=(L, N // tn),
        in_specs=[pl.BlockSpec((B, D), lambda l, j: (0, 0)),
                  pl.BlockSpec((1, D, tn), lambda l, j: (l, 0, j)),
                  pl.BlockSpec((1, 1, tn), lambda l, j: (l, 0, j))],
        out_specs=pl.BlockSpec((1, B, tn), lambda l, j: (l, 0, j)),
        out_shape=jax.ShapeDtypeStruct((L, B, N), F32),
        compiler_params=_cparams(("parallel", "parallel")),
        name="modulation",
    )(c, w_mod, b_mod.reshape(L, 1, N))


def _inproj_kernel(z_chunks, v_chunks, x_ref, sc_ref, sh_ref, g_ref, w_ref, gain_ref, cos_ref,
                   sa_ref, sb_ref, bd_ref, z_ref, vt_ref):
    x = x_ref[0]
    ms = jnp.mean(x * x, axis=-1, keepdims=True)
    h = x * lax.rsqrt(ms + EPS) * g_ref[...]
    h = h * (1.0 + sc_ref[0]) + sh_ref[0]
    hb = h.astype(BF16)
    W = MXU_DIM

    def project(src):
        return _dot(hb, w_ref[:, src * W:(src + 1) * W])

    def finish_z(dst, src, normed, zc):
        if normed:
            msq = _dot((zc * zc).astype(BF16), bd_ref[...])
            y = zc * lax.rsqrt(msq + EPS) * gain_ref[:, src * W:(src + 1) * W]
            r_up = pltpu.roll(y, HEAD_DIM // 2, 1)
            r_dn = pltpu.roll(y, W - HEAD_DIM // 2, 1)
            zc = y * cos_ref[...] + r_up * sa_ref[...] + r_dn * sb_ref[...]
        z_ref[0, :, dst * W:(dst + 1) * W] = zc.astype(BF16)

    def finish_v(dst, zc):
        zt = zc.T.astype(BF16)
        for cb in range(vt_ref.shape[1]):
            vt_ref[0, cb, dst * W:(dst + 1) * W, :] = zt[:, cb * KV_BLOCK:(cb + 1) * KV_BLOCK]

    work = ([(src, functools.partial(finish_z, dst, src, normed)) for dst, (src, normed) in enumerate(z_chunks)]
            + [(src, functools.partial(finish_v, dst)) for dst, src in enumerate(v_chunks)])
    raw = project(work[0][0])
    for idx, (_, finish) in enumerate(work):
        nxt = project(work[idx + 1][0]) if idx + 1 < len(work) else None
        finish(raw)
        raw = nxt


def _in_projection(x, sc, sh, g, w_bf, gain, cos_t, sa_t, sb_t, bd, z_chunks, v_chunks, tm=512):
    B, S, D = x.shape
    N = w_bf.shape[1]
    W = MXU_DIM
    nz, nv = len(z_chunks) * W, len(v_chunks) * W
    return pl.pallas_call(
        functools.partial(_inproj_kernel, z_chunks, v_chunks),
        grid=(B, S // tm),
        in_specs=[pl.BlockSpec((1, tm, D), lambda b, i: (b, i, 0)),
                  pl.BlockSpec((1, 1, D), lambda b, i: (b, 0, 0)),
                  pl.BlockSpec((1, 1, D), lambda b, i: (b, 0, 0)),
                  pl.BlockSpec((1, D), lambda b, i: (0, 0)),
                  pl.BlockSpec((D, N), lambda b, i: (0, 0)),
                  pl.BlockSpec((1, N), lambda b, i: (0, 0)),
                  pl.BlockSpec((tm, W), lambda b, i: (i, 0)),
                  pl.BlockSpec((tm, W), lambda b, i: (i, 0)),
                  pl.BlockSpec((tm, W), lambda b, i: (i, 0)),
                  pl.BlockSpec((W, W), lambda b, i: (0, 0))],
        out_specs=[pl.BlockSpec((1, tm, nz), lambda b, i: (b, i, 0)),
                   pl.BlockSpec((1, tm // KV_BLOCK, nv, KV_BLOCK), lambda b, i: (b, i, 0, 0))],
        out_shape=[jax.ShapeDtypeStruct((B, S, nz), BF16),
                   jax.ShapeDtypeStruct((B, S // KV_BLOCK, nv, KV_BLOCK), BF16)],
        compiler_params=_cparams(("parallel", "parallel")),
        name="in_projection",
    )(x, sc, sh, g, w_bf, gain, cos_t, sa_t, sb_t, bd)


def _attn_kernel(mode, lam_init, *refs):
    if mode == "diff":
        q_ref, k_ref, vt_ref, lam_ref, g_ref, o_ref, s_a, s_b, acc_ref = refs
    else:
        q_ref, k_ref, vt_ref, bias_ref, o_ref, s_a, s_b, acc_ref = refs
    bq, bk = Q_BLOCK, KV_BLOCK
    S = q_ref.shape[1]
    nq = S // bq
    lane = lax.broadcasted_iota(jnp.int32, (1, LANES), 1)
    lo = lane < HEAD_DIM
    extra = acc_ref.shape[0] - LANES
    ones_rows = jnp.where(lax.broadcasted_iota(jnp.int32, (extra, bk), 0) == 0, 1.0, 0.0).astype(BF16)

    def both_maps(qi):
        q = q_ref[0, pl.ds(pl.multiple_of(qi * bq, bq), bq), :]
        zero = jnp.zeros_like(q)
        return jnp.concatenate([jnp.where(lo, q, zero), jnp.where(lo, zero, q)], axis=0)

    def scores(s_ref, kb, q2):
        ks = pl.multiple_of(kb * bk, bk)
        s_ref[...] = _dot_nt(k_ref[0, pl.ds(ks, bk), :], q2)

    def q_block(qi, _):
        qs = pl.multiple_of(qi * bq, bq)
        q2 = both_maps(qi)

        def table(kb):
            if mode == "diff":
                return None
            return bias_ref[jnp.minimum(2 * qi - kb + 1, bias_ref.shape[0] - 1)]

        def update(s_ref, kb, stats, bias, first=0):
            vt1 = jnp.concatenate([vt_ref[0, kb], ones_rows], axis=0)
            out = []
            for half in range(2):
                cols = slice(half * bq + first, (half + 1) * bq)
                m = stats[half]
                s = s_ref[:, cols]
                if bias is not None:
                    s = s + bias[:, first:]
                mn = jnp.maximum(m[:, first:], jnp.max(s, axis=0, keepdims=True))
                p = jnp.exp2(s - mn)
                acc_ref[:, cols] = (jnp.exp2(m[:, first:] - mn) * acc_ref[:, cols]
                                    + _dot(vt1, p.astype(BF16)))
                out.append(jnp.concatenate([m[:, :first], mn], axis=1) if first else mn)
            return tuple(out)

        def pair(j, stats):
            scores(s_b, 2 * j + 1, q2)
            stats = update(s_a, 2 * j, stats, table(2 * j))
            scores(s_a, 2 * j + 2, q2)
            return update(s_b, 2 * j + 1, stats, table(2 * j + 1))

        acc_ref[...] = jnp.zeros_like(acc_ref)
        stats = lax.fori_loop(0, qi, pair, (jnp.full((1, bq), NEG, F32),) * 2)

        krow = lax.broadcasted_iota(jnp.int32, (bk, bq), 0)
        qcol = lax.broadcasted_iota(jnp.int32, (bk, bq), 1)
        scores(s_b, 2 * qi + 1, q2)
        for rel, s_ref in enumerate((s_a, s_b)):
            bias = table(2 * qi + rel)
            if mode == "diff":
                bias = jnp.where(krow + rel * bk <= qcol, 0.0, NEG).astype(F32)
            stats = update(s_ref, 2 * qi + rel, stats, bias, first=rel * bk)
            if rel == 0:
                scores(s_a, 0, both_maps(jnp.minimum(qi + 1, nq - 1)))
        a_lo, a_hi = acc_ref[0:LANES, :bq], acc_ref[0:LANES, bq:]
        l_lo, l_hi = acc_ref[LANES:LANES + 1, :bq], acc_ref[LANES:LANES + 1, bq:]
        if mode == "diff":
            lp = lam_ref[...]
            lam = (jnp.exp(jnp.sum(lp[0:1] * lp[1:2], axis=-1, keepdims=True))
                   - jnp.exp(jnp.sum(lp[2:3] * lp[3:4], axis=-1, keepdims=True)) + lam_init)
            y = a_lo * (1.0 / l_lo) - a_hi * (lam / l_hi)
            ms = jnp.mean(y * y, axis=0, keepdims=True)
            y = y * lax.rsqrt(ms + EPS) * (g_ref[...] * (1.0 - lam_init))
        else:
            feat = lax.broadcasted_iota(jnp.int32, (LANES, 1), 0)
            y = jnp.where(feat < HEAD_DIM, a_lo * (1.0 / l_lo), a_hi * (1.0 / l_hi))
        o_ref[0, pl.ds(qs, bq), :] = y.T.astype(o_ref.dtype)
        return 0

    scores(s_a, 0, both_maps(0))
    lax.fori_loop(0, nq, q_block, 0)


def _attention(mode, z, vt, q_col, k_col, v_row, n_blocks, extras, extra_specs, lam_init=0.0):
    B, S, _ = z.shape
    zspec = lambda col: pl.BlockSpec((1, S, LANES), lambda b, h: (b, 0, col + h))
    vspec = pl.BlockSpec((1, S // KV_BLOCK, LANES, KV_BLOCK), lambda b, h: (b, 0, v_row + h, 0))
    return pl.pallas_call(
        functools.partial(_attn_kernel, mode, lam_init),
        grid=(B, n_blocks),
        in_specs=[zspec(q_col), zspec(k_col), vspec] + extra_specs,
        out_specs=pl.BlockSpec((1, S, LANES), lambda b, h: (b, 0, h)),
        out_shape=jax.ShapeDtypeStruct((B, S, n_blocks * LANES), BF16),
        scratch_shapes=[pltpu.VMEM((KV_BLOCK, 2 * Q_BLOCK), F32), pltpu.VMEM((KV_BLOCK, 2 * Q_BLOCK), F32),
                        pltpu.VMEM((LANES + BF16_ROWS, 2 * Q_BLOCK), F32)],
        compiler_params=_cparams(("parallel", "parallel")),
        name=mode + "_attention",
    )(z, z, vt, *extras)


def _dilated_tables(S):
    bq, bk = Q_BLOCK, KV_BLOCK
    dds = np.arange(-1, S // bk)
    dist = (dds[:, None, None] * bk + np.arange(bq)[None, None, :] - np.arange(bk)[None, :, None])
    cnt = np.zeros(dist.shape, np.float32)
    for window, dil in DILATED_PAIRS:
        cnt += (dist >= 0) & (dist <= window) & (dist % dil == 0)
    n = len(dds)
    while n > 1 and np.array_equal(cnt[n - 1], cnt[n - 2]):
        n -= 1
    cnt = cnt[:n]
    return jnp.asarray(np.where(cnt > 0, np.log2(np.maximum(cnt, 1.0)), NEG).astype(np.float32))


def _pool_kernel(u_ref, w_ref, b_ref, sc_ref, o_ref):
    u = u_ref[0].astype(F32)
    S, C = u.shape
    row = lax.broadcasted_iota(jnp.int32, (S, C), 0)
    lane = lax.broadcasted_iota(jnp.int32, (S, C), 1)
    grp = lane // (C // len(POOL_WINDOWS))

    def shifted(a, k):
        return jnp.where(row >= k, pltpu.roll(a, k, 0), 0.0)

    acc = u
    win = jnp.zeros_like(u)
    width = 1
    for gi, w in enumerate(POOL_WINDOWS):
        while width < w:
            acc = acc + shifted(acc, width)
            width *= 2
        win = jnp.where(grp == gi, acc, win)
    wl = jnp.zeros_like(row)
    for gi, w in enumerate(POOL_WINDOWS):
        wl = jnp.where(grp == gi, w, wl)
    cnt = jnp.minimum(row + 1, wl).astype(F32)
    d = win / cnt - u
    y = _dot(d.astype(BF16), w_ref[...]) + b_ref[...]
    o_ref[0] = (y * sc_ref[...]).astype(o_ref.dtype)


def _pool_mixer(z, col_block, w_bd, b, scale):
    B, S, _ = z.shape
    C = w_bd.shape[0]
    return pl.pallas_call(
        _pool_kernel,
        grid=(B,),
        in_specs=[pl.BlockSpec((1, S, C), lambda b_: (b_, 0, col_block)),
                  pl.BlockSpec((C, C), lambda b_: (0, 0)),
                  pl.BlockSpec((1, C), lambda b_: (0, 0)),
                  pl.BlockSpec((1, C), lambda b_: (0, 0))],
        out_specs=pl.BlockSpec((1, S, C), lambda b_: (b_, 0, 0)),
        out_shape=jax.ShapeDtypeStruct((B, S, C), BF16),
        compiler_params=_cparams(("parallel",)),
        name="pool_mixer",
    )(z, w_bd, b, scale)


def _outproj_kernel(ya_ref, yb_ref, yc_ref, x_ref, wo_ref, ga_ref, sc_ref, sh_ref, g_ref, wr_ref,
                    br_ref, x1_ref, h_ref, rt_ref):
    na = ya_ref.shape[2]
    nb = yb_ref.shape[2]
    halves = [slice(r * (x_ref.shape[1] // 2), (r + 1) * (x_ref.shape[1] // 2)) for r in range(2)]

    def mix(rows):
        return (_dot(ya_ref[0, rows, :], wo_ref[0:na, :]) + _dot(yb_ref[0, rows, :], wo_ref[na:na + nb, :])
                + _dot(yc_ref[0, rows, :], wo_ref[na + nb:, :]))

    def residual_norm(rows, y):
        x1 = x_ref[0, rows, :] + ga_ref[0] * y
        x1_ref[0, rows, :] = x1
        ms = jnp.mean(x1 * x1, axis=-1, keepdims=True)
        h = x1 * lax.rsqrt(ms + EPS) * g_ref[...]
        h = h * (1.0 + sc_ref[0]) + sh_ref[0]
        hb = h.astype(BF16)
        h_ref[0, rows, :] = hb
        return hb

    ys = [mix(rows) for rows in halves]
    hb = jnp.concatenate([residual_norm(rows, y) for rows, y in zip(halves, ys)], axis=0)

    logits = _dot_nt(wr_ref[...], hb) + br_ref[...]
    tm = logits.shape[1]
    G, E = N_GROUPS, EXPERTS_PER_GROUP
    big = float(LANES)
    grow = lax.broadcasted_iota(jnp.int32, (8, tm), 0)
    grow_f = grow.astype(F32)
    gl = jnp.where(grow < G, logits[G * E:G * E + 8, :], -jnp.inf)
    gmax = jnp.max(gl, axis=0, keepdims=True)
    gidx = jnp.min(jnp.where(gl == gmax, grow_f, big), axis=0, keepdims=True)
    gsum = jnp.sum(jnp.where(grow < G, jnp.exp(gl - gmax), 0.0), axis=0, keepdims=True)
    g_w = 1.0 / gsum
    erow = lax.broadcasted_iota(jnp.int32, (G * E, tm), 0)
    erow_f = erow.astype(F32)
    el = jnp.where((erow // E).astype(F32) == gidx, logits[0:G * E, :], -jnp.inf)
    v1 = jnp.max(el, axis=0, keepdims=True)
    i1 = jnp.min(jnp.where(el == v1, erow_f, big), axis=0, keepdims=True)
    el2 = jnp.where(erow_f == i1, -jnp.inf, el)
    v2 = jnp.max(el2, axis=0, keepdims=True)
    i2 = jnp.min(jnp.where(el2 == v2, erow_f, big), axis=0, keepdims=True)
    t = jnp.exp(v2 - v1)
    w1 = g_w / (1.0 + t)
    w2 = w1 * t
    rt_ref[0] = jnp.concatenate([i1, i2, w1, w2, jnp.zeros((rt_ref.shape[1] - 4, tm), F32)], axis=0)


def _out_projection(ya, yb, yc, x, wo_bf, ga, sc, sh, g, wr, br, tm=DISPATCH_TILE):
    B, S, D = x.shape
    nt = S // tm
    tok = lambda n: pl.BlockSpec((1, tm, n), lambda b, i: (b, i, 0))
    per_b = pl.BlockSpec((1, 1, D), lambda b, i: (b, 0, 0))
    const = lambda shape: pl.BlockSpec(shape, lambda b, i: (0, 0))
    return pl.pallas_call(
        _outproj_kernel,
        grid=(B, nt),
        in_specs=[tok(ya.shape[2]), tok(yb.shape[2]), tok(yc.shape[2]), tok(D), const((D, D)),
                  per_b, per_b, per_b, const((1, D)), const((LANES, D)), const((LANES, 1))],
        out_specs=[tok(D), tok(D),
                   pl.BlockSpec((1, 8, tm), lambda b, i: (b * nt + i, 0, 0))],
        out_shape=[jax.ShapeDtypeStruct((B, S, D), F32),
                   jax.ShapeDtypeStruct((B, S, D), BF16),
                   jax.ShapeDtypeStruct((B * nt, 8, tm), F32)],
        compiler_params=_cparams(("parallel", "parallel")),
        name="out_projection",
    )(ya, yb, yc, x, wo_bf, ga, sc, sh, g, wr, br)


def _rank_kernel(rt_ref, before_ref, ltri_ref, pos_ref, info_ref, tinfo_ref, cnt_ref, off_ref, carry_ref):
    ph, i = pl.program_id(0), pl.program_id(1)
    ne = N_EXPERTS
    rt_t = rt_ref[0]
    tm = rt_t.shape[1]
    expert = lax.broadcasted_iota(jnp.int32, (ne, tm), 0).astype(F32)
    hit1 = expert == rt_t[RT_E:RT_E + 1, :]
    hit2 = expert == rt_t[RT_E + 1:RT_E + 2, :]
    onehot = jnp.where(hit1, 1.0, jnp.where(hit2, 1.0, 0.0))
    n_blk = jnp.ceil(jnp.sum(onehot, axis=1, keepdims=True) * (1.0 / ROW_ALIGN)) * ROW_ALIGN
    n_blk = jnp.broadcast_to(n_blk, (ne, LANES))

    @pl.when((ph == 0) & (i == 0))
    def _():
        cnt_ref[...] = jnp.zeros_like(cnt_ref)

    @pl.when(ph == 0)
    def _():
        cnt_ref[...] += n_blk

    @pl.when((ph == 1) & (i == 0))
    def _():
        ntile = jnp.ceil(cnt_ref[...] * (1.0 / MOE_TILE))
        off_tiles = _dot(ltri_ref[...], ntile.astype(BF16))
        off_ref[...] = off_tiles * MOE_TILE
        info_ref[0] = cnt_ref[...]
        info_ref[1] = off_tiles
        carry_ref[...] = jnp.zeros_like(carry_ref)

    @pl.when(ph == 1)
    def _():
        dst0 = carry_ref[...] + off_ref[...]
        loc0 = _dot(ltri_ref[...], (n_blk * (1.0 / ROW_ALIGN)).astype(BF16)) * ROW_ALIGN
        within = _dot(onehot.astype(BF16), before_ref[...])
        in_sorted = within + dst0[:, 0:1]
        in_buffer = within + loc0[:, 0:1]
        rows = [jnp.sum(jnp.where(hit, v, 0.0), axis=0, keepdims=True)
                for v in (in_sorted, in_buffer) for hit in (hit1, hit2)]
        rows.append(jnp.zeros((pos_ref.shape[1] - len(rows), tm), F32))
        pos_ref[0] = jnp.concatenate(rows, axis=0).astype(jnp.int32)
        tinfo_ref[0, 0] = dst0
        tinfo_ref[0, 1] = n_blk
        tinfo_ref[0, 2] = loc0
        carry_ref[...] += n_blk


def _rank(rt):
    n_tok_tiles, _, tm = rt.shape
    T = n_tok_tiles * tm
    idx = np.arange(tm)
    before = jnp.asarray(idx[:, None] < idx[None, :], BF16)
    ex = np.arange(N_EXPERTS)
    ltri = jnp.asarray(ex[None, :] < ex[:, None], BF16)
    stat = pltpu.VMEM((N_EXPERTS, LANES), F32)
    return pl.pallas_call(
        _rank_kernel,
        grid=(2, T // tm),
        in_specs=[pl.BlockSpec((1, 8, tm), lambda ph, i: (i, 0, 0)),
                  pl.BlockSpec((tm, tm), lambda ph, i: (0, 0)),
                  pl.BlockSpec((N_EXPERTS, N_EXPERTS), lambda ph, i: (0, 0))],
        out_specs=[pl.BlockSpec((1, 8, tm), lambda ph, i: (i * ph, 0, 0)),
                   pl.BlockSpec((2, N_EXPERTS, LANES), lambda ph, i: (0, 0, 0)),
                   pl.BlockSpec((1, 3, N_EXPERTS, LANES), lambda ph, i: (i * ph, 0, 0, 0))],
        out_shape=[jax.ShapeDtypeStruct((T // tm, 8, tm), jnp.int32),
                   jax.ShapeDtypeStruct((2, N_EXPERTS, LANES), F32),
                   jax.ShapeDtypeStruct((T // tm, 3, N_EXPERTS, LANES), F32)],
        scratch_shapes=[stat, stat, stat],
        compiler_params=_cparams(("arbitrary", "arbitrary")),
        name="moe_rank",
    )(rt, before, ltri)


def _block_copies(n, src_of, dst_of, sem, act):
    def whole(j, _):
        off = pl.multiple_of(j * COPY_CHUNK, COPY_CHUNK)
        act(pltpu.make_async_copy(src_of(off, COPY_CHUNK), dst_of(off, COPY_CHUNK), sem))
        return 0
    lax.fori_loop(0, n // COPY_CHUNK, whole, 0)
    for b in range(COPY_CHUNK.bit_length() - 2, ROW_ALIGN.bit_length() - 2, -1):
        size = 1 << b

        @pl.when((n >> b) & 1 == 1)
        def _():
            off = pl.multiple_of((n >> (b + 1)) << (b + 1), ROW_ALIGN)
            act(pltpu.make_async_copy(src_of(off, size), dst_of(off, size), sem))


def _dispatch_kernel(dst_ref, nblk_ref, loc_ref, tail0_ref, tailn_ref, nv_ref, h_ref, lpos_ref, xs_ref,
                     obuf, zbuf, sems, zsem):
    i = pl.program_id(0)
    n = pl.num_programs(0)
    nc, tm = obuf.shape[1], h_ref.shape[0]
    R = obuf.shape[2]
    slot = i % 2

    def tile_copies(tile, sl, act):
        def body(e, _):
            c = tile * N_EXPERTS + e
            loc = pl.multiple_of(loc_ref[c], ROW_ALIGN)
            dst = pl.multiple_of(dst_ref[c], ROW_ALIGN)
            _block_copies(nblk_ref[c],
                          lambda off, size: obuf.at[sl, :, pl.ds(loc + off, size)],
                          lambda off, size: xs_ref.at[:, pl.ds(dst + off, size)],
                          sems.at[sl], act)
            return 0
        lax.fori_loop(0, N_EXPERTS, body, 0)

    def zero_fill(act):
        def body(e, _):
            t0 = pl.multiple_of(tail0_ref[e], ROW_ALIGN)
            _block_copies(tailn_ref[e], lambda off, size: zbuf.at[:, pl.ds(0, size)],
                          lambda off, size: xs_ref.at[:, pl.ds(t0 + off, size)], zsem, act)
            return 0
        lax.fori_loop(0, N_EXPERTS, body, 0)

        def unused(j, _):
            j0 = pl.multiple_of(j * MOE_TILE, MOE_TILE)
            act(pltpu.make_async_copy(zbuf, xs_ref.at[:, pl.ds(j0, MOE_TILE)], zsem))
            return 0
        lax.fori_loop(nv_ref[0], xs_ref.shape[1] // MOE_TILE, unused, 0)

    @pl.when(i == 0)
    def _():
        zbuf[...] = jnp.zeros_like(zbuf)
        zero_fill(lambda cp: cp.start())

    @pl.when(i >= 2)
    def _():
        tile_copies(i - 2, slot, lambda cp: cp.wait())

    hb = h_ref[...]
    lp = lpos_ref[0].astype(F32)
    l1, l2 = lp[2:3, :], lp[3:4, :]
    blk = MXU_DIM
    for jb in range(R // blk):
        r = (lax.broadcasted_iota(jnp.int32, (blk, tm), 0) + jb * blk).astype(F32)
        sel = jnp.where(r == l1, 1.0, jnp.where(r == l2, 1.0, 0.0)).astype(BF16)
        rows = _dot(sel, hb)
        for s in range(nc):
            obuf[slot, s, jb * blk:(jb + 1) * blk, :] = rows[:, s * LANES:(s + 1) * LANES]
    tile_copies(i, slot, lambda cp: cp.start())

    @pl.when(i == n - 1)
    def _():
        @pl.when(i >= 1)
        def _():
            tile_copies(i - 1, 1 - slot, lambda cp: cp.wait())
        tile_copies(i, slot, lambda cp: cp.wait())
        zero_fill(lambda cp: cp.wait())


def _dispatch(dst0, nblk, loc0, tail0, tailn, n_valid, h, pos, n_rows):
    T, D = h.shape
    NC = D // LANES
    tm = pos.shape[2]
    R = 2 * tm + N_EXPERTS * ROW_ALIGN
    R = -(-R // MXU_DIM) * MXU_DIM
    return pl.pallas_call(
        _dispatch_kernel,
        grid_spec=pltpu.PrefetchScalarGridSpec(
            num_scalar_prefetch=6,
            grid=(T // tm,),
            in_specs=[pl.BlockSpec((tm, D), lambda i, *_: (i, 0)),
                      pl.BlockSpec((1, pos.shape[1], tm), lambda i, *_: (i, 0, 0))],
            out_specs=pl.BlockSpec(memory_space=pl.ANY),
            scratch_shapes=[pltpu.VMEM((2, NC, R, LANES), F32), pltpu.VMEM((NC, MOE_TILE, LANES), F32),
                            pltpu.SemaphoreType.DMA((2,)), pltpu.SemaphoreType.DMA]),
        out_shape=jax.ShapeDtypeStruct((NC, n_rows, LANES), F32),
        compiler_params=_cparams(("arbitrary",)),
        name="moe_dispatch",
    )(dst0, nblk, loc0, tail0, tailn, n_valid, h, pos)


def _experts_kernel(te_ref, nv_ref, xs_ref, w1_ref, w3_ref, w2_ref, ys_ref, w1b, w3b, w2b):
    j = pl.program_id(0)
    e = te_ref[j]
    e_prev = te_ref[jnp.maximum(j - 1, 0)]

    @pl.when((j == 0) | (e != e_prev))
    def _():
        w1b[...] = w1_ref[0].astype(BF16)
        w3b[...] = w3_ref[0].astype(BF16)
        w2b[...] = w2_ref[0].astype(BF16)

    @pl.when(j < nv_ref[0])
    def _():
        nc = xs_ref.shape[0]
        half = xs_ref.shape[1] // 2

        def up(rows):
            h = jnp.concatenate([xs_ref[s, rows, :] for s in range(nc)], axis=1).astype(BF16)
            return _dot(h, w1b[...]), _dot(h, w3b[...])

        def down(rows, ab):
            a, b = ab
            hid = (a * (1.0 / (1.0 + jnp.exp(-a))) * b).astype(BF16)
            y = _dot(hid, w2b[...])
            for s in range(nc):
                ys_ref[s, rows, :] = y[:, s * LANES:(s + 1) * LANES]

        top, bottom = slice(0, half), slice(half, 2 * half)
        ab_top = up(top)
        ab_bottom = up(bottom)
        down(top, ab_top)
        down(bottom, ab_bottom)


def _experts(tile_expert, n_valid, xs, w1, w3, w2):
    NC, P, _ = xs.shape
    NE, D, F = w1.shape
    nt = P // MOE_TILE
    tile = lambda j, te, nv: (0, jnp.minimum(j, nv[0] - 1), 0)
    wspec = lambda shape: pl.BlockSpec(shape, lambda j, te, nv: (te[j], 0, 0))
    return pl.pallas_call(
        _experts_kernel,
        grid_spec=pltpu.PrefetchScalarGridSpec(
            num_scalar_prefetch=2,
            grid=(nt,),
            in_specs=[pl.BlockSpec((NC, MOE_TILE, LANES), tile),
                      wspec((1, D, F)), wspec((1, D, F)), wspec((1, F, D))],
            out_specs=pl.BlockSpec((NC, MOE_TILE, LANES), tile),
            scratch_shapes=[pltpu.VMEM((D, F), BF16), pltpu.VMEM((D, F), BF16),
                            pltpu.VMEM((F, D), BF16)]),
        out_shape=jax.ShapeDtypeStruct((NC, P, LANES), F32),
        input_output_aliases={2: 0},
        compiler_params=_cparams(("arbitrary",)),
        name="moe_experts",
    )(tile_expert, n_valid, xs, w1, w3, w2)


def _combine_kernel(pos_ref, x_ref, rt_ref, ga_ref, ys_ref, o_ref, buf, sems):
    i = pl.program_id(0)
    n = pl.num_programs(0)
    tm = x_ref.shape[0]
    nc = buf.shape[1]

    def row_copy(tile, slot, t, k):
        src = ys_ref.at[:, pos_ref[tile * (2 * tm) + 2 * t + k]]
        return pltpu.make_async_copy(src, buf.at[slot, :, k * tm + t], sems.at[slot])

    def issue(tile, slot):
        def body(t, _):
            row_copy(tile, slot, t, 0).start(priority=0)
            row_copy(tile, slot, t, 1).start(priority=1)
            return 0
        lax.fori_loop(0, tm, body, 0, unroll=8)

    @pl.when(i == 0)
    def _():
        issue(0, 0)

    @pl.when(i + 1 < n)
    def _():
        issue(i + 1, (i + 1) % 2)

    slot = i % 2

    def drain(t, _):
        row_copy(i, slot, t, 0).wait()
        row_copy(i, slot, t, 1).wait()
        return 0

    lax.fori_loop(0, tm, drain, 0, unroll=8)
    rt_t = rt_ref[0]
    rt = jnp.concatenate([rt_t, jnp.zeros((LANES - rt_t.shape[0], tm), F32)], axis=0).T
    w1 = rt[:, RT_W:RT_W + 1]
    w2 = rt[:, RT_W + 1:RT_W + 2]
    for s in range(nc):
        cols = slice(s * LANES, (s + 1) * LANES)
        y = w1 * buf[slot, s, 0:tm, :] + w2 * buf[slot, s, tm:2 * tm, :]
        o_ref[:, cols] = x_ref[:, cols] + ga_ref[0][:, cols] * y


def _combine(pos_flat, x1, rt, ga, ys, tm=256):
    B, S, D = x1.shape
    NC = ys.shape[0]
    tm = min(tm, S)
    nt = S // tm
    per_rt = rt.shape[2] // tm
    return pl.pallas_call(
        _combine_kernel,
        grid_spec=pltpu.PrefetchScalarGridSpec(
            num_scalar_prefetch=1,
            grid=(B * nt,),
            in_specs=[pl.BlockSpec((tm, D), lambda i, pos: (i, 0)),
                      pl.BlockSpec((1, rt.shape[1], tm), lambda i, pos: (i // per_rt, 0, i % per_rt)),
                      pl.BlockSpec((1, 1, D), lambda i, pos: (i // nt, 0, 0)),
                      pl.BlockSpec(memory_space=pl.ANY)],
            out_specs=pl.BlockSpec((tm, D), lambda i, pos: (i, 0)),
            scratch_shapes=[pltpu.VMEM((2, NC, 2 * tm, LANES), F32),
                            pltpu.SemaphoreType.DMA((2,))]),
        out_shape=jax.ShapeDtypeStruct((B * S, D), F32),
        compiler_params=_cparams(("arbitrary",)),
        name="moe_combine",
    )(pos_flat, x1.reshape(B * S, D), rt, ga, ys).reshape(B, S, D)


def _moe(h3, rt, x1, ga, w1, w3, w2, first_expert):
    n_tok_tiles = rt.shape[0]
    T = n_tok_tiles * rt.shape[2]
    n_tiles = -(-(2 * T + n_tok_tiles * N_EXPERTS * ROW_ALIGN) // MOE_TILE) + N_EXPERTS
    pos, info, tinfo = _rank(rt)
    pos_flat = pos[:, :2, :].transpose(0, 2, 1).reshape(-1)
    counts = info[0, :, 0]
    starts = info[1, :, 0]
    ends = starts + jnp.ceil(counts * (1.0 / MOE_TILE))
    tile_ids = jnp.arange(n_tiles, dtype=F32)
    tile_expert = jnp.minimum(jnp.sum(tile_ids[:, None] >= ends[None, :], axis=1), N_EXPERTS - 1)
    n_valid = ends[N_EXPERTS - 1:].astype(jnp.int32)
    as_ints = lambda a: a.astype(jnp.int32).reshape(-1)
    xs = _dispatch(as_ints(tinfo[:, 0, :, 0]), as_ints(tinfo[:, 1, :, 0]),
                   as_ints(tinfo[:, 2, :, 0]), as_ints(starts * MOE_TILE + counts),
                   as_ints(ends * MOE_TILE - starts * MOE_TILE - counts), n_valid,
                   h3.reshape(T, h3.shape[-1]), pos, n_tiles * MOE_TILE)
    ys = _experts(tile_expert.astype(jnp.int32) + first_expert, n_valid, xs, w1, w3, w2)
    return _combine(pos_flat, x1, rt, ga, ys)


def kernel(x, c, w_mod, b_mod, g_norm1, w_in, gq_a, gk_a, lam_a, g_sub_a, w_pool, b_pool, pool_scale, gq_c, gk_c, w_out, g_norm2, w_rg, b_rg, w_re, b_re, w1, w3, w2):
    B, S, D = x.shape
    L = w_mod.shape[0]
    a_width = D // 2
    pool_width = D // 4
    c_width = D // 4
    W = MXU_DIM
    reps = W // HEAD_DIM

    inv = 1.0 / (ROPE_THETA ** (jnp.arange(0, HEAD_DIM, 2, dtype=F32) / HEAD_DIM))
    ang = jnp.arange(S, dtype=F32)[:, None] * inv[None, :]
    ang = jnp.concatenate([ang, ang], axis=-1)
    cos_h, sin_h = jnp.cos(ang), jnp.sin(ang)
    first = jnp.arange(HEAD_DIM) < HEAD_DIM // 2
    cos_t = jnp.tile(cos_h, (1, reps))
    sa_t = jnp.tile(jnp.where(first[None, :], 0.0, sin_h), (1, reps))
    sb_t = jnp.tile(jnp.where(first[None, :], -sin_h, 0.0), (1, reps))
    head_of = np.arange(W) // HEAD_DIM
    bd = jnp.asarray((head_of[:, None] == head_of[None, :]) / HEAD_DIM, BF16)

    qa0, ka0, va0 = 0, a_width, 2 * a_width
    ub0 = 3 * a_width
    qc0, kc0, vc0 = ub0 + pool_width, ub0 + pool_width + c_width, ub0 + pool_width + 2 * c_width
    chunks = lambda lo_, hi_: list(range(lo_ // W, hi_ // W))
    z_chunks = tuple([(j, True) for j in chunks(qa0, va0)] + [(j, False) for j in chunks(ub0, qc0)]
                     + [(j, True) for j in chunks(qc0, vc0)])
    v_chunks = tuple(chunks(va0, ub0) + chunks(vc0, w_in.shape[2]))
    zq_a, zk_a = 0, a_width // LANES
    z_ub = 2 * a_width // pool_width
    zq_c = (2 * a_width + pool_width) // LANES
    zk_c = zq_c + c_width // LANES
    scale = HEAD_DIM ** -0.5 * math.log2(math.e)
    bias_t = _dilated_tables(S)

    mod = _modulation(c, w_mod, b_mod)
    for l in range(L):
        sh1, sc1, ga1, sh2, sc2, ga2 = [m[:, None, :] for m in jnp.split(mod[l], N_MOD, axis=-1)]
        gain = jnp.ones((w_in.shape[2],), F32)
        gain = gain.at[qa0:ka0].set(jnp.tile(gq_a[l], a_width // HEAD_DIM) * scale)
        gain = gain.at[ka0:va0].set(jnp.tile(gk_a[l], a_width // HEAD_DIM))
        gain = gain.at[qc0:kc0].set(jnp.tile(gq_c[l], c_width // HEAD_DIM) * scale)
        gain = gain.at[kc0:vc0].set(jnp.tile(gk_c[l], c_width // HEAD_DIM))
        z, vt = _in_projection(x, sc1, sh1, g_norm1[l][None, :], w_in[l].astype(BF16),
                               gain[None, :], cos_t, sa_t, sb_t, bd, z_chunks, v_chunks)

        lam_init = 0.8 - 0.6 * math.exp(-0.3 * l)
        n_a = a_width // LANES
        ya = _attention("diff", z, vt, zq_a, zk_a, 0, n_a,
                        [lam_a[l], g_sub_a[l][:, None]],
                        [pl.BlockSpec(lam_a[l].shape, lambda b, h: (0, 0)),
                         pl.BlockSpec((LANES, 1), lambda b, h: (0, 0))], lam_init=lam_init)
        n_c = c_width // LANES
        yc = _attention("dil", z, vt, zq_c, zk_c, n_a, n_c,
                        [bias_t], [pl.BlockSpec(bias_t.shape, lambda b, h: (0, 0, 0))])
        w_bd = jax.scipy.linalg.block_diag(*[w_pool[l, g] for g in range(w_pool.shape[1])])
        yb = _pool_mixer(z, z_ub, w_bd.astype(BF16), b_pool[l].reshape(1, -1),
                         pool_scale[l][None, :])

        wr = jnp.zeros((LANES, D), F32)
        wr = wr.at[:N_EXPERTS].set(w_re[l].transpose(0, 2, 1).reshape(N_EXPERTS, D))
        wr = wr.at[N_EXPERTS:N_EXPERTS + N_GROUPS].set(w_rg[l].T)
        br = jnp.zeros((LANES, 1), F32)
        br = br.at[:N_EXPERTS, 0].set(b_re[l].reshape(-1))
        br = br.at[N_EXPERTS:N_EXPERTS + N_GROUPS, 0].set(b_rg[l])
        x1, h3, rt = _out_projection(ya, yb, yc, x, w_out[l].astype(BF16), ga1, sc2, sh2,
                                     g_norm2[l][None, :], wr.astype(BF16), br)
        F = w1.shape[-1]
        x = _moe(h3, rt, x1, ga2, w1.reshape(L * N_EXPERTS, D, F), w3.reshape(L * N_EXPERTS, D, F),
                 w2.reshape(L * N_EXPERTS, F, D), l * N_EXPERTS)
    return x
```

```python
import functools
import math

import jax
import jax.numpy as jnp
import numpy as np
from jax import lax
from jax.experimental import pallas as pl
from jax.experimental.pallas import tpu as pltpu

HEAD_DIM = 64
POOL_WINDOWS = (2, 4, 8, 16)
DILATED_PAIRS = ((128, 1), (512, 4), (2048, 16))
ROPE_THETA = 10000.0
N_GROUPS = 4
EXPERTS_PER_GROUP = 8
N_EXPERTS = N_GROUPS * EXPERTS_PER_GROUP
N_MOD = 6
EPS = 1e-6

LANES = 128
BF16_ROWS = 16
MXU_DIM = 256
KV_BLOCK = 256
Q_BLOCK = 2 * KV_BLOCK
MOE_TILE = 512
DISPATCH_TILE = 512
ROW_ALIGN = 8
COPY_CHUNK = 64
RT_E, RT_W = 0, 2
VMEM_LIMIT = 48 * 1024 * 1024
NEG = -1e30
F32 = jnp.float32
BF16 = jnp.bfloat16


def _cparams(sem):
    return pltpu.CompilerParams(dimension_semantics=sem, vmem_limit_bytes=VMEM_LIMIT)


def _dot(a, b):
    return jnp.dot(a, b, preferred_element_type=F32)


def _dot_nt(a, b):
    return lax.dot_general(a, b, (((1,), (1,)), ((), ())), preferred_element_type=F32)


def _mod_kernel(c_ref, w_ref, b_ref, o_ref):
    c = c_ref[...]
    cond = c * (1.0 / (1.0 + jnp.exp(-c)))
    o_ref[0] = _dot(cond.astype(BF16), w_ref[0].astype(BF16)) + b_ref[0]


def _modulation(c, w_mod, b_mod):
    L, D, N = w_mod.shape
    B = c.shape[0]
    tn = 1024
    return pl.pallas_call(
        _mod_kernel,
        grid=(L, N // tn),
        in_specs=[pl.BlockSpec((B, D), lambda l, j: (0, 0)),
                  pl.BlockSpec((1, D, tn), lambda l, j: (l, 0, j)),
                  pl.BlockSpec((1, 1, tn), lambda l, j: (l, 0, j))],
        out_specs=pl.BlockSpec((1, B, tn), lambda l, j: (l, 0, j)),
        out_shape=jax.ShapeDtypeStruct((L, B, N), F32),
        compiler_params=_cparams(("parallel", "parallel")),
        name="modulation",
    )(c, w_mod, b_mod.reshape(L, 1, N))


def _inproj_kernel(z_chunks, v_chunks, x_ref, sc_ref, sh_ref, g_ref, w_ref, gain_ref, cos_ref,
                   sa_ref, sb_ref, bd_ref, z_ref, vt_ref):
    x = x_ref[0]
    ms = jnp.mean(x * x, axis=-1, keepdims=True)
    h = x * lax.rsqrt(ms + EPS) * g_ref[...]
    h = h * (1.0 + sc_ref[0]) + sh_ref[0]
    hb = h.astype(BF16)
    W = MXU_DIM

    def project(src):
        return _dot(hb, w_ref[:, src * W:(src + 1) * W])

    def finish_z(dst, src, normed, zc):
        if normed:
            msq = _dot((zc * zc).astype(BF16), bd_ref[...])
            y = zc * lax.rsqrt(msq + EPS) * gain_ref[:, src * W:(src + 1) * W]
            r_up = pltpu.roll(y, HEAD_DIM // 2, 1)
            r_dn = pltpu.roll(y, W - HEAD_DIM // 2, 1)
            zc = y * cos_ref[...] + r_up * sa_ref[...] + r_dn * sb_ref[...]
        z_ref[0, :, dst * W:(dst + 1) * W] = zc.astype(BF16)

    def finish_v(dst, zc):
        zt = zc.T.astype(BF16)
        for cb in range(vt_ref.shape[1]):
            vt_ref[0, cb, dst * W:(dst + 1) * W, :] = zt[:, cb * KV_BLOCK:(cb + 1) * KV_BLOCK]

    work = ([(src, functools.partial(finish_z, dst, src, normed)) for dst, (src, normed) in enumerate(z_chunks)]
            + [(src, functools.partial(finish_v, dst)) for dst, src in enumerate(v_chunks)])
    raw = project(work[0][0])
    for idx, (_, finish) in enumerate(work):
        nxt = project(work[idx + 1][0]) if idx + 1 < len(work) else None
        finish(raw)
        raw = nxt


def _in_projection(x, sc, sh, g, w_bf, gain, cos_t, sa_t, sb_t, bd, z_chunks, v_chunks, tm=1024):
    B, S, D = x.shape
    tm = min(tm, S)
    N = w_bf.shape[1]
    W = MXU_DIM
    nz, nv = len(z_chunks) * W, len(v_chunks) * W
    return pl.pallas_call(
        functools.partial(_inproj_kernel, z_chunks, v_chunks),
        grid=(B, S // tm),
        in_specs=[pl.BlockSpec((1, tm, D), lambda b, i: (b, i, 0)),
                  pl.BlockSpec((1, 1, D), lambda b, i: (b, 0, 0)),
                  pl.BlockSpec((1, 1, D), lambda b, i: (b, 0, 0)),
                  pl.BlockSpec((1, D), lambda b, i: (0, 0)),
                  pl.BlockSpec((D, N), lambda b, i: (0, 0)),
                  pl.BlockSpec((1, N), lambda b, i: (0, 0)),
                  pl.BlockSpec((tm, W), lambda b, i: (i, 0)),
                  pl.BlockSpec((tm, W), lambda b, i: (i, 0)),
                  pl.BlockSpec((tm, W), lambda b, i: (i, 0)),
                  pl.BlockSpec((W, W), lambda b, i: (0, 0))],
        out_specs=[pl.BlockSpec((1, tm, nz), lambda b, i: (b, i, 0)),
                   pl.BlockSpec((1, tm // KV_BLOCK, nv, KV_BLOCK), lambda b, i: (b, i, 0, 0))],
        out_shape=[jax.ShapeDtypeStruct((B, S, nz), BF16),
                   jax.ShapeDtypeStruct((B, S // KV_BLOCK, nv, KV_BLOCK), BF16)],
        compiler_params=_cparams(("parallel", "parallel")),
        name="in_projection",
    )(x, sc, sh, g, w_bf, gain, cos_t, sa_t, sb_t, bd)


def _attn_kernel(mode, lam_init, *refs):
    if mode == "diff":
        q_ref, k_ref, vt_ref, lam_ref, g_ref, o_ref, s_a, s_b, acc_ref = refs
    else:
        q_ref, k_ref, vt_ref, bias_ref, o_ref, s_a, s_b, acc_ref = refs
    bq, bk = Q_BLOCK, KV_BLOCK
    S = q_ref.shape[1]
    nq = S // bq
    lane = lax.broadcasted_iota(jnp.int32, (1, LANES), 1)
    lo = lane < HEAD_DIM
    extra = acc_ref.shape[0] - LANES
    ones_rows = jnp.where(lax.broadcasted_iota(jnp.int32, (extra, bk), 0) == 0, 1.0, 0.0).astype(BF16)

    def both_maps(qi):
        q = q_ref[0, pl.ds(pl.multiple_of(qi * bq, bq), bq), :]
        zero = jnp.zeros_like(q)
        return jnp.concatenate([jnp.where(lo, q, zero), jnp.where(lo, zero, q)], axis=0)

    def scores(s_ref, kb, q2):
        ks = pl.multiple_of(kb * bk, bk)
        s_ref[...] = _dot_nt(k_ref[0, pl.ds(ks, bk), :], q2)

    def q_block(qi, _):
        qs = pl.multiple_of(qi * bq, bq)
        q2 = both_maps(qi)

        def table(kb):
            if mode == "diff":
                return None
            return bias_ref[jnp.minimum(2 * qi - kb + 1, bias_ref.shape[0] - 1)]

        def update(s_ref, kb, stats, bias, first=0):
            vt1 = jnp.concatenate([vt_ref[0, kb], ones_rows], axis=0)
            out = []
            for half in range(2):
                cols = slice(half * bq + first, (half + 1) * bq)
                m = stats[half]
                s = s_ref[:, cols]
                if bias is not None:
                    s = s + bias[:, first:]
                mn = jnp.maximum(m[:, first:], jnp.max(s, axis=0, keepdims=True))
                p = jnp.exp2(s - mn)
                acc_ref[:, cols] = (jnp.exp2(m[:, first:] - mn) * acc_ref[:, cols]
                                    + _dot(vt1, p.astype(BF16)))
                out.append(jnp.concatenate([m[:, :first], mn], axis=1) if first else mn)
            return tuple(out)

        def pair(j, stats):
            scores(s_b, 2 * j + 1, q2)
            stats = update(s_a, 2 * j, stats, table(2 * j))
            scores(s_a, 2 * j + 2, q2)
            return update(s_b, 2 * j + 1, stats, table(2 * j + 1))

        acc_ref[...] = jnp.zeros_like(acc_ref)
        stats = lax.fori_loop(0, qi, pair, (jnp.full((1, bq), NEG, F32),) * 2)

        krow = lax.broadcasted_iota(jnp.int32, (bk, bq), 0)
        qcol = lax.broadcasted_iota(jnp.int32, (bk, bq), 1)
        scores(s_b, 2 * qi + 1, q2)
        for rel, s_ref in enumerate((s_a, s_b)):
            bias = table(2 * qi + rel)
            if mode == "diff":
                bias = jnp.where(krow + rel * bk <= qcol, 0.0, NEG).astype(F32)
            stats = update(s_ref, 2 * qi + rel, stats, bias, first=rel * bk)
            if rel == 0:
                scores(s_a, 0, both_maps(jnp.minimum(qi + 1, nq - 1)))
        a_lo, a_hi = acc_ref[0:LANES, :bq], acc_ref[0:LANES, bq:]
        l_lo, l_hi = acc_ref[LANES:LANES + 1, :bq], acc_ref[LANES:LANES + 1, bq:]
        if mode == "diff":
            lp = lam_ref[...]
            lam = (jnp.exp(jnp.sum(lp[0:1] * lp[1:2], axis=-1, keepdims=True))
                   - jnp.exp(jnp.sum(lp[2:3] * lp[3:4], axis=-1, keepdims=True)) + lam_init)
            y = a_lo * (1.0 / l_lo) - a_hi * (lam / l_hi)
            ms = jnp.mean(y * y, axis=0, keepdims=True)
            y = y * lax.rsqrt(ms + EPS) * (g_ref[...] * (1.0 - lam_init))
        else:
            feat = lax.broadcasted_iota(jnp.int32, (LANES, 1), 0)
            y = jnp.where(feat < HEAD_DIM, a_lo * (1.0 / l_lo), a_hi * (1.0 / l_hi))
        o_ref[0, pl.ds(qs, bq), :] = y.T.astype(o_ref.dtype)
        return 0

    scores(s_a, 0, both_maps(0))
    lax.fori_loop(0, nq, q_block, 0)


def _attention(mode, z, vt, q_col, k_col, v_row, n_blocks, extras, extra_specs, lam_init=0.0):
    B, S, _ = z.shape
    zspec = lambda col: pl.BlockSpec((1, S, LANES), lambda b, h: (b, 0, col + h))
    vspec = pl.BlockSpec((1, S // KV_BLOCK, LANES, KV_BLOCK), lambda b, h: (b, 0, v_row + h, 0))
    return pl.pallas_call(
        functools.partial(_attn_kernel, mode, lam_init),
        grid=(B, n_blocks),
        in_specs=[zspec(q_col), zspec(k_col), vspec] + extra_specs,
        out_specs=pl.BlockSpec((1, S, LANES), lambda b, h: (b, 0, h)),
        out_shape=jax.ShapeDtypeStruct((B, S, n_blocks * LANES), BF16),
        scratch_shapes=[pltpu.VMEM((KV_BLOCK, 2 * Q_BLOCK), F32), pltpu.VMEM((KV_BLOCK, 2 * Q_BLOCK), F32),
                        pltpu.VMEM((LANES + BF16_ROWS, 2 * Q_BLOCK), F32)],
        compiler_params=_cparams(("parallel", "parallel")),
        name=mode + "_attention",
    )(z, z, vt, *extras)


def _dilated_tables(S):
    bq, bk = Q_BLOCK, KV_BLOCK
    dds = np.arange(-1, S // bk)
    dist = (dds[:, None, None] * bk + np.arange(bq)[None, None, :] - np.arange(bk)[None, :, None])
    cnt = np.zeros(dist.shape, np.float32)
    for window, dil in DILATED_PAIRS:
        cnt += (dist >= 0) & (dist <= window) & (dist % dil == 0)
    n = len(dds)
    while n > 1 and np.array_equal(cnt[n - 1], cnt[n - 2]):
        n -= 1
    cnt = cnt[:n]
    return jnp.asarray(np.where(cnt > 0, np.log2(np.maximum(cnt, 1.0)), NEG).astype(np.float32))


def _pool_kernel(u_ref, w_ref, b_ref, sc_ref, o_ref):
    u = u_ref[0].astype(F32)
    S, C = u.shape
    row = lax.broadcasted_iota(jnp.int32, (S, C), 0)
    lane = lax.broadcasted_iota(jnp.int32, (S, C), 1)
    grp = lane // (C // len(POOL_WINDOWS))

    def shifted(a, k):
        return jnp.where(row >= k, pltpu.roll(a, k, 0), 0.0)

    acc = u
    win = jnp.zeros_like(u)
    width = 1
    for gi, w in enumerate(POOL_WINDOWS):
        while width < w:
            acc = acc + shifted(acc, width)
            width *= 2
        win = jnp.where(grp == gi, acc, win)
    wl = jnp.zeros_like(row)
    for gi, w in enumerate(POOL_WINDOWS):
        wl = jnp.where(grp == gi, w, wl)
    cnt = jnp.minimum(row + 1, wl).astype(F32)
    d = win / cnt - u
    y = _dot(d.astype(BF16), w_ref[...]) + b_ref[...]
    o_ref[0] = (y * sc_ref[...]).astype(o_ref.dtype)


def _pool_mixer(z, col_block, w_bd, b, scale):
    B, S, _ = z.shape
    C = w_bd.shape[0]
    return pl.pallas_call(
        _pool_kernel,
        grid=(B,),
        in_specs=[pl.BlockSpec((1, S, C), lambda b_: (b_, 0, col_block)),
                  pl.BlockSpec((C, C), lambda b_: (0, 0)),
                  pl.BlockSpec((1, C), lambda b_: (0, 0)),
                  pl.BlockSpec((1, C), lambda b_: (0, 0))],
        out_specs=pl.BlockSpec((1, S, C), lambda b_: (b_, 0, 0)),
        out_shape=jax.ShapeDtypeStruct((B, S, C), BF16),
        compiler_params=_cparams(("parallel",)),
        name="pool_mixer",
    )(z, w_bd, b, scale)


def _outproj_kernel(ya_ref, yb_ref, yc_ref, x_ref, wo_ref, ga_ref, sc_ref, sh_ref, g_ref, wr_ref,
                    br_ref, x1_ref, h_ref, rt_ref):
    na = ya_ref.shape[2]
    nb = yb_ref.shape[2]
    halves = [slice(r * (x_ref.shape[1] // 2), (r + 1) * (x_ref.shape[1] // 2)) for r in range(2)]

    def mix(rows):
        return (_dot(ya_ref[0, rows, :], wo_ref[0:na, :]) + _dot(yb_ref[0, rows, :], wo_ref[na:na + nb, :])
                + _dot(yc_ref[0, rows, :], wo_ref[na + nb:, :]))

    def residual_norm(rows, y):
        x1 = x_ref[0, rows, :] + ga_ref[0] * y
        x1_ref[0, rows, :] = x1
        ms = jnp.mean(x1 * x1, axis=-1, keepdims=True)
        h = x1 * lax.rsqrt(ms + EPS) * g_ref[...]
        h = h * (1.0 + sc_ref[0]) + sh_ref[0]
        hb = h.astype(BF16)
        h_ref[0, rows, :] = hb
        return hb

    ys = [mix(rows) for rows in halves]
    hb = jnp.concatenate([residual_norm(rows, y) for rows, y in zip(halves, ys)], axis=0)

    logits = _dot_nt(wr_ref[...], hb) + br_ref[...]
    tm = logits.shape[1]
    G, E = N_GROUPS, EXPERTS_PER_GROUP
    big = float(LANES)
    grow = lax.broadcasted_iota(jnp.int32, (8, tm), 0)
    grow_f = grow.astype(F32)
    gl = jnp.where(grow < G, logits[G * E:G * E + 8, :], -jnp.inf)
    gmax = jnp.max(gl, axis=0, keepdims=True)
    gidx = jnp.min(jnp.where(gl == gmax, grow_f, big), axis=0, keepdims=True)
    gsum = jnp.sum(jnp.where(grow < G, jnp.exp(gl - gmax), 0.0), axis=0, keepdims=True)
    g_w = 1.0 / gsum
    erow = lax.broadcasted_iota(jnp.int32, (G * E, tm), 0)
    erow_f = erow.astype(F32)
    el = jnp.where((erow // E).astype(F32) == gidx, logits[0:G * E, :], -jnp.inf)
    v1 = jnp.max(el, axis=0, keepdims=True)
    i1 = jnp.min(jnp.where(el == v1, erow_f, big), axis=0, keepdims=True)
    el2 = jnp.where(erow_f == i1, -jnp.inf, el)
    v2 = jnp.max(el2, axis=0, keepdims=True)
    i2 = jnp.min(jnp.where(el2 == v2, erow_f, big), axis=0, keepdims=True)
    t = jnp.exp(v2 - v1)
    w1 = g_w / (1.0 + t)
    w2 = w1 * t
    rt_ref[0] = jnp.concatenate([i1, i2, w1, w2, jnp.zeros((rt_ref.shape[1] - 4, tm), F32)], axis=0)


def _out_projection(ya, yb, yc, x, wo_bf, ga, sc, sh, g, wr, br, tm=DISPATCH_TILE):
    B, S, D = x.shape
    nt = S // tm
    tok = lambda n: pl.BlockSpec((1, tm, n), lambda b, i: (b, i, 0))
    per_b = pl.BlockSpec((1, 1, D), lambda b, i: (b, 0, 0))
    const = lambda shape: pl.BlockSpec(shape, lambda b, i: (0, 0))
    return pl.pallas_call(
        _outproj_kernel,
        grid=(B, nt),
        in_specs=[tok(ya.shape[2]), tok(yb.shape[2]), tok(yc.shape[2]), tok(D), const((D, D)),
                  per_b, per_b, per_b, const((1, D)), const((LANES, D)), const((LANES, 1))],
        out_specs=[tok(D), tok(D),
                   pl.BlockSpec((1, 8, tm), lambda b, i: (b * nt + i, 0, 0))],
        out_shape=[jax.ShapeDtypeStruct((B, S, D), F32),
                   jax.ShapeDtypeStruct((B, S, D), BF16),
                   jax.ShapeDtypeStruct((B * nt, 8, tm), F32)],
        compiler_params=_cparams(("parallel", "parallel")),
        name="out_projection",
    )(ya, yb, yc, x, wo_bf, ga, sc, sh, g, wr, br)


def _rank_kernel(rt_ref, before_ref, ltri_ref, pos_ref, info_ref, tinfo_ref, cnt_ref, off_ref, carry_ref):
    ph, i = pl.program_id(0), pl.program_id(1)
    ne = N_EXPERTS
    rt_t = rt_ref[0]
    tm = rt_t.shape[1]
    expert = lax.broadcasted_iota(jnp.int32, (ne, tm), 0).astype(F32)
    hit1 = expert == rt_t[RT_E:RT_E + 1, :]
    hit2 = expert == rt_t[RT_E + 1:RT_E + 2, :]
    onehot = jnp.where(hit1, 1.0, jnp.where(hit2, 1.0, 0.0))
    n_blk = jnp.ceil(jnp.sum(onehot, axis=1, keepdims=True) * (1.0 / ROW_ALIGN)) * ROW_ALIGN
    n_blk = jnp.broadcast_to(n_blk, (ne, LANES))

    @pl.when((ph == 0) & (i == 0))
    def _():
        cnt_ref[...] = jnp.zeros_like(cnt_ref)

    @pl.when(ph == 0)
    def _():
        cnt_ref[...] += n_blk

    @pl.when((ph == 1) & (i == 0))
    def _():
        ntile = jnp.ceil(cnt_ref[...] * (1.0 / MOE_TILE))
        off_tiles = _dot(ltri_ref[...], ntile.astype(BF16))
        off_ref[...] = off_tiles * MOE_TILE
        info_ref[0] = cnt_ref[...]
        info_ref[1] = off_tiles
        carry_ref[...] = jnp.zeros_like(carry_ref)

    @pl.when(ph == 1)
    def _():
        dst0 = carry_ref[...] + off_ref[...]
        loc0 = _dot(ltri_ref[...], (n_blk * (1.0 / ROW_ALIGN)).astype(BF16)) * ROW_ALIGN
        within = _dot(onehot.astype(BF16), before_ref[...])
        in_sorted = within + dst0[:, 0:1]
        in_buffer = within + loc0[:, 0:1]
        rows = [jnp.sum(jnp.where(hit, v, 0.0), axis=0, keepdims=True)
                for v in (in_sorted, in_buffer) for hit in (hit1, hit2)]
        rows.append(jnp.zeros((pos_ref.shape[1] - len(rows), tm), F32))
        pos_ref[0] = jnp.concatenate(rows, axis=0).astype(jnp.int32)
        tinfo_ref[0, 0] = dst0
        tinfo_ref[0, 1] = n_blk
        tinfo_ref[0, 2] = loc0
        carry_ref[...] += n_blk


def _rank(rt):
    n_tok_tiles, _, tm = rt.shape
    T = n_tok_tiles * tm
    idx = np.arange(tm)
    before = jnp.asarray(idx[:, None] < idx[None, :], BF16)
    ex = np.arange(N_EXPERTS)
    ltri = jnp.asarray(ex[None, :] < ex[:, None], BF16)
    stat = pltpu.VMEM((N_EXPERTS, LANES), F32)
    return pl.pallas_call(
        _rank_kernel,
        grid=(2, T // tm),
        in_specs=[pl.BlockSpec((1, 8, tm), lambda ph, i: (i, 0, 0)),
                  pl.BlockSpec((tm, tm), lambda ph, i: (0, 0)),
                  pl.BlockSpec((N_EXPERTS, N_EXPERTS), lambda ph, i: (0, 0))],
        out_specs=[pl.BlockSpec((1, 8, tm), lambda ph, i: (i * ph, 0, 0)),
                   pl.BlockSpec((2, N_EXPERTS, LANES), lambda ph, i: (0, 0, 0)),
                   pl.BlockSpec((1, 3, N_EXPERTS, LANES), lambda ph, i: (i * ph, 0, 0, 0))],
        out_shape=[jax.ShapeDtypeStruct((T // tm, 8, tm), jnp.int32),
                   jax.ShapeDtypeStruct((2, N_EXPERTS, LANES), F32),
                   jax.ShapeDtypeStruct((T // tm, 3, N_EXPERTS, LANES), F32)],
        scratch_shapes=[stat, stat, stat],
        compiler_params=_cparams(("arbitrary", "arbitrary")),
        name="moe_rank",
    )(rt, before, ltri)


def _block_copies(n, src_of, dst_of, sem, act):
    def whole(j, _):
        off = pl.multiple_of(j * COPY_CHUNK, COPY_CHUNK)
        act(pltpu.make_async_copy(src_of(off, COPY_CHUNK), dst_of(off, COPY_CHUNK), sem))
        return 0
    lax.fori_loop(0, n // COPY_CHUNK, whole, 0)
    for b in range(COPY_CHUNK.bit_length() - 2, ROW_ALIGN.bit_length() - 2, -1):
        size = 1 << b

        @pl.when((n >> b) & 1 == 1)
        def _():
            off = pl.multiple_of((n >> (b + 1)) << (b + 1), ROW_ALIGN)
            act(pltpu.make_async_copy(src_of(off, size), dst_of(off, size), sem))


def _dispatch_kernel(dst_ref, nblk_ref, loc_ref, tail0_ref, tailn_ref, nv_ref, h_ref, lpos_ref, xs_ref,
                     obuf, zbuf, sems, zsem):
    i = pl.program_id(0)
    n = pl.num_programs(0)
    nc, tm = obuf.shape[1], h_ref.shape[0]
    R = obuf.shape[2]
    slot = i % 2

    def tile_copies(tile, sl, act):
        def body(e, _):
            c = tile * N_EXPERTS + e
            loc = pl.multiple_of(loc_ref[c], ROW_ALIGN)
            dst = pl.multiple_of(dst_ref[c], ROW_ALIGN)
            _block_copies(nblk_ref[c],
                          lambda off, size: obuf.at[sl, :, pl.ds(loc + off, size)],
                          lambda off, size: xs_ref.at[:, pl.ds(dst + off, size)],
                          sems.at[sl], act)
            return 0
        lax.fori_loop(0, N_EXPERTS, body, 0)

    def zero_fill(act):
        def body(e, _):
            t0 = pl.multiple_of(tail0_ref[e], ROW_ALIGN)
            _block_copies(tailn_ref[e], lambda off, size: zbuf.at[:, pl.ds(0, size)],
                          lambda off, size: xs_ref.at[:, pl.ds(t0 + off, size)], zsem, act)
            return 0
        lax.fori_loop(0, N_EXPERTS, body, 0)

        def unused(j, _):
            j0 = pl.multiple_of(j * MOE_TILE, MOE_TILE)
            act(pltpu.make_async_copy(zbuf, xs_ref.at[:, pl.ds(j0, MOE_TILE)], zsem))
            return 0
        lax.fori_loop(nv_ref[0], xs_ref.shape[1] // MOE_TILE, unused, 0)

    @pl.when(i == 0)
    def _():
        zbuf[...] = jnp.zeros_like(zbuf)
        zero_fill(lambda cp: cp.start())

    @pl.when(i >= 2)
    def _():
        tile_copies(i - 2, slot, lambda cp: cp.wait())

    hb = h_ref[...]
    lp = lpos_ref[0].astype(F32)
    l1, l2 = lp[2:3, :], lp[3:4, :]
    blk = MXU_DIM
    for jb in range(R // blk):
        r = (lax.broadcasted_iota(jnp.int32, (blk, tm), 0) + jb * blk).astype(F32)
        sel = jnp.where(r == l1, 1.0, jnp.where(r == l2, 1.0, 0.0)).astype(BF16)
        rows = _dot(sel, hb)
        for s in range(nc):
            obuf[slot, s, jb * blk:(jb + 1) * blk, :] = rows[:, s * LANES:(s + 1) * LANES]
    tile_copies(i, slot, lambda cp: cp.start())

    @pl.when(i == n - 1)
    def _():
        @pl.when(i >= 1)
        def _():
            tile_copies(i - 1, 1 - slot, lambda cp: cp.wait())
        tile_copies(i, slot, lambda cp: cp.wait())
        zero_fill(lambda cp: cp.wait())


def _dispatch(dst0, nblk, loc0, tail0, tailn, n_valid, h, pos, n_rows):
    T, D = h.shape
    NC = D // LANES
    tm = pos.shape[2]
    R = 2 * tm + N_EXPERTS * ROW_ALIGN
    R = -(-R // MXU_DIM) * MXU_DIM
    return pl.pallas_call(
        _dispatch_kernel,
        grid_spec=pltpu.PrefetchScalarGridSpec(
            num_scalar_prefetch=6,
            grid=(T // tm,),
            in_specs=[pl.BlockSpec((tm, D), lambda i, *_: (i, 0)),
                      pl.BlockSpec((1, pos.shape[1], tm), lambda i, *_: (i, 0, 0))],
            out_specs=pl.BlockSpec(memory_space=pl.ANY),
            scratch_shapes=[pltpu.VMEM((2, NC, R, LANES), F32), pltpu.VMEM((NC, MOE_TILE, LANES), F32),
                            pltpu.SemaphoreType.DMA((2,)), pltpu.SemaphoreType.DMA]),
        out_shape=jax.ShapeDtypeStruct((NC, n_rows, LANES), F32),
        compiler_params=_cparams(("arbitrary",)),
        name="moe_dispatch",
    )(dst0, nblk, loc0, tail0, tailn, n_valid, h, pos)


def _experts_kernel(te_ref, nv_ref, xs_ref, w1_ref, w3_ref, w2_ref, ys_ref, w1b, w3b, w2b):
    j = pl.program_id(0)
    e = te_ref[j]
    e_prev = te_ref[jnp.maximum(j - 1, 0)]

    @pl.when((j == 0) | (e != e_prev))
    def _():
        w1b[...] = w1_ref[0].astype(BF16)
        w3b[...] = w3_ref[0].astype(BF16)
        w2b[...] = w2_ref[0].astype(BF16)

    @pl.when(j < nv_ref[0])
    def _():
        nc = xs_ref.shape[0]
        half = xs_ref.shape[1] // 2

        def up(rows):
            h = jnp.concatenate([xs_ref[s, rows, :] for s in range(nc)], axis=1).astype(BF16)
            return _dot(h, w1b[...]), _dot(h, w3b[...])

        def down(rows, ab):
            a, b = ab
            hid = (a * (1.0 / (1.0 + jnp.exp(-a))) * b).astype(BF16)
            y = _dot(hid, w2b[...])
            for s in range(nc):
                ys_ref[s, rows, :] = y[:, s * LANES:(s + 1) * LANES]

        top, bottom = slice(0, half), slice(half, 2 * half)
        ab_top = up(top)
        ab_bottom = up(bottom)
        down(top, ab_top)
        down(bottom, ab_bottom)


def _experts(tile_expert, n_valid, xs, w1, w3, w2):
    NC, P, _ = xs.shape
    NE, D, F = w1.shape
    nt = P // MOE_TILE
    tile = lambda j, te, nv: (0, jnp.minimum(j, nv[0] - 1), 0)
    wspec = lambda shape: pl.BlockSpec(shape, lambda j, te, nv: (te[j], 0, 0))
    return pl.pallas_call(
        _experts_kernel,
        grid_spec=pltpu.PrefetchScalarGridSpec(
            num_scalar_prefetch=2,
            grid=(nt,),
            in_specs=[pl.BlockSpec((NC, MOE_TILE, LANES), tile),
                      wspec((1, D, F)), wspec((1, D, F)), wspec((1, F, D))],
            out_specs=pl.BlockSpec((NC, MOE_TILE, LANES), tile),
            scratch_shapes=[pltpu.VMEM((D, F), BF16), pltpu.VMEM((D, F), BF16),
                            pltpu.VMEM((F, D), BF16)]),
        out_shape=jax.ShapeDtypeStruct((NC, P, LANES), F32),
        input_output_aliases={2: 0},
        compiler_params=_cparams(("arbitrary",)),
        name="moe_experts",
    )(tile_expert, n_valid, xs, w1, w3, w2)


def _combine_kernel(pos_ref, x_ref, rt_ref, ga_ref, ys_ref, o_ref, buf, sems):
    i = pl.program_id(0)
    n = pl.num_programs(0)
    tm = x_ref.shape[0]
    nc = buf.shape[1]

    def row_copy(tile, slot, t, k):
        src = ys_ref.at[:, pos_ref[tile * (2 * tm) + 2 * t + k]]
        return pltpu.make_async_copy(src, buf.at[slot, :, k * tm + t], sems.at[slot])

    def issue(tile, slot):
        def body(t, _):
            row_copy(tile, slot, t, 0).start(priority=0)
            row_copy(tile, slot, t, 1).start(priority=1)
            return 0
        lax.fori_loop(0, tm, body, 0, unroll=8)

    @pl.when(i == 0)
    def _():
        issue(0, 0)

    @pl.when(i + 1 < n)
    def _():
        issue(i + 1, (i + 1) % 2)

    slot = i % 2

    def drain(t, _):
        row_copy(i, slot, t, 0).wait()
        row_copy(i, slot, t, 1).wait()
        return 0

    lax.fori_loop(0, tm, drain, 0, unroll=8)
    rt_t = rt_ref[0]
    rt = jnp.concatenate([rt_t, jnp.zeros((LANES - rt_t.shape[0], tm), F32)], axis=0).T
    w1 = rt[:, RT_W:RT_W + 1]
    w2 = rt[:, RT_W + 1:RT_W + 2]
    for s in range(nc):
        cols = slice(s * LANES, (s + 1) * LANES)
        y = w1 * buf[slot, s, 0:tm, :] + w2 * buf[slot, s, tm:2 * tm, :]
        o_ref[:, cols] = x_ref[:, cols] + ga_ref[0][:, cols] * y


def _combine(pos_flat, x1, rt, ga, ys, tm=512):
    B, S, D = x1.shape
    NC = ys.shape[0]
    tm = min(tm, S)
    nt = S // tm
    per_rt = rt.shape[2] // tm
    return pl.pallas_call(
        _combine_kernel,
        grid_spec=pltpu.PrefetchScalarGridSpec(
            num_scalar_prefetch=1,
            grid=(B * nt,),
            in_specs=[pl.BlockSpec((tm, D), lambda i, pos: (i, 0)),
                      pl.BlockSpec((1, rt.shape[1], tm), lambda i, pos: (i // per_rt, 0, i % per_rt)),
                      pl.BlockSpec((1, 1, D), lambda i, pos: (i // nt, 0, 0)),
                      pl.BlockSpec(memory_space=pl.ANY)],
            out_specs=pl.BlockSpec((tm, D), lambda i, pos: (i, 0)),
            scratch_shapes=[pltpu.VMEM((2, NC, 2 * tm, LANES), F32),
                            pltpu.SemaphoreType.DMA((2,))]),
        out_shape=jax.ShapeDtypeStruct((B * S, D), F32),
        compiler_params=_cparams(("arbitrary",)),
        name="moe_combine",
    )(pos_flat, x1.reshape(B * S, D), rt, ga, ys).reshape(B, S, D)


def _moe(h3, rt, x1, ga, w1, w3, w2, first_expert):
    n_tok_tiles = rt.shape[0]
    T = n_tok_tiles * rt.shape[2]
    n_tiles = -(-(2 * T + n_tok_tiles * N_EXPERTS * ROW_ALIGN) // MOE_TILE) + N_EXPERTS
    pos, info, tinfo = _rank(rt)
    pos_flat = pos[:, :2, :].transpose(0, 2, 1).reshape(-1)
    counts = info[0, :, 0]
    starts = info[1, :, 0]
    ends = starts + jnp.ceil(counts * (1.0 / MOE_TILE))
    tile_ids = jnp.arange(n_tiles, dtype=F32)
    tile_expert = jnp.minimum(jnp.sum(tile_ids[:, None] >= ends[None, :], axis=1), N_EXPERTS - 1)
    n_valid = ends[N_EXPERTS - 1:].astype(jnp.int32)
    as_ints = lambda a: a.astype(jnp.int32).reshape(-1)
    xs = _dispatch(as_ints(tinfo[:, 0, :, 0]), as_ints(tinfo[:, 1, :, 0]),
                   as_ints(tinfo[:, 2, :, 0]), as_ints(starts * MOE_TILE + counts),
                   as_ints(ends * MOE_TILE - starts * MOE_TILE - counts), n_valid,
                   h3.reshape(T, h3.shape[-1]), pos, n_tiles * MOE_TILE)
    ys = _experts(tile_expert.astype(jnp.int32) + first_expert, n_valid, xs, w1, w3, w2)
    return _combine(pos_flat, x1, rt, ga, ys)


def kernel(x, c, w_mod, b_mod, g_norm1, w_in, gq_a, gk_a, lam_a, g_sub_a, w_pool, b_pool, pool_scale, gq_c, gk_c, w_out, g_norm2, w_rg, b_rg, w_re, b_re, w1, w3, w2):
    B, S, D = x.shape
    L = w_mod.shape[0]
    a_width = D // 2
    pool_width = D // 4
    c_width = D // 4
    W = MXU_DIM
    reps = W // HEAD_DIM

    inv = 1.0 / (ROPE_THETA ** (jnp.arange(0, HEAD_DIM, 2, dtype=F32) / HEAD_DIM))
    ang = jnp.arange(S, dtype=F32)[:, None] * inv[None, :]
    ang = jnp.concatenate([ang, ang], axis=-1)
    cos_h, sin_h = jnp.cos(ang), jnp.sin(ang)
    first = jnp.arange(HEAD_DIM) < HEAD_DIM // 2
    cos_t = jnp.tile(cos_h, (1, reps))
    sa_t = jnp.tile(jnp.where(first[None, :], 0.0, sin_h), (1, reps))
    sb_t = jnp.tile(jnp.where(first[None, :], -sin_h, 0.0), (1, reps))
    head_of = np.arange(W) // HEAD_DIM
    bd = jnp.asarray((head_of[:, None] == head_of[None, :]) / HEAD_DIM, BF16)

    qa0, ka0, va0 = 0, a_width, 2 * a_width
    ub0 = 3 * a_width
    qc0, kc0, vc0 = ub0 + pool_width, ub0 + pool_width + c_width, ub0 + pool_width + 2 * c_width
    chunks = lambda lo_, hi_: list(range(lo_ // W, hi_ // W))
    z_chunks = tuple([(j, True) for j in chunks(qa0, va0)] + [(j, False) for j in chunks(ub0, qc0)]
                     + [(j, True) for j in chunks(qc0, vc0)])
    v_chunks = tuple(chunks(va0, ub0) + chunks(vc0, w_in.shape[2]))
    zq_a, zk_a = 0, a_width // LANES
    z_ub = 2 * a_width // pool_width
    zq_c = (2 * a_width + pool_width) // LANES
    zk_c = zq_c + c_width // LANES
    scale = HEAD_DIM ** -0.5 * math.log2(math.e)
    bias_t = _dilated_tables(S)

    mod = _modulation(c, w_mod, b_mod)
    for l in range(L):
        sh1, sc1, ga1, sh2, sc2, ga2 = [m[:, None, :] for m in jnp.split(mod[l], N_MOD, axis=-1)]
        gain = jnp.ones((w_in.shape[2],), F32)
        gain = gain.at[qa0:ka0].set(jnp.tile(gq_a[l], a_width // HEAD_DIM) * scale)
        gain = gain.at[ka0:va0].set(jnp.tile(gk_a[l], a_width // HEAD_DIM))
        gain = gain.at[qc0:kc0].set(jnp.tile(gq_c[l], c_width // HEAD_DIM) * scale)
        gain = gain.at[kc0:vc0].set(jnp.tile(gk_c[l], c_width // HEAD_DIM))
        z, vt = _in_projection(x, sc1, sh1, g_norm1[l][None, :], w_in[l].astype(BF16),
                               gain[None, :], cos_t, sa_t, sb_t, bd, z_chunks, v_chunks)

        lam_init = 0.8 - 0.6 * math.exp(-0.3 * l)
        n_a = a_width // LANES
        ya = _attention("diff", z, vt, zq_a, zk_a, 0, n_a,
                        [lam_a[l], g_sub_a[l][:, None]],
                        [pl.BlockSpec(lam_a[l].shape, lambda b, h: (0, 0)),
                         pl.BlockSpec((LANES, 1), lambda b, h: (0, 0))], lam_init=lam_init)
        n_c = c_width // LANES
        yc = _attention("dil", z, vt, zq_c, zk_c, n_a, n_c,
                        [bias_t], [pl.BlockSpec(bias_t.shape, lambda b, h: (0, 0, 0))])
        w_bd = jax.scipy.linalg.block_diag(*[w_pool[l, g] for g in range(w_pool.shape[1])])
        yb = _pool_mixer(z, z_ub, w_bd.astype(BF16), b_pool[l].reshape(1, -1),
                         pool_scale[l][None, :])

        wr = jnp.zeros((LANES, D), F32)
        wr = wr.at[:N_EXPERTS].set(w_re[l].transpose(0, 2, 1).reshape(N_EXPERTS, D))
        wr = wr.at[N_EXPERTS:N_EXPERTS + N_GROUPS].set(w_rg[l].T)
        br = jnp.zeros((LANES, 1), F32)
        br = br.at[:N_EXPERTS, 0].set(b_re[l].reshape(-1))
        br = br.at[N_EXPERTS:N_EXPERTS + N_GROUPS, 0].set(b_rg[l])
        x1, h3, rt = _out_projection(ya, yb, yc, x, w_out[l].astype(BF16), ga1, sc2, sh2,
                                     g_norm2[l][None, :], wr.astype(BF16), br)
        F = w1.shape[-1]
        x = _moe(h3, rt, x1, ga2, w1.reshape(L * N_EXPERTS, D, F), w3.reshape(L * N_EXPERTS, D, F),
                 w2.reshape(L * N_EXPERTS, F, D), l * N_EXPERTS)
    return x
```

```python
import functools
import math

import jax
import jax.numpy as jnp
import numpy as np
from jax import lax
from jax.experimental import pallas as pl
from jax.experimental.pallas import tpu as pltpu

HEAD_DIM = 64
POOL_WINDOWS = (2, 4, 8, 16)
DILATED_PAIRS = ((128, 1), (512, 4), (2048, 16))
ROPE_THETA = 10000.0
N_GROUPS = 4
EXPERTS_PER_GROUP = 8
N_EXPERTS = N_GROUPS * EXPERTS_PER_GROUP
N_MOD = 6
EPS = 1e-6

LANES = 128
BF16_ROWS = 16
MXU_DIM = 256
KV_BLOCK = 256
Q_BLOCK = 2 * KV_BLOCK
MOE_TILE = 512
DISPATCH_TILE = 512
ROW_ALIGN = 8
COPY_CHUNK = 64
RT_E, RT_W = 0, 2
VMEM_LIMIT = 48 * 1024 * 1024
NEG = -1e30
F32 = jnp.float32
BF16 = jnp.bfloat16


def _cparams(sem):
    return pltpu.CompilerParams(dimension_semantics=sem, vmem_limit_bytes=VMEM_LIMIT)


def _dot(a, b):
    return jnp.dot(a, b, preferred_element_type=F32)


def _dot_nt(a, b):
    return lax.dot_general(a, b, (((1,), (1,)), ((), ())), preferred_element_type=F32)


def _mod_kernel(c_ref, w_ref, b_ref, o_ref):
    c = c_ref[...]
    cond = c * (1.0 / (1.0 + jnp.exp(-c)))
    o_ref[0] = _dot(cond.astype(BF16), w_ref[0].astype(BF16)) + b_ref[0]


def _modulation(c, w_mod, b_mod):
    L, D, N = w_mod.shape
    B = c.shape[0]
    tn = 1024
    return pl.pallas_call(
        _mod_kernel,
        grid=(L, N // tn),
        in_specs=[pl.BlockSpec((B, D), lambda l, j: (0, 0)),
                  pl.BlockSpec((1, D, tn), lambda l, j: (l, 0, j)),
                  pl.BlockSpec((1, 1, tn), lambda l, j: (l, 0, j))],
        out_specs=pl.BlockSpec((1, B, tn), lambda l, j: (l, 0, j)),
        out_shape=jax.ShapeDtypeStruct((L, B, N), F32),
        compiler_params=_cparams(("parallel", "parallel")),
        name="modulation",
    )(c, w_mod, b_mod.reshape(L, 1, N))


def _inproj_kernel(z_chunks, v_chunks, x_ref, sc_ref, sh_ref, g_ref, w_ref, gain_ref, cos_ref,
                   sa_ref, sb_ref, bd_ref, z_ref, vt_ref):
    x = x_ref[0]
    ms = jnp.mean(x * x, axis=-1, keepdims=True)
    h = x * lax.rsqrt(ms + EPS) * g_ref[...]
    h = h * (1.0 + sc_ref[0]) + sh_ref[0]
    hb = h.astype(BF16)
    W = MXU_DIM

    def project(src):
        return _dot(hb, w_ref[0, :, src * W:(src + 1) * W])

    def finish_z(dst, src, normed, zc):
        if normed:
            msq = _dot((zc * zc).astype(BF16), bd_ref[...])
            y = zc * lax.rsqrt(msq + EPS) * gain_ref[:, src * W:(src + 1) * W]
            r_up = pltpu.roll(y, HEAD_DIM // 2, 1)
            r_dn = pltpu.roll(y, W - HEAD_DIM // 2, 1)
            zc = y * cos_ref[...] + r_up * sa_ref[...] + r_dn * sb_ref[...]
        z_ref[0, :, dst * W:(dst + 1) * W] = zc.astype(BF16)

    def finish_v(dst, zc):
        zt = zc.T.astype(BF16)
        for cb in range(vt_ref.shape[1]):
            vt_ref[0, cb, dst * W:(dst + 1) * W, :] = zt[:, cb * KV_BLOCK:(cb + 1) * KV_BLOCK]

    work = ([(src, functools.partial(finish_z, dst, src, normed)) for dst, (src, normed) in enumerate(z_chunks)]
            + [(src, functools.partial(finish_v, dst)) for dst, src in enumerate(v_chunks)])
    raw = project(work[0][0])
    for idx, (_, finish) in enumerate(work):
        nxt = project(work[idx + 1][0]) if idx + 1 < len(work) else None
        finish(raw)
        raw = nxt


def _in_projection(x, sc, sh, g, w_bf, layer, gain, cos_t, sa_t, sb_t, bd, z_chunks, v_chunks, tm=1024):
    B, S, D = x.shape
    tm = min(tm, S)
    N = w_bf.shape[2]
    W = MXU_DIM
    nz, nv = len(z_chunks) * W, len(v_chunks) * W
    return pl.pallas_call(
        functools.partial(_inproj_kernel, z_chunks, v_chunks),
        grid=(B, S // tm),
        in_specs=[pl.BlockSpec((1, tm, D), lambda b, i: (b, i, 0)),
                  pl.BlockSpec((1, 1, D), lambda b, i: (b, 0, 0)),
                  pl.BlockSpec((1, 1, D), lambda b, i: (b, 0, 0)),
                  pl.BlockSpec((1, D), lambda b, i: (0, 0)),
                  pl.BlockSpec((1, D, N), lambda b, i: (layer, 0, 0)),
                  pl.BlockSpec((1, N), lambda b, i: (0, 0)),
                  pl.BlockSpec((tm, W), lambda b, i: (i, 0)),
                  pl.BlockSpec((tm, W), lambda b, i: (i, 0)),
                  pl.BlockSpec((tm, W), lambda b, i: (i, 0)),
                  pl.BlockSpec((W, W), lambda b, i: (0, 0))],
        out_specs=[pl.BlockSpec((1, tm, nz), lambda b, i: (b, i, 0)),
                   pl.BlockSpec((1, tm // KV_BLOCK, nv, KV_BLOCK), lambda b, i: (b, i, 0, 0))],
        out_shape=[jax.ShapeDtypeStruct((B, S, nz), BF16),
                   jax.ShapeDtypeStruct((B, S // KV_BLOCK, nv, KV_BLOCK), BF16)],
        compiler_params=_cparams(("parallel", "parallel")),
        name="in_projection",
    )(x, sc, sh, g, w_bf, gain, cos_t, sa_t, sb_t, bd)


def _attn_kernel(mode, lam_init, *refs):
    if mode == "diff":
        q_ref, k_ref, vt_ref, lam_ref, g_ref, o_ref, s_a, s_b, acc_ref = refs
    else:
        q_ref, k_ref, vt_ref, bias_ref, o_ref, s_a, s_b, acc_ref = refs
    bq, bk = Q_BLOCK, KV_BLOCK
    S = q_ref.shape[1]
    nq = S // bq
    lane = lax.broadcasted_iota(jnp.int32, (1, LANES), 1)
    lo = lane < HEAD_DIM
    extra = acc_ref.shape[0] - LANES
    ones_rows = jnp.where(lax.broadcasted_iota(jnp.int32, (extra, bk), 0) == 0, 1.0, 0.0).astype(BF16)

    def both_maps(qi):
        q = q_ref[0, pl.ds(pl.multiple_of(qi * bq, bq), bq), :]
        zero = jnp.zeros_like(q)
        return jnp.concatenate([jnp.where(lo, q, zero), jnp.where(lo, zero, q)], axis=0)

    def scores(s_ref, kb, q2):
        ks = pl.multiple_of(kb * bk, bk)
        s_ref[...] = _dot_nt(k_ref[0, pl.ds(ks, bk), :], q2)

    def q_block(qi, _):
        qs = pl.multiple_of(qi * bq, bq)
        q2 = both_maps(qi)

        def table(kb):
            if mode == "diff":
                return None
            return bias_ref[jnp.minimum(2 * qi - kb + 1, bias_ref.shape[0] - 1)]

        def update(s_ref, kb, stats, bias, first=0):
            vt1 = jnp.concatenate([vt_ref[0, kb], ones_rows], axis=0)
            out = []
            for half in range(2):
                cols = slice(half * bq + first, (half + 1) * bq)
                m = stats[half]
                s = s_ref[:, cols]
                if bias is not None:
                    s = s + bias[:, first:]
                mn = jnp.maximum(m[:, first:], jnp.max(s, axis=0, keepdims=True))
                p = jnp.exp2(s - mn)
                acc_ref[:, cols] = (jnp.exp2(m[:, first:] - mn) * acc_ref[:, cols]
                                    + _dot(vt1, p.astype(BF16)))
                out.append(jnp.concatenate([m[:, :first], mn], axis=1) if first else mn)
            return tuple(out)

        def pair(j, stats):
            scores(s_b, 2 * j + 1, q2)
            stats = update(s_a, 2 * j, stats, table(2 * j))
            scores(s_a, 2 * j + 2, q2)
            return update(s_b, 2 * j + 1, stats, table(2 * j + 1))

        acc_ref[...] = jnp.zeros_like(acc_ref)
        stats = lax.fori_loop(0, qi, pair, (jnp.full((1, bq), NEG, F32),) * 2)

        krow = lax.broadcasted_iota(jnp.int32, (bk, bq), 0)
        qcol = lax.broadcasted_iota(jnp.int32, (bk, bq), 1)
        scores(s_b, 2 * qi + 1, q2)
        for rel, s_ref in enumerate((s_a, s_b)):
            bias = table(2 * qi + rel)
            if mode == "diff":
                bias = jnp.where(krow + rel * bk <= qcol, 0.0, NEG).astype(F32)
            stats = update(s_ref, 2 * qi + rel, stats, bias, first=rel * bk)
            if rel == 0:
                scores(s_a, 0, both_maps(jnp.minimum(qi + 1, nq - 1)))
        a_lo, a_hi = acc_ref[0:LANES, :bq], acc_ref[0:LANES, bq:]
        l_lo, l_hi = acc_ref[LANES:LANES + 1, :bq], acc_ref[LANES:LANES + 1, bq:]
        if mode == "diff":
            lp = lam_ref[...]
            lam = (jnp.exp(jnp.sum(lp[0:1] * lp[1:2], axis=-1, keepdims=True))
                   - jnp.exp(jnp.sum(lp[2:3] * lp[3:4], axis=-1, keepdims=True)) + lam_init)
            y = a_lo * (1.0 / l_lo) - a_hi * (lam / l_hi)
            ms = jnp.mean(y * y, axis=0, keepdims=True)
            y = y * lax.rsqrt(ms + EPS) * (g_ref[...] * (1.0 - lam_init))
        else:
            feat = lax.broadcasted_iota(jnp.int32, (LANES, 1), 0)
            y = jnp.where(feat < HEAD_DIM, a_lo * (1.0 / l_lo), a_hi * (1.0 / l_hi))
        o_ref[0, pl.ds(qs, bq), :] = y.T.astype(o_ref.dtype)
        return 0

    scores(s_a, 0, both_maps(0))
    lax.fori_loop(0, nq, q_block, 0)


def _attention(mode, z, vt, q_col, k_col, v_row, n_blocks, extras, extra_specs, lam_init=0.0):
    B, S, _ = z.shape
    zspec = lambda col: pl.BlockSpec((1, S, LANES), lambda b, h: (b, 0, col + h))
    vspec = pl.BlockSpec((1, S // KV_BLOCK, LANES, KV_BLOCK), lambda b, h: (b, 0, v_row + h, 0))
    return pl.pallas_call(
        functools.partial(_attn_kernel, mode, lam_init),
        grid=(B, n_blocks),
        in_specs=[zspec(q_col), zspec(k_col), vspec] + extra_specs,
        out_specs=pl.BlockSpec((1, S, LANES), lambda b, h: (b, 0, h)),
        out_shape=jax.ShapeDtypeStruct((B, S, n_blocks * LANES), BF16),
        scratch_shapes=[pltpu.VMEM((KV_BLOCK, 2 * Q_BLOCK), F32), pltpu.VMEM((KV_BLOCK, 2 * Q_BLOCK), F32),
                        pltpu.VMEM((LANES + BF16_ROWS, 2 * Q_BLOCK), F32)],
        compiler_params=_cparams(("parallel", "parallel")),
        name=mode + "_attention",
    )(z, z, vt, *extras)


def _dilated_tables(S):
    bq, bk = Q_BLOCK, KV_BLOCK
    dds = np.arange(-1, S // bk)
    dist = (dds[:, None, None] * bk + np.arange(bq)[None, None, :] - np.arange(bk)[None, :, None])
    cnt = np.zeros(dist.shape, np.float32)
    for window, dil in DILATED_PAIRS:
        cnt += (dist >= 0) & (dist <= window) & (dist % dil == 0)
    n = len(dds)
    while n > 1 and np.array_equal(cnt[n - 1], cnt[n - 2]):
        n -= 1
    cnt = cnt[:n]
    return jnp.asarray(np.where(cnt > 0, np.log2(np.maximum(cnt, 1.0)), NEG).astype(np.float32))


def _pool_kernel(u_ref, w_ref, b_ref, sc_ref, o_ref):
    u = u_ref[0].astype(F32)
    S, C = u.shape
    row = lax.broadcasted_iota(jnp.int32, (S, C), 0)
    lane = lax.broadcasted_iota(jnp.int32, (S, C), 1)
    grp = lane // (C // len(POOL_WINDOWS))

    def shifted(a, k):
        return jnp.where(row >= k, pltpu.roll(a, k, 0), 0.0)

    acc = u
    win = jnp.zeros_like(u)
    width = 1
    for gi, w in enumerate(POOL_WINDOWS):
        while width < w:
            acc = acc + shifted(acc, width)
            width *= 2
        win = jnp.where(grp == gi, acc, win)
    wl = jnp.zeros_like(row)
    for gi, w in enumerate(POOL_WINDOWS):
        wl = jnp.where(grp == gi, w, wl)
    cnt = jnp.minimum(row + 1, wl).astype(F32)
    d = win / cnt - u
    y = _dot(d.astype(BF16), w_ref[...]) + b_ref[...]
    o_ref[0] = (y * sc_ref[...]).astype(o_ref.dtype)


def _pool_mixer(z, col_block, w_bd, b, scale):
    B, S, _ = z.shape
    C = w_bd.shape[0]
    return pl.pallas_call(
        _pool_kernel,
        grid=(B,),
        in_specs=[pl.BlockSpec((1, S, C), lambda b_: (b_, 0, col_block)),
                  pl.BlockSpec((C, C), lambda b_: (0, 0)),
                  pl.BlockSpec((1, C), lambda b_: (0, 0)),
                  pl.BlockSpec((1, C), lambda b_: (0, 0))],
        out_specs=pl.BlockSpec((1, S, C), lambda b_: (b_, 0, 0)),
        out_shape=jax.ShapeDtypeStruct((B, S, C), BF16),
        compiler_params=_cparams(("parallel",)),
        name="pool_mixer",
    )(z, w_bd, b, scale)


def _outproj_kernel(ya_ref, yb_ref, yc_ref, x_ref, wo_ref, ga_ref, sc_ref, sh_ref, g_ref, wr_ref,
                    br_ref, x1_ref, h_ref, rt_ref):
    na = ya_ref.shape[2]
    nb = yb_ref.shape[2]
    halves = [slice(r * (x_ref.shape[1] // 2), (r + 1) * (x_ref.shape[1] // 2)) for r in range(2)]

    def mix(rows):
        return (_dot(ya_ref[0, rows, :], wo_ref[0:na, :]) + _dot(yb_ref[0, rows, :], wo_ref[na:na + nb, :])
                + _dot(yc_ref[0, rows, :], wo_ref[na + nb:, :]))

    def residual_norm(rows, y):
        x1 = x_ref[0, rows, :] + ga_ref[0] * y
        x1_ref[0, rows, :] = x1
        ms = jnp.mean(x1 * x1, axis=-1, keepdims=True)
        h = x1 * lax.rsqrt(ms + EPS) * g_ref[...]
        h = h * (1.0 + sc_ref[0]) + sh_ref[0]
        hb = h.astype(BF16)
        h_ref[0, rows, :] = hb
        return hb

    ys = [mix(rows) for rows in halves]
    hb = jnp.concatenate([residual_norm(rows, y) for rows, y in zip(halves, ys)], axis=0)

    logits = _dot_nt(wr_ref[...], hb) + br_ref[...]
    tm = logits.shape[1]
    G, E = N_GROUPS, EXPERTS_PER_GROUP
    big = float(LANES)
    grow = lax.broadcasted_iota(jnp.int32, (8, tm), 0)
    grow_f = grow.astype(F32)
    gl = jnp.where(grow < G, logits[G * E:G * E + 8, :], -jnp.inf)
    gmax = jnp.max(gl, axis=0, keepdims=True)
    gidx = jnp.min(jnp.where(gl == gmax, grow_f, big), axis=0, keepdims=True)
    gsum = jnp.sum(jnp.where(grow < G, jnp.exp(gl - gmax), 0.0), axis=0, keepdims=True)
    g_w = 1.0 / gsum
    erow = lax.broadcasted_iota(jnp.int32, (G * E, tm), 0)
    erow_f = erow.astype(F32)
    el = jnp.where((erow // E).astype(F32) == gidx, logits[0:G * E, :], -jnp.inf)
    v1 = jnp.max(el, axis=0, keepdims=True)
    i1 = jnp.min(jnp.where(el == v1, erow_f, big), axis=0, keepdims=True)
    el2 = jnp.where(erow_f == i1, -jnp.inf, el)
    v2 = jnp.max(el2, axis=0, keepdims=True)
    i2 = jnp.min(jnp.where(el2 == v2, erow_f, big), axis=0, keepdims=True)
    t = jnp.exp(v2 - v1)
    w1 = g_w / (1.0 + t)
    w2 = w1 * t
    rt_ref[0] = jnp.concatenate([i1, i2, w1, w2, jnp.zeros((rt_ref.shape[1] - 4, tm), F32)], axis=0)


def _out_projection(ya, yb, yc, x, wo_bf, ga, sc, sh, g, wr, br, tm=DISPATCH_TILE):
    B, S, D = x.shape
    nt = S // tm
    tok = lambda n: pl.BlockSpec((1, tm, n), lambda b, i: (b, i, 0))
    per_b = pl.BlockSpec((1, 1, D), lambda b, i: (b, 0, 0))
    const = lambda shape: pl.BlockSpec(shape, lambda b, i: (0, 0))
    return pl.pallas_call(
        _outproj_kernel,
        grid=(B, nt),
        in_specs=[tok(ya.shape[2]), tok(yb.shape[2]), tok(yc.shape[2]), tok(D), const((D, D)),
                  per_b, per_b, per_b, const((1, D)), const((LANES, D)), const((LANES, 1))],
        out_specs=[tok(D), tok(D),
                   pl.BlockSpec((1, 8, tm), lambda b, i: (b * nt + i, 0, 0))],
        out_shape=[jax.ShapeDtypeStruct((B, S, D), F32),
                   jax.ShapeDtypeStruct((B, S, D), BF16),
                   jax.ShapeDtypeStruct((B * nt, 8, tm), F32)],
        compiler_params=_cparams(("parallel", "parallel")),
        name="out_projection",
    )(ya, yb, yc, x, wo_bf, ga, sc, sh, g, wr, br)


def _rank_kernel(rt_ref, before_ref, ltri_ref, pos_ref, info_ref, tinfo_ref, cnt_ref, off_ref, carry_ref):
    ph, i = pl.program_id(0), pl.program_id(1)
    ne = N_EXPERTS
    rt_t = rt_ref[0]
    tm = rt_t.shape[1]
    expert = lax.broadcasted_iota(jnp.int32, (ne, tm), 0).astype(F32)
    hit1 = expert == rt_t[RT_E:RT_E + 1, :]
    hit2 = expert == rt_t[RT_E + 1:RT_E + 2, :]
    onehot = jnp.where(hit1, 1.0, jnp.where(hit2, 1.0, 0.0))
    n_blk = jnp.ceil(jnp.sum(onehot, axis=1, keepdims=True) * (1.0 / ROW_ALIGN)) * ROW_ALIGN
    n_blk = jnp.broadcast_to(n_blk, (ne, LANES))

    @pl.when((ph == 0) & (i == 0))
    def _():
        cnt_ref[...] = jnp.zeros_like(cnt_ref)

    @pl.when(ph == 0)
    def _():
        cnt_ref[...] += n_blk

    @pl.when((ph == 1) & (i == 0))
    def _():
        ntile = jnp.ceil(cnt_ref[...] * (1.0 / MOE_TILE))
        off_tiles = _dot(ltri_ref[...], ntile.astype(BF16))
        off_ref[...] = off_tiles * MOE_TILE
        info_ref[0] = cnt_ref[...]
        info_ref[1] = off_tiles
        carry_ref[...] = jnp.zeros_like(carry_ref)

    @pl.when(ph == 1)
    def _():
        dst0 = carry_ref[...] + off_ref[...]
        loc0 = _dot(ltri_ref[...], (n_blk * (1.0 / ROW_ALIGN)).astype(BF16)) * ROW_ALIGN
        within = _dot(onehot.astype(BF16), before_ref[...])
        in_sorted = within + dst0[:, 0:1]
        in_buffer = within + loc0[:, 0:1]
        rows = [jnp.sum(jnp.where(hit, v, 0.0), axis=0, keepdims=True)
                for v in (in_sorted, in_buffer) for hit in (hit1, hit2)]
        rows.append(jnp.zeros((pos_ref.shape[1] - len(rows), tm), F32))
        pos_ref[0] = jnp.concatenate(rows, axis=0).astype(jnp.int32)
        tinfo_ref[0, 0] = dst0
        tinfo_ref[0, 1] = n_blk
        tinfo_ref[0, 2] = loc0
        carry_ref[...] += n_blk


def _rank(rt):
    n_tok_tiles, _, tm = rt.shape
    T = n_tok_tiles * tm
    idx = np.arange(tm)
    before = jnp.asarray(idx[:, None] < idx[None, :], BF16)
    ex = np.arange(N_EXPERTS)
    ltri = jnp.asarray(ex[None, :] < ex[:, None], BF16)
    stat = pltpu.VMEM((N_EXPERTS, LANES), F32)
    return pl.pallas_call(
        _rank_kernel,
        grid=(2, T // tm),
        in_specs=[pl.BlockSpec((1, 8, tm), lambda ph, i: (i, 0, 0)),
                  pl.BlockSpec((tm, tm), lambda ph, i: (0, 0)),
                  pl.BlockSpec((N_EXPERTS, N_EXPERTS), lambda ph, i: (0, 0))],
        out_specs=[pl.BlockSpec((1, 8, tm), lambda ph, i: (i * ph, 0, 0)),
                   pl.BlockSpec((2, N_EXPERTS, LANES), lambda ph, i: (0, 0, 0)),
                   pl.BlockSpec((1, 3, N_EXPERTS, LANES), lambda ph, i: (i * ph, 0, 0, 0))],
        out_shape=[jax.ShapeDtypeStruct((T // tm, 8, tm), jnp.int32),
                   jax.ShapeDtypeStruct((2, N_EXPERTS, LANES), F32),
                   jax.ShapeDtypeStruct((T // tm, 3, N_EXPERTS, LANES), F32)],
        scratch_shapes=[stat, stat, stat],
        compiler_params=_cparams(("arbitrary", "arbitrary")),
        name="moe_rank",
    )(rt, before, ltri)


def _block_copies(n, src_of, dst_of, sem, act):
    def whole(j, _):
        off = pl.multiple_of(j * COPY_CHUNK, COPY_CHUNK)
        act(pltpu.make_async_copy(src_of(off, COPY_CHUNK), dst_of(off, COPY_CHUNK), sem))
        return 0
    lax.fori_loop(0, n // COPY_CHUNK, whole, 0)
    for b in range(COPY_CHUNK.bit_length() - 2, ROW_ALIGN.bit_length() - 2, -1):
        size = 1 << b

        @pl.when((n >> b) & 1 == 1)
        def _():
            off = pl.multiple_of((n >> (b + 1)) << (b + 1), ROW_ALIGN)
            act(pltpu.make_async_copy(src_of(off, size), dst_of(off, size), sem))


def _dispatch_kernel(dst_ref, nblk_ref, loc_ref, tail0_ref, tailn_ref, nv_ref, h_ref, lpos_ref, xs_ref,
                     obuf, zbuf, sems, zsem):
    i = pl.program_id(0)
    n = pl.num_programs(0)
    nc, tm = obuf.shape[1], h_ref.shape[0]
    R = obuf.shape[2]
    slot = i % 2

    def tile_copies(tile, sl, act):
        def body(e, _):
            c = tile * N_EXPERTS + e
            loc = pl.multiple_of(loc_ref[c], ROW_ALIGN)
            dst = pl.multiple_of(dst_ref[c], ROW_ALIGN)
            _block_copies(nblk_ref[c],
                          lambda off, size: obuf.at[sl, :, pl.ds(loc + off, size)],
                          lambda off, size: xs_ref.at[:, pl.ds(dst + off, size)],
                          sems.at[sl], act)
            return 0
        lax.fori_loop(0, N_EXPERTS, body, 0)

    def zero_fill(act):
        def body(e, _):
            t0 = pl.multiple_of(tail0_ref[e], ROW_ALIGN)
            _block_copies(tailn_ref[e], lambda off, size: zbuf.at[:, pl.ds(0, size)],
                          lambda off, size: xs_ref.at[:, pl.ds(t0 + off, size)], zsem, act)
            return 0
        lax.fori_loop(0, N_EXPERTS, body, 0)

        def unused(j, _):
            j0 = pl.multiple_of(j * MOE_TILE, MOE_TILE)
            act(pltpu.make_async_copy(zbuf, xs_ref.at[:, pl.ds(j0, MOE_TILE)], zsem))
            return 0
        lax.fori_loop(nv_ref[0], xs_ref.shape[1] // MOE_TILE, unused, 0)

    @pl.when(i == 0)
    def _():
        zbuf[...] = jnp.zeros_like(zbuf)
        zero_fill(lambda cp: cp.start())

    @pl.when(i >= 2)
    def _():
        tile_copies(i - 2, slot, lambda cp: cp.wait())

    hb = h_ref[...]
    lp = lpos_ref[0].astype(F32)
    l1, l2 = lp[2:3, :], lp[3:4, :]
    blk = MXU_DIM
    for jb in range(R // blk):
        r = (lax.broadcasted_iota(jnp.int32, (blk, tm), 0) + jb * blk).astype(F32)
        sel = jnp.where(r == l1, 1.0, jnp.where(r == l2, 1.0, 0.0)).astype(BF16)
        rows = _dot(sel, hb)
        for s in range(nc):
            obuf[slot, s, jb * blk:(jb + 1) * blk, :] = rows[:, s * LANES:(s + 1) * LANES]
    tile_copies(i, slot, lambda cp: cp.start())

    @pl.when(i == n - 1)
    def _():
        @pl.when(i >= 1)
        def _():
            tile_copies(i - 1, 1 - slot, lambda cp: cp.wait())
        tile_copies(i, slot, lambda cp: cp.wait())
        zero_fill(lambda cp: cp.wait())


def _dispatch(dst0, nblk, loc0, tail0, tailn, n_valid, h, pos, n_rows):
    T, D = h.shape
    NC = D // LANES
    tm = pos.shape[2]
    R = 2 * tm + N_EXPERTS * ROW_ALIGN
    R = -(-R // MXU_DIM) * MXU_DIM
    return pl.pallas_call(
        _dispatch_kernel,
        grid_spec=pltpu.PrefetchScalarGridSpec(
            num_scalar_prefetch=6,
            grid=(T // tm,),
            in_specs=[pl.BlockSpec((tm, D), lambda i, *_: (i, 0)),
                      pl.BlockSpec((1, pos.shape[1], tm), lambda i, *_: (i, 0, 0))],
            out_specs=pl.BlockSpec(memory_space=pl.ANY),
            scratch_shapes=[pltpu.VMEM((2, NC, R, LANES), F32), pltpu.VMEM((NC, MOE_TILE, LANES), F32),
                            pltpu.SemaphoreType.DMA((2,)), pltpu.SemaphoreType.DMA]),
        out_shape=jax.ShapeDtypeStruct((NC, n_rows, LANES), F32),
        compiler_params=_cparams(("arbitrary",)),
        name="moe_dispatch",
    )(dst0, nblk, loc0, tail0, tailn, n_valid, h, pos)


def _experts_kernel(te_ref, nv_ref, xs_ref, w1_ref, w3_ref, w2_ref, ys_ref, w1b, w3b, w2b):
    j = pl.program_id(0)
    e = te_ref[j]
    e_prev = te_ref[jnp.maximum(j - 1, 0)]

    @pl.when((j == 0) | (e != e_prev))
    def _():
        w1b[...] = w1_ref[0].astype(BF16)
        w3b[...] = w3_ref[0].astype(BF16)
        w2b[...] = w2_ref[0].astype(BF16)

    @pl.when(j < nv_ref[0])
    def _():
        nc = xs_ref.shape[0]
        half = xs_ref.shape[1] // 2

        def up(rows):
            h = jnp.concatenate([xs_ref[s, rows, :] for s in range(nc)], axis=1).astype(BF16)
            return _dot(h, w1b[...]), _dot(h, w3b[...])

        def down(rows, ab):
            a, b = ab
            hid = (a * (1.0 / (1.0 + jnp.exp(-a))) * b).astype(BF16)
            y = _dot(hid, w2b[...])
            for s in range(nc):
                ys_ref[s, rows, :] = y[:, s * LANES:(s + 1) * LANES]

        top, bottom = slice(0, half), slice(half, 2 * half)
        ab_top = up(top)
        ab_bottom = up(bottom)
        down(top, ab_top)
        down(bottom, ab_bottom)


def _experts(tile_expert, n_valid, xs, w1, w3, w2):
    NC, P, _ = xs.shape
    NE, D, F = w1.shape
    nt = P // MOE_TILE
    tile = lambda j, te, nv: (0, jnp.minimum(j, nv[0] - 1), 0)
    wspec = lambda shape: pl.BlockSpec(shape, lambda j, te, nv: (te[j], 0, 0))
    return pl.pallas_call(
        _experts_kernel,
        grid_spec=pltpu.PrefetchScalarGridSpec(
            num_scalar_prefetch=2,
            grid=(nt,),
            in_specs=[pl.BlockSpec((NC, MOE_TILE, LANES), tile),
                      wspec((1, D, F)), wspec((1, D, F)), wspec((1, F, D))],
            out_specs=pl.BlockSpec((NC, MOE_TILE, LANES), tile),
            scratch_shapes=[pltpu.VMEM((D, F), BF16), pltpu.VMEM((D, F), BF16),
                            pltpu.VMEM((F, D), BF16)]),
        out_shape=jax.ShapeDtypeStruct((NC, P, LANES), F32),
        input_output_aliases={2: 0},
        compiler_params=_cparams(("arbitrary",)),
        name="moe_experts",
    )(tile_expert, n_valid, xs, w1, w3, w2)


def _combine_kernel(pos_ref, x_ref, rt_ref, ga_ref, ys_ref, o_ref, buf, sems):
    i = pl.program_id(0)
    n = pl.num_programs(0)
    tm = x_ref.shape[0]
    nc = buf.shape[1]

    def row_copy(tile, slot, t, k):
        src = ys_ref.at[:, pos_ref[tile * (2 * tm) + k * tm + t]]
        return pltpu.make_async_copy(src, buf.at[slot, :, k * tm + t], sems.at[slot])

    def issue(tile, slot):
        def body(t, _):
            row_copy(tile, slot, t, 0).start(priority=0)
            row_copy(tile, slot, t, 1).start(priority=1)
            return 0
        lax.fori_loop(0, tm, body, 0, unroll=8)

    @pl.when(i == 0)
    def _():
        issue(0, 0)

    @pl.when(i + 1 < n)
    def _():
        issue(i + 1, (i + 1) % 2)

    slot = i % 2

    def drain(t, _):
        row_copy(i, slot, t, 0).wait()
        row_copy(i, slot, t, 1).wait()
        return 0

    lax.fori_loop(0, tm, drain, 0, unroll=8)
    rt_t = rt_ref[0]
    rt = jnp.concatenate([rt_t, jnp.zeros((LANES - rt_t.shape[0], tm), F32)], axis=0).T
    w1 = rt[:, RT_W:RT_W + 1]
    w2 = rt[:, RT_W + 1:RT_W + 2]
    for s in range(nc):
        cols = slice(s * LANES, (s + 1) * LANES)
        y = w1 * buf[slot, s, 0:tm, :] + w2 * buf[slot, s, tm:2 * tm, :]
        o_ref[:, cols] = x_ref[:, cols] + ga_ref[0][:, cols] * y


def _combine(pos_flat, x1, rt, ga, ys):
    B, S, D = x1.shape
    NC = ys.shape[0]
    tm = rt.shape[2]
    nt = S // tm
    return pl.pallas_call(
        _combine_kernel,
        grid_spec=pltpu.PrefetchScalarGridSpec(
            num_scalar_prefetch=1,
            grid=(B * nt,),
            in_specs=[pl.BlockSpec((tm, D), lambda i, pos: (i, 0)),
                      pl.BlockSpec((1, rt.shape[1], tm), lambda i, pos: (i, 0, 0)),
                      pl.BlockSpec((1, 1, D), lambda i, pos: (i // nt, 0, 0)),
                      pl.BlockSpec(memory_space=pl.ANY)],
            out_specs=pl.BlockSpec((tm, D), lambda i, pos: (i, 0)),
            scratch_shapes=[pltpu.VMEM((2, NC, 2 * tm, LANES), F32),
                            pltpu.SemaphoreType.DMA((2,))]),
        out_shape=jax.ShapeDtypeStruct((B * S, D), F32),
        compiler_params=_cparams(("arbitrary",)),
        name="moe_combine",
    )(pos_flat, x1.reshape(B * S, D), rt, ga, ys).reshape(B, S, D)


def _moe(h3, rt, x1, ga, w1, w3, w2, first_expert):
    n_tok_tiles = rt.shape[0]
    T = n_tok_tiles * rt.shape[2]
    n_tiles = -(-(2 * T + n_tok_tiles * N_EXPERTS * ROW_ALIGN) // MOE_TILE) + N_EXPERTS
    pos, info, tinfo = _rank(rt)
    pos_flat = pos[:, :2, :].reshape(-1)
    counts = info[0, :, 0]
    starts = info[1, :, 0]
    ends = starts + jnp.ceil(counts * (1.0 / MOE_TILE))
    tile_ids = jnp.arange(n_tiles, dtype=F32)
    tile_expert = jnp.minimum(jnp.sum(tile_ids[:, None] >= ends[None, :], axis=1), N_EXPERTS - 1)
    n_valid = ends[N_EXPERTS - 1:].astype(jnp.int32)
    as_ints = lambda a: a.astype(jnp.int32).reshape(-1)
    xs = _dispatch(as_ints(tinfo[:, 0, :, 0]), as_ints(tinfo[:, 1, :, 0]),
                   as_ints(tinfo[:, 2, :, 0]), as_ints(starts * MOE_TILE + counts),
                   as_ints(ends * MOE_TILE - starts * MOE_TILE - counts), n_valid,
                   h3.reshape(T, h3.shape[-1]), pos, n_tiles * MOE_TILE)
    ys = _experts(tile_expert.astype(jnp.int32) + first_expert, n_valid, xs, w1, w3, w2)
    return _combine(pos_flat, x1, rt, ga, ys)


def kernel(x, c, w_mod, b_mod, g_norm1, w_in, gq_a, gk_a, lam_a, g_sub_a, w_pool, b_pool, pool_scale, gq_c, gk_c, w_out, g_norm2, w_rg, b_rg, w_re, b_re, w1, w3, w2):
    B, S, D = x.shape
    L = w_mod.shape[0]
    a_width = D // 2
    pool_width = D // 4
    c_width = D // 4
    W = MXU_DIM
    reps = W // HEAD_DIM

    inv = 1.0 / (ROPE_THETA ** (jnp.arange(0, HEAD_DIM, 2, dtype=F32) / HEAD_DIM))
    ang = jnp.arange(S, dtype=F32)[:, None] * inv[None, :]
    ang = jnp.concatenate([ang, ang], axis=-1)
    cos_h, sin_h = jnp.cos(ang), jnp.sin(ang)
    first = jnp.arange(HEAD_DIM) < HEAD_DIM // 2
    cos_t = jnp.tile(cos_h, (1, reps))
    sa_t = jnp.tile(jnp.where(first[None, :], 0.0, sin_h), (1, reps))
    sb_t = jnp.tile(jnp.where(first[None, :], -sin_h, 0.0), (1, reps))
    head_of = np.arange(W) // HEAD_DIM
    bd = jnp.asarray((head_of[:, None] == head_of[None, :]) / HEAD_DIM, BF16)

    qa0, ka0, va0 = 0, a_width, 2 * a_width
    ub0 = 3 * a_width
    qc0, kc0, vc0 = ub0 + pool_width, ub0 + pool_width + c_width, ub0 + pool_width + 2 * c_width
    chunks = lambda lo_, hi_: list(range(lo_ // W, hi_ // W))
    z_chunks = tuple([(j, True) for j in chunks(qa0, va0)] + [(j, False) for j in chunks(ub0, qc0)]
                     + [(j, True) for j in chunks(qc0, vc0)])
    v_chunks = tuple(chunks(va0, ub0) + chunks(vc0, w_in.shape[2]))
    zq_a, zk_a = 0, a_width // LANES
    z_ub = 2 * a_width // pool_width
    zq_c = (2 * a_width + pool_width) // LANES
    zk_c = zq_c + c_width // LANES
    scale = HEAD_DIM ** -0.5 * math.log2(math.e)
    bias_t = _dilated_tables(S)

    mod = _modulation(c, w_mod, b_mod)
    w_in_bf = w_in.astype(BF16)
    for l in range(L):
        sh1, sc1, ga1, sh2, sc2, ga2 = [m[:, None, :] for m in jnp.split(mod[l], N_MOD, axis=-1)]
        gain = jnp.ones((w_in.shape[2],), F32)
        gain = gain.at[qa0:ka0].set(jnp.tile(gq_a[l], a_width // HEAD_DIM) * scale)
        gain = gain.at[ka0:va0].set(jnp.tile(gk_a[l], a_width // HEAD_DIM))
        gain = gain.at[qc0:kc0].set(jnp.tile(gq_c[l], c_width // HEAD_DIM) * scale)
        gain = gain.at[kc0:vc0].set(jnp.tile(gk_c[l], c_width // HEAD_DIM))
        z, vt = _in_projection(x, sc1, sh1, g_norm1[l][None, :], w_in_bf, l,
                               gain[None, :], cos_t, sa_t, sb_t, bd, z_chunks, v_chunks)

        lam_init = 0.8 - 0.6 * math.exp(-0.3 * l)
        n_a = a_width // LANES
        ya = _attention("diff", z, vt, zq_a, zk_a, 0, n_a,
                        [lam_a[l], g_sub_a[l][:, None]],
                        [pl.BlockSpec(lam_a[l].shape, lambda b, h: (0, 0)),
                         pl.BlockSpec((LANES, 1), lambda b, h: (0, 0))], lam_init=lam_init)
        n_c = c_width // LANES
        yc = _attention("dil", z, vt, zq_c, zk_c, n_a, n_c,
                        [bias_t], [pl.BlockSpec(bias_t.shape, lambda b, h: (0, 0, 0))])
        w_bd = jax.scipy.linalg.block_diag(*[w_pool[l, g] for g in range(w_pool.shape[1])])
        yb = _pool_mixer(z, z_ub, w_bd.astype(BF16), b_pool[l].reshape(1, -1),
                         pool_scale[l][None, :])

        wr = jnp.zeros((LANES, D), F32)
        wr = wr.at[:N_EXPERTS].set(w_re[l].transpose(0, 2, 1).reshape(N_EXPERTS, D))
        wr = wr.at[N_EXPERTS:N_EXPERTS + N_GROUPS].set(w_rg[l].T)
        br = jnp.zeros((LANES, 1), F32)
        br = br.at[:N_EXPERTS, 0].set(b_re[l].reshape(-1))
        br = br.at[N_EXPERTS:N_EXPERTS + N_GROUPS, 0].set(b_rg[l])
        x1, h3, rt = _out_projection(ya, yb, yc, x, w_out[l].astype(BF16), ga1, sc2, sh2,
                                     g_norm2[l][None, :], wr.astype(BF16), br)
        F = w1.shape[-1]
        x = _moe(h3, rt, x1, ga2, w1.reshape(L * N_EXPERTS, D, F), w3.reshape(L * N_EXPERTS, D, F),
                 w2.reshape(L * N_EXPERTS, F, D), l * N_EXPERTS)
    return x
```
